```python
import math
import jax, jax.numpy as jnp
from jax import lax
import numpy as np

D_MODEL = 1024
BATCH = 8
SEQ = 4096
DEPTH = 1
DEC_BATCH = 128
DEC_SEQ = 4
PAST_LEN = 8192
PAGE_SIZE = 128

D_MIX = D_MODEL
D_A = D_MIX // 2
N_BLK_A = 8
BLK_W_A = D_A // N_BLK_A
CONV_W = 4
RG_C = 8.0
D_B = D_MIX - D_A
HEAD_DIM = 64
N_HEADS_B = D_B // HEAD_DIM
DILATED_CFG = ((128, 1), (512, 4), (2048, 16))
WIN_MAX = max(w for w, _ in DILATED_CFG)
Q_BLK = 128
ROPE_THETA = 10000.0
N_GROUPS = 4
N_EXP_PER_GROUP = 4
N_EXPERTS = N_GROUPS * N_EXP_PER_GROUP
TOP_K_INNER = 2
D_EXPERT = 512
DN_ALPHA = (2.0 * DEPTH) ** 0.25
DN_BETA = (8.0 * DEPTH) ** -0.25
LN_EPS = 1e-5
NEG_INF = -1e30

kernel_name = "hymba_rglru_dilated_hmoe_step"

F32 = jnp.float32


def layer_norm(x, g, b):
    x32 = x.astype(F32)
    mu = x32.mean(-1, keepdims=True)
    var = jnp.square(x32 - mu).mean(-1, keepdims=True)
    return ((x32 - mu) * lax.rsqrt(var + LN_EPS) * g + b).astype(x.dtype)


def rms_norm(y, g):
    y32 = y.astype(F32)
    return (y32 * lax.rsqrt(jnp.square(y32).mean(-1, keepdims=True) + LN_EPS) * g).astype(y.dtype)


def rope(t, pos):
    half = HEAD_DIM // 2
    inv = ROPE_THETA ** (-jnp.arange(half, dtype=F32) * 2.0 / HEAD_DIM)
    ang = pos.astype(F32)[:, None] * inv[None, :]
    cos = jnp.cos(ang)[None, :, None, :]
    sin = jnp.sin(ang)[None, :, None, :]
    t32 = t.astype(F32)
    t1, t2 = t32[..., :half], t32[..., half:]
    return jnp.concatenate([t1 * cos - t2 * sin, t2 * cos + t1 * sin], -1).astype(t.dtype)


def rglru_branch(xa, ga, conv_state, h0, conv_w, conv_b, w_rg_a, b_rg_a, w_rg_x, b_rg_x, rg_lambda):
    b, t, _ = xa.shape
    xp = jnp.concatenate([conv_state.astype(xa.dtype), xa], axis=1)
    xc = conv_b + sum(xp[:, j:j + t] * conv_w[j] for j in range(CONV_W))
    new_conv = xp[:, -(CONV_W - 1):]
    xb = xc.reshape(b, t, N_BLK_A, BLK_W_A)
    r = jax.nn.sigmoid(jnp.einsum('bthi,hij->bthj', xb, w_rg_a, preferred_element_type=F32) + b_rg_a).reshape(b, t, D_A)
    i = jax.nn.sigmoid(jnp.einsum('bthi,hij->bthj', xb, w_rg_x, preferred_element_type=F32) + b_rg_x).reshape(b, t, D_A)
    log_a = -RG_C * r * jax.nn.softplus(-rg_lambda.astype(F32))
    a = jnp.exp(log_a)
    u = jnp.sqrt(-jnp.expm1(2.0 * log_a)) * (i * xc.astype(F32))
    u = u.at[:, 0].add(a[:, 0] * h0.astype(F32))

    def combine(left, right):
        a1, b1 = left
        a2, b2 = right
        return a1 * a2, a2 * b1 + b2

    _, h = lax.associative_scan(combine, (a, u), axis=1)
    y = (h * jax.nn.gelu(ga.astype(F32))).astype(xa.dtype)
    return y, new_conv, h[:, -1]


def dilated_branch_prompt(q, k, v, window, dil):
    b, s, h, hd = q.shape
    n_keys = window // dil
    sd = s // dil
    nb = -(-sd // Q_BLK)
    sp = nb * Q_BLK

    def strided(t):
        return t.reshape(b, sd, dil, h, hd).transpose(0, 2, 1, 3, 4)

    qs = jnp.pad(strided(q), ((0, 0), (0, 0), (0, sp - sd), (0, 0), (0, 0))).reshape(b, dil, nb, Q_BLK, h, hd)

    def kv_blocks(t):
        tp = jnp.pad(strided(t), ((0, 0), (0, 0), (Q_BLK, sp - sd), (0, 0), (0, 0))).reshape(b, dil, nb + 1, Q_BLK, h, hd)
        return jnp.concatenate([tp[:, :, :-1], tp[:, :, 1:]], axis=3)

    kb, vb = kv_blocks(k), kv_blocks(v)
    sc = jnp.einsum('bgnqhd,bgnkhd->bgnhqk', qs, kb, preferred_element_type=F32) * (hd ** -0.5)
    qi = jnp.arange(Q_BLK)
    ki = jnp.arange(2 * Q_BLK)
    dist = Q_BLK + qi[:, None] - ki[None, :]
    band = (dist >= 0) & (dist <= n_keys)
    key_row = jnp.arange(nb)[:, None] * Q_BLK + ki[None, :] - Q_BLK
    mask = band[None] & (key_row >= 0)[:, None, :]
    sc = jnp.where(mask[:, None], sc, NEG_INF)
    m = sc.max(-1)
    p = jnp.exp(sc - m[..., None])
    l = p.sum(-1)
    acc = jnp.einsum('bgnhqk,bgnkhd->bgnqhd', p, vb.astype(F32))

    def unstride_stat(t):
        t = t.transpose(0, 1, 2, 4, 3).reshape(b, dil, sp, h)[:, :, :sd]
        return t.transpose(0, 2, 1, 3).reshape(b, s, h)

    acc = acc.reshape(b, dil, sp, h, hd)[:, :, :sd].transpose(0, 2, 1, 3, 4).reshape(b, s, h, hd)
    return unstride_stat(m), unstride_stat(l), acc


def dilated_branch_sample(q, k_full, v_full, window, dil):
    b, t, h, hd = q.shape
    n_buf = k_full.shape[1] - t
    n_keys = window // dil
    idx = n_buf + jnp.arange(t)[:, None] - dil * jnp.arange(n_keys + 1)[None, :]
    valid = idx >= 0
    idx = jnp.maximum(idx, 0)
    kg = k_full[:, idx]
    vg = v_full[:, idx]
    sc = jnp.einsum('bthd,btkhd->bthk', q, kg, preferred_element_type=F32) * (hd ** -0.5)
    sc = jnp.where(valid[None, :, None, :], sc, NEG_INF)
    m = sc.max(-1)
    p = jnp.exp(sc - m[..., None])
    l = p.sum(-1)
    acc = jnp.einsum('bthk,btkhd->bthd', p, vg.astype(F32))
    return m, l, acc


def merge_branches(stats):
    m = jnp.stack([s[0] for s in stats])
    l = jnp.stack([s[1] for s in stats])
    acc = jnp.stack([s[2] for s in stats])
    w = jnp.exp(m - m.max(0))
    return (w[..., None] * acc).sum(0) / (w * l).sum(0)[..., None]


def hier_moe(u, w_router_group, b_router_group, w_router_expert, b_router_expert, w_exp_gate, w_exp_up, w_exp_down):
    b, t, _ = u.shape
    g_logits = jnp.einsum('btd,dg->btg', u, w_router_group, preferred_element_type=F32) + b_router_group
    g_prob = jax.nn.softmax(g_logits, -1)
    _, g_idx = lax.top_k(g_logits, 1)
    p_group = jnp.take_along_axis(g_prob, g_idx, -1)
    e_logits = jnp.einsum('btd,gde->btge', u, w_router_expert, preferred_element_type=F32) + b_router_expert
    e_logits = jnp.take_along_axis(e_logits, g_idx[..., None], axis=2)[:, :, 0]
    top_v, top_i = lax.top_k(e_logits, TOP_K_INNER)
    w_top = jax.nn.softmax(top_v, -1) * p_group
    expert_id = g_idx * N_EXP_PER_GROUP + top_i
    comb = jnp.sum(jax.nn.one_hot(expert_id, N_EXPERTS, dtype=F32) * w_top[..., None], axis=-2)
    comb = comb.reshape(b, t, N_GROUPS, N_EXP_PER_GROUP).astype(u.dtype)
    wg = w_exp_gate.reshape(N_GROUPS, N_EXP_PER_GROUP, D_MODEL, D_EXPERT)
    wu = w_exp_up.reshape(N_GROUPS, N_EXP_PER_GROUP, D_MODEL, D_EXPERT)
    wd = w_exp_down.reshape(N_GROUPS, N_EXP_PER_GROUP, D_EXPERT, D_MODEL)
    out = jnp.zeros(u.shape, F32)
    for g in range(N_GROUPS):
        hg = jnp.einsum('btd,edf->btef', u, wg[g])
        hu = jnp.einsum('btd,edf->btef', u, wu[g])
        act = jax.nn.silu(hg) * hu * comb[:, :, g, :, None]
        out = out + jnp.einsum('btef,efd->btd', act, wd[g], preferred_element_type=F32)
    return out.astype(u.dtype)


def decoder_layer(x, c, pos, conv_state, h0, k_buf, v_buf, w_ada, b_ada, w_in, conv_w, conv_b,
                  w_rg_a, b_rg_a, w_rg_x, b_rg_x, rg_lambda, g_norm_a, g_norm_b, w_out, ln1_g, ln1_b,
                  w_router_group, b_router_group, w_router_expert, b_router_expert,
                  w_exp_gate, w_exp_up, w_exp_down, ln2_g, ln2_b):
    b, t, _ = x.shape
    mod = jax.nn.silu(c) @ w_ada + b_ada
    sh1, sc1, gt1, sh2, sc2, gt2 = jnp.split(mod[:, None, :], 6, axis=-1)
    u = x * (1 + sc1) + sh1
    proj = u @ w_in
    xa, ga, q, k, v = jnp.split(proj, [D_A, 2 * D_A, 2 * D_A + D_B, 2 * D_A + 2 * D_B], axis=-1)
    y_a, new_conv, h_last = rglru_branch(xa, ga, conv_state, h0, conv_w, conv_b,
                                         w_rg_a, b_rg_a, w_rg_x, b_rg_x, rg_lambda)
    q = rope(q.reshape(b, t, N_HEADS_B, HEAD_DIM), pos)
    k = rope(k.reshape(b, t, N_HEADS_B, HEAD_DIM), pos)
    v = v.reshape(b, t, N_HEADS_B, HEAD_DIM)
    if k_buf is None:
        k_all, v_all = k, v
        stats = [dilated_branch_prompt(q, k, v, w, d) for w, d in DILATED_CFG]
    else:
        k_all = jnp.concatenate([k_buf.astype(k.dtype), k], axis=1)
        v_all = jnp.concatenate([v_buf.astype(v.dtype), v], axis=1)
        stats = [dilated_branch_sample(q, k_all, v_all, w, d) for w, d in DILATED_CFG]
    keep = min(WIN_MAX, k_all.shape[1])
    new_k, new_v = k_all[:, -keep:], v_all[:, -keep:]
    y_b = merge_branches(stats).astype(x.dtype).reshape(b, t, D_B)
    mixed = jnp.concatenate([rms_norm(y_a, g_norm_a), rms_norm(y_b, g_norm_b)], axis=-1) @ w_out
    x = layer_norm(DN_ALPHA * x + gt1 * mixed, ln1_g, ln1_b)
    u2 = x * (1 + sc2) + sh2
    ffn = hier_moe(u2, w_router_group, b_router_group, w_router_expert, b_router_expert,
                   w_exp_gate, w_exp_up, w_exp_down)
    x = layer_norm(DN_ALPHA * x + gt2 * ffn, ln2_g, ln2_b)
    return x, new_conv, h_last, new_k, new_v


def setup_inputs(seed: int = 0) -> dict:
    key = jax.random.key(seed)
    ks = iter(jax.random.split(key, 40))

    def nrm(shape, scale):
        return jax.random.normal(next(ks), shape, F32) * scale

    wb = min(WIN_MAX, PAST_LEN)
    u_a = jax.random.uniform(next(ks), (D_A,), F32, 0.9, 0.999) ** (1.0 / RG_C)
    rg_lambda = jnp.log(u_a) - jnp.log1p(-u_a)
    return {
        "x_prompt": nrm((BATCH, SEQ, D_MODEL), 1.0),
        "x_sample": nrm((DEC_BATCH, DEC_SEQ, D_MODEL), 1.0),
        "state_conv": nrm((DEC_BATCH, CONV_W - 1, D_A), 1.0),
        "state_rglru": nrm((DEC_BATCH, D_A), 0.5),
        "cache_win_k": nrm((DEC_BATCH, wb, N_HEADS_B, HEAD_DIM), 1.0),
        "cache_win_v": nrm((DEC_BATCH, wb, N_HEADS_B, HEAD_DIM), 1.0),
        "c_prompt": nrm((BATCH, D_MODEL), 1.0),
        "c_sample": nrm((DEC_BATCH, D_MODEL), 1.0),
        "w_ada": nrm((D_MODEL, 6 * D_MODEL), D_MODEL ** -0.5),
        "b_ada": nrm((6 * D_MODEL,), 0.02),
        "w_in": nrm((D_MODEL, 2 * D_A + 3 * D_B), D_MODEL ** -0.5),
        "conv_w": nrm((CONV_W, D_A), CONV_W ** -0.5),
        "conv_b": nrm((D_A,), 0.02),
        "w_rg_a": nrm((N_BLK_A, BLK_W_A, BLK_W_A), BLK_W_A ** -0.5),
        "b_rg_a": nrm((N_BLK_A, BLK_W_A), 0.02),
        "w_rg_x": nrm((N_BLK_A, BLK_W_A, BLK_W_A), BLK_W_A ** -0.5),
        "b_rg_x": nrm((N_BLK_A, BLK_W_A), 0.02),
        "rg_lambda": rg_lambda,
        "g_norm_a": 1.0 + nrm((D_A,), 0.02),
        "g_norm_b": 1.0 + nrm((D_B,), 0.02),
        "w_out": nrm((D_MIX, D_MODEL), DN_BETA * D_MIX ** -0.5),
        "ln1_g": 1.0 + nrm((D_MODEL,), 0.02),
        "ln1_b": nrm((D_MODEL,), 0.02),
        "w_router_group": nrm((D_MODEL, N_GROUPS), D_MODEL ** -0.5),
        "b_router_group": nrm((N_GROUPS,), 0.01),
        "w_router_expert": nrm((N_GROUPS, D_MODEL, N_EXP_PER_GROUP), D_MODEL ** -0.5),
        "b_router_expert": nrm((N_GROUPS, N_EXP_PER_GROUP), 0.01),
        "w_exp_gate": nrm((N_EXPERTS, D_MODEL, D_EXPERT), D_MODEL ** -0.5),
        "w_exp_up": nrm((N_EXPERTS, D_MODEL, D_EXPERT), D_MODEL ** -0.5),
        "w_exp_down": nrm((N_EXPERTS, D_EXPERT, D_MODEL), DN_BETA * D_EXPERT ** -0.5),
        "ln2_g": 1.0 + nrm((D_MODEL,), 0.02),
        "ln2_b": nrm((D_MODEL,), 0.02),
    }


def reference(x_prompt, x_sample, state_conv, state_rglru, cache_win_k, cache_win_v, c_prompt, c_sample,
              w_ada, b_ada, w_in, conv_w, conv_b, w_rg_a, b_rg_a, w_rg_x, b_rg_x, rg_lambda,
              g_norm_a, g_norm_b, w_out, ln1_g, ln1_b, w_router_group, b_router_group,
              w_router_expert, b_router_expert, w_exp_gate, w_exp_up, w_exp_down, ln2_g, ln2_b):
    weights = (w_ada, b_ada, w_in, conv_w, conv_b, w_rg_a, b_rg_a, w_rg_x, b_rg_x, rg_lambda,
               g_norm_a, g_norm_b, w_out, ln1_g, ln1_b, w_router_group, b_router_group,
               w_router_expert, b_router_expert, w_exp_gate, w_exp_up, w_exp_down, ln2_g, ln2_b)
    b_p, s_p, _ = x_prompt.shape
    t_s = x_sample.shape[1]
    pos_p = jnp.arange(s_p)
    pos_s = PAST_LEN + jnp.arange(t_s)
    conv0 = jnp.zeros((b_p, CONV_W - 1, D_A), x_prompt.dtype)
    h0 = jnp.zeros((b_p, D_A), F32)
    y_prompt, y_sample = x_prompt, x_sample
    for layer in range(DEPTH):
        y_prompt, conv_p, h_p, k_p, v_p = decoder_layer(
            y_prompt, c_prompt, pos_p, conv0, h0, None, None, *weights)
        y_sample, conv_s, h_s, k_s, v_s = decoder_layer(
            y_sample, c_sample, pos_s, state_conv, state_rglru, cache_win_k, cache_win_v, *weights)
    return (y_prompt, y_sample, conv_p, h_p, k_p, v_p, conv_s, h_s, k_s, v_s)
```

```python
import functools
import math

import jax
import jax.numpy as jnp
from jax import lax
from jax.experimental import pallas as pl
from jax.experimental.pallas import tpu as pltpu

F32 = jnp.float32
BF16 = jnp.bfloat16

D_MODEL = 1024
D_A = 512
N_BLK_A = 8
BLK_W_A = D_A // N_BLK_A
CONV_W = 4
RG_C = 8.0
D_B = 512
HEAD_DIM = 64
N_HEADS_B = D_B // HEAD_DIM
DILATED_CFG = ((128, 1), (512, 4), (2048, 16))
WIN_MAX = 2048
N_KEYS = 128
ROPE_THETA = 10000.0
PAST_LEN = 8192
N_GROUPS = 4
N_EXP_PER_GROUP = 4
N_EXPERTS = N_GROUPS * N_EXP_PER_GROUP
D_EXPERT = 512
DN_ALPHA = 2.0 ** 0.25
LN_EPS = 1e-5
NEG_INF = -1e30

N_PAIRS = 6
N_CLASSES = N_GROUPS * N_PAIRS
LANES = 128
SUBLANES = 8
HEAD_PAIR_W = 2 * HEAD_DIM
N_HEAD_PAIRS = N_HEADS_B // 2
Q_BLK = 128
ROUTE_W = LANES
D_EXT = D_MODEL + ROUTE_W
VMEM_LIMIT = 56 * 1024 * 1024


def _cparams(sem):
    return pltpu.CompilerParams(dimension_semantics=sem, vmem_limit_bytes=VMEM_LIMIT)


def _mod_kernel(c_ref, w_ref, b_ref, o_ref):
    c = c_ref[...]
    s = (c * jax.nn.sigmoid(c)).astype(BF16)
    o_ref[...] = jnp.dot(s, w_ref[...].astype(BF16), preferred_element_type=F32) + b_ref[...]


def _modulation(c_all, w_ada, b_ada):
    n = c_all.shape[0]
    tn = 1024
    return pl.pallas_call(
        _mod_kernel,
        grid=(6 * D_MODEL // tn,),
        in_specs=[pl.BlockSpec((n, D_MODEL), lambda j: (0, 0)),
                  pl.BlockSpec((D_MODEL, tn), lambda j: (0, j)),
                  pl.BlockSpec((1, tn), lambda j: (0, j))],
        out_specs=pl.BlockSpec((n, tn), lambda j: (0, j)),
        out_shape=jax.ShapeDtypeStruct((n, 6 * D_MODEL), F32),
        compiler_params=_cparams(("arbitrary",)),
        name="adaln_mod",
    )(c_all, w_ada, b_ada.reshape(1, -1))


def _rope_apply(t, cos, sin_signed):
    lane = lax.broadcasted_iota(jnp.int32, t.shape, t.ndim - 1)
    first_half = (lane & (HEAD_DIM - 1)) < HEAD_DIM // 2
    width = t.shape[-1]
    swapped = jnp.where(first_half,
                        pltpu.roll(t, width - HEAD_DIM // 2, t.ndim - 1),
                        pltpu.roll(t, HEAD_DIM // 2, t.ndim - 1))
    return t * cos + swapped * sin_signed


def _rglru_gates(xc, wg_ref, b_a, b_x, lam):
    half = D_A // 2
    xcb = xc.astype(BF16)
    g0 = jnp.dot(xcb[:, :half], wg_ref[0], preferred_element_type=F32)
    g1 = jnp.dot(xcb[:, half:], wg_ref[1], preferred_element_type=F32)
    r = jax.nn.sigmoid(jnp.concatenate([g0[:, :half], g1[:, :half]], axis=1) + b_a)
    i = jax.nn.sigmoid(jnp.concatenate([g0[:, half:], g1[:, half:]], axis=1) + b_x)
    z = -lam
    softplus = jnp.maximum(z, 0.0) + jnp.log1p(jnp.exp(-jnp.abs(z)))
    log_a = -RG_C * r * softplus
    a = jnp.exp(log_a)
    one_minus_a2 = -jnp.tanh(log_a) * (a * a + 1.0)
    u = jnp.sqrt(one_minus_a2) * (i * xc)
    return a, u


def _rms_norm(y, g):
    return y * lax.rsqrt(jnp.mean(y * y, axis=-1, keepdims=True) + LN_EPS) * g


def _layer_norm(x, g, b):
    mu = jnp.mean(x, axis=-1, keepdims=True)
    xc = x - mu
    var = jnp.mean(xc * xc, axis=-1, keepdims=True)
    return xc * lax.rsqrt(var + LN_EPS) * g + b


def _inproj_prompt_kernel(x_ref, mod_ref, w_in_ref, cos_ref, sin_ref, conv_w_ref, conv_b_ref,
                          wg_ref, b_a_ref, b_x_ref, lam_ref, gna_ref,
                          ya_ref, q_ref, k_ref, v_ref, conv_out_ref, h_out_ref,
                          xp_buf, a_s, u_s, h_s, h_carry):
    t = pl.program_id(1)
    rows = x_ref.shape[0]
    pad = SUBLANES

    @pl.when(t == 0)
    def _():
        xp_buf[0:pad, :] = jnp.zeros((pad, D_A), F32)
        h_carry[...] = jnp.zeros_like(h_carry)

    u = (x_ref[...] * (1.0 + mod_ref[1:2, :]) + mod_ref[0:1, :]).astype(BF16)

    def proj(j):
        return jnp.dot(u, w_in_ref[:, j * D_A:(j + 1) * D_A], preferred_element_type=F32)

    cos = cos_ref[...]
    sin = sin_ref[...]
    q_ref[...] = _rope_apply(proj(2), cos, sin)
    k_ref[...] = _rope_apply(proj(3), cos, sin)
    v_ref[...] = proj(4)

    xa = proj(0)
    xp_buf[pad:pad + rows, :] = xa
    xc = conv_b_ref[...] + xa * conv_w_ref[CONV_W - 1:CONV_W, :]
    for j in range(CONV_W - 1):
        off = pad - (CONV_W - 1) + j
        xc = xc + xp_buf[off:off + rows, :] * conv_w_ref[j:j + 1, :]
    tail = xp_buf[rows + pad - (CONV_W - 1):rows + pad, :]
    conv_out_ref[...] = tail
    xp_buf[pad - (CONV_W - 1):pad, :] = tail

    a, u_in = _rglru_gates(xc, wg_ref, b_a_ref[...], b_x_ref[...], lam_ref[...])
    a_s[...] = a
    u_s[...] = u_in

    row = lax.broadcasted_iota(jnp.int32, (SUBLANES, D_A), 0)

    def group(g, h_prev):
        r0 = pl.multiple_of(g * SUBLANES, SUBLANES)
        ag = a_s[pl.ds(r0, SUBLANES), :]
        ug = u_s[pl.ds(r0, SUBLANES), :]
        for sh in (1, 2, 4):
            keep = row >= sh
            a_sh = pltpu.roll(ag, sh, 0)
            u_sh = pltpu.roll(ug, sh, 0)
            ug = jnp.where(keep, ag * u_sh + ug, ug)
            ag = jnp.where(keep, ag * a_sh, ag)
        hg = ag * h_prev + ug
        h_s[pl.ds(r0, SUBLANES), :] = hg
        return hg[SUBLANES - 1:SUBLANES, :]

    h_last = lax.fori_loop(0, rows // SUBLANES, group, h_carry[...], unroll=4)
    h_carry[...] = h_last
    h_out_ref[...] = h_last

    y = h_s[...] * jax.nn.gelu(proj(1))
    ya_ref[...] = _rms_norm(y, gna_ref[...]).astype(BF16)


def _inproj_prompt(x, mod3, w_in_bf, cos_t, sin_t, rg):
    b, s, _ = x.shape
    tile = 512
    nt = s // tile
    row_spec = lambda w: pl.BlockSpec((None, tile, w), lambda i, j: (i, j, 0))
    vec = lambda r, w: pl.BlockSpec((r, w), lambda i, j: (0, 0))
    outs = pl.pallas_call(
        _inproj_prompt_kernel,
        grid=(b, nt),
        in_specs=[row_spec(D_MODEL),
                  pl.BlockSpec((None, 6, D_MODEL), lambda i, j: (i, 0, 0)),
                  vec(D_MODEL, 5 * D_A),
                  pl.BlockSpec((tile, D_B), lambda i, j: (j, 0)),
                  pl.BlockSpec((tile, D_B), lambda i, j: (j, 0)),
                  vec(CONV_W, D_A), vec(1, D_A),
                  pl.BlockSpec((2, D_A // 2, D_A), lambda i, j: (0, 0, 0)),
                  vec(1, D_A), vec(1, D_A), vec(1, D_A), vec(1, D_A)],
        out_specs=[row_spec(D_A), row_spec(D_B), row_spec(D_B), row_spec(D_B),
                   pl.BlockSpec((None, CONV_W - 1, D_A), lambda i, j: (i, 0, 0)),
                   pl.BlockSpec((None, 1, D_A), lambda i, j: (i, 0, 0))],
        out_shape=[jax.ShapeDtypeStruct((b, s, D_A), BF16),
                   jax.ShapeDtypeStruct((b, s, D_B), F32),
                   jax.ShapeDtypeStruct((b, s, D_B), F32),
                   jax.ShapeDtypeStruct((b, s, D_B), F32),
                   jax.ShapeDtypeStruct((b, CONV_W - 1, D_A), F32),
                   jax.ShapeDtypeStruct((b, 1, D_A), F32)],
        scratch_shapes=[pltpu.VMEM((tile + SUBLANES, D_A), F32),
                        pltpu.VMEM((tile, D_A), F32),
                        pltpu.VMEM((tile, D_A), F32),
                        pltpu.VMEM((tile, D_A), F32),
                        pltpu.VMEM((1, D_A), F32)],
        compiler_params=_cparams(("arbitrary", "arbitrary")),
        name="inproj_prompt",
    )(x, mod3, w_in_bf, cos_t, sin_t, rg["conv_w"], rg["conv_b"], rg["w_gate"],
      rg["b_a"], rg["b_x"], rg["lam"], rg["g_norm_a"])
    return outs


def _attn_prompt_kernel(q_ref, k_ref, v_ref, o_ref, kwin_ref, vwin_ref, acc_s, m_s, l_s):
    s = q_ref.shape[0]
    keep = kwin_ref.shape[0]
    kwin_ref[...] = k_ref[s - keep:s, :]
    vwin_ref[...] = v_ref[s - keep:s, :]

    lane = lax.broadcasted_iota(jnp.int32, (Q_BLK, HEAD_PAIR_W), 1)
    head0 = lane < HEAD_DIM

    def rows(start, n, d):
        return pl.ds(start, n) if d == 1 else pl.ds(start, n, stride=d)

    def unit(bi, d, start_q, start_k, nk):
        qb = q_ref[rows(start_q, Q_BLK, d), :] * (HEAD_DIM ** -0.5)
        kb = k_ref[rows(start_k, nk, d), :].astype(BF16)
        vb = v_ref[rows(start_k, nk, d), :].astype(BF16)
        qi = lax.broadcasted_iota(jnp.int32, (Q_BLK, nk), 0)
        ki = lax.broadcasted_iota(jnp.int32, (Q_BLK, nk), 1)
        dist = (nk - Q_BLK) + qi - ki
        band = (dist >= 0) & (dist <= N_KEYS)
        stats = []
        for sel in (head0, jnp.logical_not(head0)):
            qh = jnp.where(sel, qb, 0.0).astype(BF16)
            sc = lax.dot_general(qh, kb, (((1,), (1,)), ((), ())), preferred_element_type=F32)
            sc = jnp.where(band, sc, NEG_INF)
            m = jnp.max(sc, axis=-1, keepdims=True)
            p = jnp.exp(sc - m)
            l = jnp.sum(p, axis=-1, keepdims=True)
            o = jnp.dot(p.astype(BF16), vb, preferred_element_type=F32)
            stats.append((m, l, o))
        (m0, l0, o0), (m1, l1, o1) = stats
        dst = rows(start_q, Q_BLK, d)
        acc_s[bi, dst, :] = jnp.where(head0, o0, o1)
        m_s[bi, dst, :] = jnp.where(head0, m0, m1)
        l_s[bi, dst, :] = jnp.where(head0, l0, l1)

    for bi, (_, d) in enumerate(DILATED_CFG):
        nb = s // d // Q_BLK

        def residue(r, carry, bi=bi, d=d, nb=nb):
            unit(bi, d, r, r, Q_BLK)

            def block(j, c):
                start_q = r + d * Q_BLK * j
                unit(bi, d, start_q, start_q - d * Q_BLK, 2 * Q_BLK)
                return c

            return lax.fori_loop(1, nb, block, carry)

        if d == 1:
            residue(0, 0)
        else:
            lax.fori_loop(0, d, residue, 0)

    chunk = 512

    def merge(i, carry):
        r0 = pl.multiple_of(i * chunk, chunk)
        sl = pl.ds(r0, chunk)
        ms = [m_s[bi, sl, :] for bi in range(len(DILATED_CFG))]
        mx = functools.reduce(jnp.maximum, ms)
        num = jnp.zeros((chunk, HEAD_PAIR_W), F32)
        den = jnp.zeros((chunk, HEAD_PAIR_W), F32)
        for bi in range(len(DILATED_CFG)):
            w = jnp.exp(ms[bi] - mx)
            num = num + w * acc_s[bi, sl, :]
            den = den + w * l_s[bi, sl, :]
        o_ref[sl, :] = num / den
        return carry

    lax.fori_loop(0, s // chunk, merge, 0)


def _attn_prompt(q, k, v):
    b, s, _ = q.shape
    keep = min(WIN_MAX, s)
    nbr = len(DILATED_CFG)
    spec = pl.BlockSpec((None, s, HEAD_PAIR_W), lambda i, j: (i, 0, j))
    wspec = pl.BlockSpec((None, keep, HEAD_PAIR_W), lambda i, j: (i, 0, j))
    return pl.pallas_call(
        _attn_prompt_kernel,
        grid=(b, N_HEAD_PAIRS),
        in_specs=[spec, spec, spec],
        out_specs=[spec, wspec, wspec],
        out_shape=[jax.ShapeDtypeStruct((b, s, D_B), F32),
                   jax.ShapeDtypeStruct((b, keep, D_B), F32),
                   jax.ShapeDtypeStruct((b, keep, D_B), F32)],
        scratch_shapes=[pltpu.VMEM((nbr, s, HEAD_PAIR_W), F32),
                        pltpu.VMEM((nbr, s, HEAD_PAIR_W), F32),
                        pltpu.VMEM((nbr, s, HEAD_PAIR_W), F32)],
        compiler_params=_cparams(("arbitrary", "arbitrary")),
        name="attn_prompt",
    )(q, k, v)


def _inproj_sample_kernel(x_ref, mod_ref, w_in_ref, cos_ref, sin_ref, conv_state_ref, h0_ref,
                          conv_w_ref, conv_b_ref, wg_ref, b_a_ref, b_x_ref, lam_ref, gna_ref,
                          ya_ref, q_ref, k_ref, v_ref, conv_out_ref, h_out_ref):
    nt, nb, _ = x_ref.shape
    sh1 = mod_ref[:, 0:D_MODEL]
    sc1 = mod_ref[:, D_MODEL:2 * D_MODEL]
    u = (x_ref[...] * (1.0 + sc1)[None] + sh1[None]).astype(BF16).reshape(nt * nb, D_MODEL)

    def proj(j):
        return jnp.dot(u, w_in_ref[:, j * D_A:(j + 1) * D_A], preferred_element_type=F32)

    cos = cos_ref[...].reshape(nt * nb, D_B)
    sin = sin_ref[...].reshape(nt * nb, D_B)
    q_ref[...] = _rope_apply(proj(2), cos, sin).reshape(nt, nb, D_B)
    k_ref[...] = _rope_apply(proj(3), cos, sin).reshape(nt, nb, D_B)
    v_ref[...] = proj(4).reshape(nt, nb, D_B)

    xa = proj(0).reshape(nt, nb, D_A)
    xp = [conv_state_ref[j] for j in range(CONV_W - 1)] + [xa[t] for t in range(nt)]
    xc = jnp.concatenate(
        [conv_b_ref[...] + sum(xp[t + j] * conv_w_ref[j:j + 1, :] for j in range(CONV_W))
         for t in range(nt)], axis=0)
    for j in range(CONV_W - 1):
        conv_out_ref[j] = xp[nt + j]

    a, u_in = _rglru_gates(xc, wg_ref, b_a_ref[...], b_x_ref[...], lam_ref[...])
    h = h0_ref[...]
    hs = []
    for t in range(nt):
        h = a[t * nb:(t + 1) * nb] * h + u_in[t * nb:(t + 1) * nb]
        hs.append(h)
    h_out_ref[...] = h
    y = jnp.concatenate(hs, axis=0) * jax.nn.gelu(proj(1))
    ya_ref[...] = _rms_norm(y, gna_ref[...]).astype(BF16).reshape(nt, nb, D_A)


def _inproj_sample(x_tb, mod_s, w_in_bf, cos_t, sin_t, conv_state_tb, h0, rg):
    nt, nb, _ = x_tb.shape
    full = lambda shape: pl.BlockSpec(shape, lambda i: (0,) * len(shape))
    return pl.pallas_call(
        _inproj_sample_kernel,
        grid=(1,),
        in_specs=[full((nt, nb, D_MODEL)), full((nb, 6 * D_MODEL)), full((D_MODEL, 5 * D_A)),
                  full((nt, nb, D_B)), full((nt, nb, D_B)),
                  full((CONV_W - 1, nb, D_A)), full((nb, D_A)),
                  full((CONV_W, D_A)), full((1, D_A)), full((2, D_A // 2, D_A)),
                  full((1, D_A)), full((1, D_A)), full((1, D_A)), full((1, D_A))],
        out_specs=[full((nt, nb, D_A)), full((nt, nb, D_B)), full((nt, nb, D_B)),
                   full((nt, nb, D_B)), full((CONV_W - 1, nb, D_A)), full((nb, D_A))],
        out_shape=[jax.ShapeDtypeStruct((nt, nb, D_A), BF16),
                   jax.ShapeDtypeStruct((nt, nb, D_B), F32),
                   jax.ShapeDtypeStruct((nt, nb, D_B), F32),
                   jax.ShapeDtypeStruct((nt, nb, D_B), F32),
                   jax.ShapeDtypeStruct((CONV_W - 1, nb, D_A), F32),
                   jax.ShapeDtypeStruct((nb, D_A), F32)],
        compiler_params=_cparams(("arbitrary",)),
        name="inproj_sample",
    )(x_tb, mod_s, w_in_bf, cos_t, sin_t, conv_state_tb, h0, rg["conv_w"], rg["conv_b"],
      rg["w_gate"], rg["b_a"], rg["b_x"], rg["lam"], rg["g_norm_a"])


def _attn_sample_kernel(q_ref, kn_ref, vn_ref, ck_ref, cv_ref, o_ref, kwin_ref, vwin_ref,
                        kn_pad, vn_pad):
    nt = q_ref.shape[0]
    n_buf = ck_ref.shape[0]
    n_rows = N_HEADS_B * nt

    @pl.when(pl.program_id(0) == 0)
    def _():
        kn_pad[...] = jnp.zeros_like(kn_pad)
        vn_pad[...] = jnp.zeros_like(vn_pad)

    kn_pad[0:nt, :] = kn_ref[...]
    vn_pad[0:nt, :] = vn_ref[...]

    kwin_ref[0:n_buf - nt, :] = ck_ref[nt:n_buf, :]
    kwin_ref[n_buf - nt:n_buf, :] = kn_ref[...]
    vwin_ref[0:n_buf - nt, :] = cv_ref[nt:n_buf, :]
    vwin_ref[n_buf - nt:n_buf, :] = vn_ref[...]

    ri = lax.broadcasted_iota(jnp.int32, (n_rows, nt), 0)
    ci = lax.broadcasted_iota(jnp.int32, (n_rows, nt), 1)
    pick = (ri % nt == ci).astype(BF16)
    qs = (q_ref[...] * (HEAD_DIM ** -0.5)).astype(BF16)
    q_rep = jnp.dot(pick, qs, preferred_element_type=F32)
    row_h = lax.broadcasted_iota(jnp.int32, (n_rows, D_B), 0) // nt
    lane_h = lax.broadcasted_iota(jnp.int32, (n_rows, D_B), 1) // HEAD_DIM
    own = row_h == lane_h
    qbd = jnp.where(own, q_rep, 0.0).astype(BF16)

    nt_dims = (((1,), (1,)), ((), ()))
    sc_c = lax.dot_general(qbd, ck_ref[...].astype(BF16), nt_dims, preferred_element_type=F32)
    sc_n = lax.dot_general(qbd, kn_pad[...].astype(BF16), nt_dims, preferred_element_type=F32)

    def mult(dist, limit_ok):
        c = jnp.zeros(dist.shape, F32)
        for win, d in DILATED_CFG:
            hit = (dist >= 0) & (dist <= win) & (dist % d == 0) & limit_ok
            c = c + hit.astype(F32)
        return c

    t_c = lax.broadcasted_iota(jnp.int32, sc_c.shape, 0) % nt
    dist_c = n_buf + t_c - lax.broadcasted_iota(jnp.int32, sc_c.shape, 1)
    mult_c = mult(dist_c, dist_c >= 0)
    t_n = lax.broadcasted_iota(jnp.int32, sc_n.shape, 0) % nt
    col_n = lax.broadcasted_iota(jnp.int32, sc_n.shape, 1)
    mult_n = mult(t_n - col_n, col_n < nt)

    sc_c = jnp.where(mult_c > 0, sc_c, NEG_INF)
    sc_n = jnp.where(mult_n > 0, sc_n, NEG_INF)
    m = jnp.maximum(jnp.max(sc_c, axis=-1, keepdims=True), jnp.max(sc_n, axis=-1, keepdims=True))
    p_c = mult_c * jnp.exp(sc_c - m)
    p_n = mult_n * jnp.exp(sc_n - m)
    l = jnp.sum(p_c, axis=-1, keepdims=True) + jnp.sum(p_n, axis=-1, keepdims=True)
    acc = (jnp.dot(p_c.astype(BF16), cv_ref[...].astype(BF16), preferred_element_type=F32)
           + jnp.dot(p_n.astype(BF16), vn_pad[...].astype(BF16), preferred_element_type=F32))
    o_full = jnp.where(own, acc / l, 0.0)
    out = o_full[0:nt, :]
    for h in range(1, N_HEADS_B):
        out = out + o_full[h * nt:(h + 1) * nt, :]
    o_ref[...] = out


def _attn_sample(q, k_new, v_new, cache_k, cache_v):
    b, nt, _ = q.shape
    n_buf = cache_k.shape[1]
    small = pl.BlockSpec((None, nt, D_B), lambda i: (i, 0, 0))
    big = pl.BlockSpec((None, n_buf, D_B), lambda i: (i, 0, 0))
    return pl.pallas_call(
        _attn_sample_kernel,
        grid=(b,),
        in_specs=[small, small, small, big, big],
        out_specs=[small, big, big],
        out_shape=[jax.ShapeDtypeStruct((b, nt, D_B), F32),
                   jax.ShapeDtypeStruct((b, n_buf, D_B), F32),
                   jax.ShapeDtypeStruct((b, n_buf, D_B), F32)],
        scratch_shapes=[pltpu.VMEM((LANES, D_B), F32), pltpu.VMEM((LANES, D_B), F32)],
        compiler_params=_cparams(("arbitrary",)),
        name="attn_sample",
    )(q, k_new, v_new, cache_k, cache_v)


def _split_bf16(x):
    hi = x.astype(BF16)
    lo = (x - hi.astype(F32)).astype(BF16)
    return hi, lo


def _first_argmax(vals):
    mx = functools.reduce(jnp.maximum, vals)
    idx = jnp.full(mx.shape, float(len(vals) - 1), F32)
    for j in range(len(vals) - 2, -1, -1):
        idx = jnp.where(vals[j] == mx, float(j), idx)
    return mx, idx


def _outproj_kernel(ya_ref, yb_ref, x_ref, gt1_ref, sh2_ref, sc2_ref, w_out_ref, gnb_ref,
                    ln_g_ref, ln_b_ref, w_r_ref, b_r_ref, x1_ref, u2_ref, route_ref):
    yb = _rms_norm(yb_ref[...], gnb_ref[...]).astype(BF16)
    mixed = (jnp.dot(ya_ref[...], w_out_ref[0:D_A, :], preferred_element_type=F32)
             + jnp.dot(yb, w_out_ref[D_A:D_A + D_B, :], preferred_element_type=F32))
    x1 = _layer_norm(DN_ALPHA * x_ref[...] + gt1_ref[...] * mixed, ln_g_ref[...], ln_b_ref[...])
    x1_ref[...] = x1
    u2 = x1 * (1.0 + sc2_ref[...]) + sh2_ref[...]
    u2_ref[:, 0:D_MODEL] = u2

    u_hi, u_lo = _split_bf16(u2)
    w_hi, w_lo = _split_bf16(w_r_ref[...])
    logits = (jnp.dot(u_hi, w_hi, preferred_element_type=F32)
              + jnp.dot(u_lo, w_hi, preferred_element_type=F32)
              + jnp.dot(u_hi, w_lo, preferred_element_type=F32)) + b_r_ref[...]
    lt = logits.T

    g_rows = [lt[j:j + 1, :] for j in range(N_GROUPS)]
    g_max, g_idx = _first_argmax(g_rows)
    p_group = 1.0 / sum(jnp.exp(g - g_max) for g in g_rows)
    e_rows = []
    for e in range(N_EXP_PER_GROUP):
        acc = jnp.zeros_like(g_max)
        for g in range(N_GROUPS):
            r = N_GROUPS + g * N_EXP_PER_GROUP + e
            acc = jnp.where(g_idx == float(g), lt[r:r + 1, :], acc)
        e_rows.append(acc)
    v1, i1 = _first_argmax(e_rows)
    rest = [jnp.where(i1 == float(e), -jnp.inf, e_rows[e]) for e in range(N_EXP_PER_GROUP)]
    v2, i2 = _first_argmax(rest)
    ex = jnp.exp(v2 - v1)
    w1 = p_group / (1.0 + ex)
    w2 = p_group * ex / (1.0 + ex)
    lo = jnp.minimum(i1, i2)
    hi = jnp.maximum(i1, i2)
    pair = jnp.where(lo == 0.0, hi - 1.0, jnp.where(lo == 1.0, hi + 1.0, 5.0))
    cls = g_idx * float(N_PAIRS) + pair
    w_of_lo = jnp.where(i1 < i2, w1, w2)
    w_of_hi = jnp.where(i1 < i2, w2, w1)
    route = jnp.concatenate(
        [cls, w_of_lo, w_of_hi, jnp.zeros((ROUTE_W - 3, cls.shape[1]), F32)], axis=0)
    route_ref[...] = route[0:SUBLANES, :]
    u2_ref[:, D_MODEL:D_EXT] = route.T


def _outproj(ya, yb, x, mods, w_out_bf, gnb, ln_g, ln_b, w_r, b_r, tile):
    n = x.shape[0]
    nt = n // tile
    mod_arr, gt1_spec, sh2_spec, sc2_spec = mods
    row = lambda w: pl.BlockSpec((tile, w), lambda i: (i, 0))
    vec = lambda r, w: pl.BlockSpec((r, w), lambda i: (0, 0))
    return pl.pallas_call(
        _outproj_kernel,
        grid=(nt,),
        in_specs=[row(D_A), row(D_B), row(D_MODEL), gt1_spec, sh2_spec, sc2_spec,
                  vec(D_MODEL, D_MODEL), vec(1, D_B), vec(1, D_MODEL), vec(1, D_MODEL),
                  vec(D_MODEL, ROUTE_W), vec(1, ROUTE_W)],
        out_specs=[row(D_MODEL), row(D_EXT),
                   pl.BlockSpec((None, SUBLANES, tile), lambda i: (i, 0, 0))],
        out_shape=[jax.ShapeDtypeStruct((n, D_MODEL), F32),
                   jax.ShapeDtypeStruct((n, D_EXT), F32),
                   jax.ShapeDtypeStruct((nt, SUBLANES, tile), F32)],
        compiler_params=_cparams(("arbitrary",)),
        name="outproj_router",
    )(ya, yb, x, mod_arr, mod_arr, mod_arr, w_out_bf, gnb, ln_g, ln_b, w_r, b_r)


def _moe_kernel(e_lo_ref, e_hi_ref, nvalid_ref, src_ref,
                x_hbm, wg_lo, wg_hi, wu_lo, wu_hi, wd_lo, wd_hi, o_hbm,
                xg, og, gsem, ssem, *, n_tokens, tile):
    i = pl.program_id(0)
    n_steps = pl.num_programs(0)
    slot = i % 2

    def gather(step, sl):
        def issue(r, c):
            tok = src_ref[step * tile + r]
            pltpu.make_async_copy(x_hbm.at[pl.ds(tok, 1)], xg.at[sl, pl.ds(r, 1)],
                                  gsem.at[sl]).start()
            return c
        lax.fori_loop(0, tile, issue, 0)

    def gather_wait(sl):
        pltpu.make_async_copy(x_hbm.at[pl.ds(0, tile)], xg.at[sl], gsem.at[sl]).wait()

    def scatter(step, sl):
        nv = nvalid_ref[step]

        def issue(r, c):
            tok = src_ref[step * tile + r]
            dst = jnp.where(r < nv, tok, n_tokens + sl * tile + r)
            pltpu.make_async_copy(og.at[sl, pl.ds(r, 1)], o_hbm.at[pl.ds(dst, 1)],
                                  ssem.at[sl]).start()
            return c
        lax.fori_loop(0, tile, issue, 0)

    def scatter_wait(sl):
        pltpu.make_async_copy(og.at[sl], o_hbm.at[pl.ds(0, tile)], ssem.at[sl]).wait()

    @pl.when(i == 0)
    def _():
        gather(0, 0)
        og[...] = jnp.zeros_like(og)
        for sl in range(2):
            pad_rows = pltpu.make_async_copy(
                og.at[sl], o_hbm.at[pl.ds(n_tokens + sl * tile, tile)], ssem.at[sl])
            pad_rows.start()
            pad_rows.wait()

    @pl.when(jnp.logical_and(i + 1 < n_steps, nvalid_ref[jnp.minimum(i + 1, n_steps - 1)] > 0))
    def _():
        gather(i + 1, 1 - slot)

    @pl.when(jnp.logical_or(i == 0, nvalid_ref[i] > 0))
    def _():
        gather_wait(slot)

    @pl.when(jnp.logical_and(i >= 2, nvalid_ref[jnp.maximum(i - 2, 0)] > 0))
    def _():
        scatter_wait(slot)

    @pl.when(nvalid_ref[i] > 0)
    def _():
        xe = xg[slot]
        xb = xe[:, 0:D_MODEL].astype(BF16)
        out = jnp.zeros((tile, D_MODEL), F32)
        for col, wg, wu, wd in ((1, wg_lo, wu_lo, wd_lo), (2, wg_hi, wu_hi, wd_hi)):
            hg = jnp.dot(xb, wg[...], preferred_element_type=F32)
            hu = jnp.dot(xb, wu[...], preferred_element_type=F32)
            act = (hg * jax.nn.sigmoid(hg)) * hu * xe[:, D_MODEL + col:D_MODEL + col + 1]
            out = out + jnp.dot(act.astype(BF16), wd[...], preferred_element_type=F32)
        og[slot] = out
        scatter(i, slot)

    @pl.when(i == n_steps - 1)
    def _():
        @pl.when(jnp.logical_and(i >= 1, nvalid_ref[jnp.maximum(i - 1, 0)] > 0))
        def _():
            scatter_wait(1 - slot)

        @pl.when(nvalid_ref[i] > 0)
        def _():
            scatter_wait(slot)


def _moe(u2ext, cls, w_gate_bf, w_up_bf, w_down_bf, tile):
    n = u2ext.shape[0]
    n_steps = n // tile + N_CLASSES
    cls = cls.astype(jnp.int32)
    order = jnp.argsort(cls, stable=True).astype(jnp.int32)
    counts = jnp.zeros((N_CLASSES,), jnp.int32).at[cls].add(1)
    tiles_per = (counts + tile - 1) // tile
    tile_end = jnp.cumsum(tiles_per)
    tile_off = tile_end - tiles_per
    class_start = jnp.cumsum(counts) - counts
    step = jnp.arange(n_steps, dtype=jnp.int32)
    used = step < tile_end[-1]
    cls_of = jnp.minimum(jnp.searchsorted(tile_end, step, side="right"), N_CLASSES - 1)
    last_cls = cls_of[jnp.maximum(tile_end[-1] - 1, 0)]
    cls_of = jnp.where(used, cls_of, last_cls).astype(jnp.int32)
    local = step - tile_off[cls_of]
    nvalid = jnp.where(used, jnp.clip(counts[cls_of] - local * tile, 0, tile), 0).astype(jnp.int32)
    r = jnp.arange(tile, dtype=jnp.int32)
    pos = class_start[cls_of][:, None] + local[:, None] * tile + r[None, :]
    valid = r[None, :] < nvalid[:, None]
    src = jnp.where(valid, order[jnp.clip(pos, 0, n - 1)], 0).astype(jnp.int32).reshape(-1)
    grp = cls_of // N_PAIRS
    pair = cls_of % N_PAIRS
    pair_lo = jnp.array([0, 0, 0, 1, 1, 2], jnp.int32)[pair]
    pair_hi = jnp.array([1, 2, 3, 2, 3, 3], jnp.int32)[pair]
    e_lo = (grp * N_EXP_PER_GROUP + pair_lo).astype(jnp.int32)
    e_hi = (grp * N_EXP_PER_GROUP + pair_hi).astype(jnp.int32)

    w_in_spec = lambda which: pl.BlockSpec(
        (None, D_MODEL, D_EXPERT), lambda i, elo, ehi, nv, s: ((elo, ehi)[which][i], 0, 0))
    w_dn_spec = lambda which: pl.BlockSpec(
        (None, D_EXPERT, D_MODEL), lambda i, elo, ehi, nv, s: ((elo, ehi)[which][i], 0, 0))
    grid_spec = pltpu.PrefetchScalarGridSpec(
        num_scalar_prefetch=4,
        grid=(n_steps,),
        in_specs=[pl.BlockSpec(memory_space=pl.ANY),
                  w_in_spec(0), w_in_spec(1), w_in_spec(0), w_in_spec(1),
                  w_dn_spec(0), w_dn_spec(1)],
        out_specs=pl.BlockSpec(memory_space=pl.ANY),
        scratch_shapes=[pltpu.VMEM((2, tile, D_EXT), F32),
                        pltpu.VMEM((2, tile, D_MODEL), F32),
                        pltpu.SemaphoreType.DMA((2,)),
                        pltpu.SemaphoreType.DMA((2,))],
    )
    return pl.pallas_call(
        functools.partial(_moe_kernel, n_tokens=n, tile=tile),
        grid_spec=grid_spec,
        out_shape=jax.ShapeDtypeStruct((n + 2 * tile, D_MODEL), F32),
        compiler_params=_cparams(("arbitrary",)),
        name="moe_sparse",
    )(e_lo, e_hi, nvalid, src, u2ext, w_gate_bf, w_gate_bf, w_up_bf, w_up_bf, w_down_bf, w_down_bf)


def _final_kernel(x1_ref, ffn_ref, gt2_ref, g_ref, b_ref, o_ref):
    o_ref[...] = _layer_norm(DN_ALPHA * x1_ref[...] + gt2_ref[...] * ffn_ref[...],
                             g_ref[...], b_ref[...])


def _final_norm(x1, ffn_padded, mod_arr, gt2_spec, ln_g, ln_b, tile):
    n = x1.shape[0]
    row = pl.BlockSpec((tile, D_MODEL), lambda i: (i, 0))
    vec = pl.BlockSpec((1, D_MODEL), lambda i: (0, 0))
    return pl.pallas_call(
        _final_kernel,
        grid=(n // tile,),
        in_specs=[row, row, gt2_spec, vec, vec],
        out_specs=row,
        out_shape=jax.ShapeDtypeStruct((n, D_MODEL), F32),
        compiler_params=_cparams(("arbitrary",)),
        name="final_norm",
    )(x1, ffn_padded, mod_arr, ln_g, ln_b)


def _rope_tables(pos):
    half = HEAD_DIM // 2
    inv = ROPE_THETA ** (-jnp.arange(half, dtype=F32) * 2.0 / HEAD_DIM)
    ang = pos.astype(F32)[:, None] * inv[None, :]
    cos = jnp.cos(ang)
    sin = jnp.sin(ang)
    cos_t = jnp.tile(jnp.concatenate([cos, cos], axis=-1), (1, N_HEADS_B))
    sin_t = jnp.tile(jnp.concatenate([-sin, sin], axis=-1), (1, N_HEADS_B))
    return cos_t, sin_t


def _block_diag(w):
    n, a, b = w.shape
    eye = jnp.eye(n, dtype=w.dtype)
    return (eye[:, None, :, None] * w[:, :, None, :]).reshape(n * a, n * b)


def _prepare_weights(w_in, conv_w, conv_b, w_rg_a, b_rg_a, w_rg_x, b_rg_x, rg_lambda, g_norm_a,
                     w_router_group, b_router_group, w_router_expert, b_router_expert):
    half_blocks = N_BLK_A // 2
    w_gate = jnp.stack([
        jnp.concatenate([_block_diag(w_rg_a[h * half_blocks:(h + 1) * half_blocks]),
                         _block_diag(w_rg_x[h * half_blocks:(h + 1) * half_blocks])], axis=1)
        for h in range(2)]).astype(BF16)
    rg = dict(conv_w=conv_w, conv_b=conv_b.reshape(1, D_A), w_gate=w_gate,
              b_a=b_rg_a.reshape(1, D_A), b_x=b_rg_x.reshape(1, D_A),
              lam=rg_lambda.reshape(1, D_A), g_norm_a=g_norm_a.reshape(1, D_A))
    n_logits = N_GROUPS + N_EXPERTS
    w_r = jnp.concatenate(
        [w_router_group,
         w_router_expert.transpose(1, 0, 2).reshape(D_MODEL, N_EXPERTS),
         jnp.zeros((D_MODEL, ROUTE_W - n_logits), F32)], axis=1)
    b_r = jnp.concatenate([b_router_group, b_router_expert.reshape(-1),
                           jnp.zeros((ROUTE_W - n_logits,), F32)]).reshape(1, ROUTE_W)
    return rg, w_r, b_r


def _channel_mixing(ya, yb, x, mods, gt2_spec, weights, tile, moe_tile):
    (w_out_bf, gnb, ln1_g, ln1_b, w_r, b_r, wg_bf, wu_bf, wd_bf, ln2_g, ln2_b) = weights
    x1, u2ext, route = _outproj(ya, yb, x, mods, w_out_bf, gnb, ln1_g, ln1_b, w_r, b_r, tile)
    cls = route[:, 0, :].reshape(-1)
    ffn = _moe(u2ext, cls, wg_bf, wu_bf, wd_bf, moe_tile)
    return _final_norm(x1, ffn, mods[0], gt2_spec, ln2_g, ln2_b, tile)


def kernel(x_prompt, x_sample, state_conv, state_rglru, cache_win_k, cache_win_v, c_prompt, c_sample, w_ada, b_ada, w_in, conv_w, conv_b, w_rg_a, b_rg_a, w_rg_x, b_rg_x, rg_lambda, g_norm_a, g_norm_b, w_out, ln1_g, ln1_b, w_router_group, b_router_group, w_router_expert, b_router_expert, w_exp_gate, w_exp_up, w_exp_down, ln2_g, ln2_b):
    bp, sp, _ = x_prompt.shape
    bs, ts, _ = x_sample.shape
    n_buf = cache_win_k.shape[1]

    rg, w_r, b_r = _prepare_weights(w_in, conv_w, conv_b, w_rg_a, b_rg_a, w_rg_x, b_rg_x,
                                    rg_lambda, g_norm_a, w_router_group, b_router_group,
                                    w_router_expert, b_router_expert)
    w_in_bf = w_in.astype(BF16)
    mix_weights = (w_out.astype(BF16), g_norm_b.reshape(1, D_B), ln1_g.reshape(1, D_MODEL),
                   ln1_b.reshape(1, D_MODEL), w_r, b_r, w_exp_gate.astype(BF16),
                   w_exp_up.astype(BF16), w_exp_down.astype(BF16),
                   ln2_g.reshape(1, D_MODEL), ln2_b.reshape(1, D_MODEL))

    mod = _modulation(jnp.concatenate([c_prompt, c_sample], axis=0), w_ada, b_ada)
    mod_p, mod_s = mod[:bp], mod[bp:]

    tile_p = 512
    cos_p, sin_p = _rope_tables(jnp.arange(sp))
    ya_p, q_p, k_p, v_p, conv_p, h_p = _inproj_prompt(
        x_prompt, mod_p.reshape(bp, 6, D_MODEL), w_in_bf, cos_p, sin_p, rg)
    yb_p, kwin_p, vwin_p = _attn_prompt(q_p, k_p, v_p)
    tiles_per_seq = sp // tile_p
    mod_p3 = mod_p.reshape(bp * 6, 1, D_MODEL)
    mod_spec_p = lambda j: pl.BlockSpec((None, 1, D_MODEL),
                                        lambda i: ((i // tiles_per_seq) * 6 + j, 0, 0))
    y_p = _channel_mixing(
        ya_p.reshape(bp * sp, D_A), yb_p.reshape(bp * sp, D_B), x_prompt.reshape(bp * sp, D_MODEL),
        (mod_p3, mod_spec_p(2), mod_spec_p(3), mod_spec_p(4)), mod_spec_p(5),
        mix_weights, tile_p, 256)

    cos_s, sin_s = _rope_tables(PAST_LEN + jnp.arange(ts))
    tb = lambda t: jnp.broadcast_to(t[:, None, :], (ts, bs, D_B))
    ya_s, q_s, k_s, v_s, conv_s, h_s = _inproj_sample(
        x_sample.transpose(1, 0, 2), mod_s, w_in_bf, tb(cos_s), tb(sin_s),
        state_conv.transpose(1, 0, 2), state_rglru, rg)
    bt = lambda t: t.transpose(1, 0, 2)
    yb_s, kwin_s, vwin_s = _attn_sample(
        bt(q_s), bt(k_s), bt(v_s), cache_win_k.reshape(bs, n_buf, D_B),
        cache_win_v.reshape(bs, n_buf, D_B))
    mod_spec_s = lambda j: pl.BlockSpec((bs, D_MODEL), lambda i: (0, j))
    y_s = _channel_mixing(
        ya_s.reshape(ts * bs, D_A), bt(yb_s).reshape(ts * bs, D_B),
        x_sample.transpose(1, 0, 2).reshape(ts * bs, D_MODEL),
        (mod_s, mod_spec_s(2), mod_spec_s(3), mod_spec_s(4)), mod_spec_s(5),
        mix_weights, bs, bs)

    heads = lambda t: t.reshape(t.shape[0], t.shape[1], N_HEADS_B, HEAD_DIM)
    return (y_p.reshape(bp, sp, D_MODEL), bt(y_s.reshape(ts, bs, D_MODEL)),
            conv_p, h_p.reshape(bp, D_A), heads(kwin_p), heads(vwin_p),
            bt(conv_s), h_s, heads(kwin_s), heads(vwin_s))
```

```python
import functools
import math

import jax
import jax.numpy as jnp
from jax import lax
from jax.experimental import pallas as pl
from jax.experimental.pallas import tpu as pltpu

F32 = jnp.float32
BF16 = jnp.bfloat16

D_MODEL = 1024
D_A = 512
N_BLK_A = 8
BLK_W_A = D_A // N_BLK_A
CONV_W = 4
RG_C = 8.0
D_B = 512
HEAD_DIM = 64
N_HEADS_B = D_B // HEAD_DIM
DILATED_CFG = ((128, 1), (512, 4), (2048, 16))
WIN_MAX = 2048
N_KEYS = 128
ROPE_THETA = 10000.0
PAST_LEN = 8192
N_GROUPS = 4
N_EXP_PER_GROUP = 4
N_EXPERTS = N_GROUPS * N_EXP_PER_GROUP
D_EXPERT = 512
DN_ALPHA = 2.0 ** 0.25
LN_EPS = 1e-5
NEG_INF = -1e30

N_PAIRS = 6
N_CLASSES = N_GROUPS * N_PAIRS
LANES = 128
SUBLANES = 8
HEAD_PAIR_W = 2 * HEAD_DIM
N_HEAD_PAIRS = N_HEADS_B // 2
Q_BLK = 128
ATTN_UNROLL = 8
DMA_UNROLL = 8
ROUTE_W = LANES
D_EXT = D_MODEL + ROUTE_W
VMEM_LIMIT = 56 * 1024 * 1024


def _cparams(sem):
    return pltpu.CompilerParams(dimension_semantics=sem, vmem_limit_bytes=VMEM_LIMIT)


def _mod_kernel(c_ref, w_ref, b_ref, o_ref):
    c = c_ref[...]
    s = (c * jax.nn.sigmoid(c)).astype(BF16)
    o_ref[...] = jnp.dot(s, w_ref[...].astype(BF16), preferred_element_type=F32) + b_ref[...]


def _modulation(c_all, w_ada, b_ada):
    n = c_all.shape[0]
    tn = 1024
    return pl.pallas_call(
        _mod_kernel,
        grid=(6 * D_MODEL // tn,),
        in_specs=[pl.BlockSpec((n, D_MODEL), lambda j: (0, 0)),
                  pl.BlockSpec((D_MODEL, tn), lambda j: (0, j)),
                  pl.BlockSpec((1, tn), lambda j: (0, j))],
        out_specs=pl.BlockSpec((n, tn), lambda j: (0, j)),
        out_shape=jax.ShapeDtypeStruct((n, 6 * D_MODEL), F32),
        compiler_params=_cparams(("arbitrary",)),
        name="adaln_mod",
    )(c_all, w_ada, b_ada.reshape(1, -1))


def _rope_apply(t, cos, sin_signed):
    lane = lax.broadcasted_iota(jnp.int32, t.shape, t.ndim - 1)
    first_half = (lane & (HEAD_DIM - 1)) < HEAD_DIM // 2
    width = t.shape[-1]
    swapped = jnp.where(first_half,
                        pltpu.roll(t, width - HEAD_DIM // 2, t.ndim - 1),
                        pltpu.roll(t, HEAD_DIM // 2, t.ndim - 1))
    return t * cos + swapped * sin_signed


def _rglru_gates(xc, wg_ref, b_a, b_x, lam):
    half = D_A // 2
    xcb = xc.astype(BF16)
    g0 = jnp.dot(xcb[:, :half], wg_ref[0], preferred_element_type=F32)
    g1 = jnp.dot(xcb[:, half:], wg_ref[1], preferred_element_type=F32)
    r = jax.nn.sigmoid(jnp.concatenate([g0[:, :half], g1[:, :half]], axis=1) + b_a)
    i = jax.nn.sigmoid(jnp.concatenate([g0[:, half:], g1[:, half:]], axis=1) + b_x)
    z = -lam
    softplus = jnp.maximum(z, 0.0) + jnp.log1p(jnp.exp(-jnp.abs(z)))
    log_a = -RG_C * r * softplus
    a = jnp.exp(log_a)
    one_minus_a2 = -jnp.tanh(log_a) * (a * a + 1.0)
    u = jnp.sqrt(one_minus_a2) * (i * xc)
    return a, u


def _rms_norm(y, g):
    return y * lax.rsqrt(jnp.mean(y * y, axis=-1, keepdims=True) + LN_EPS) * g


def _layer_norm(x, g, b):
    mu = jnp.mean(x, axis=-1, keepdims=True)
    xc = x - mu
    var = jnp.mean(xc * xc, axis=-1, keepdims=True)
    return xc * lax.rsqrt(var + LN_EPS) * g + b


def _inproj_prompt_kernel(x_ref, mod_ref, w_in_ref, cos_ref, sin_ref, conv_w_ref, conv_b_ref,
                          wg_ref, b_a_ref, b_x_ref, lam_ref, gna_ref,
                          ya_ref, q_ref, k_ref, v_ref, conv_out_ref, h_out_ref,
                          xp_buf, a_s, u_s, h_s, h_carry):
    t = pl.program_id(1)
    rows = x_ref.shape[0]
    pad = SUBLANES

    @pl.when(t == 0)
    def _():
        xp_buf[0:pad, :] = jnp.zeros((pad, D_A), F32)
        h_carry[...] = jnp.zeros_like(h_carry)

    u = (x_ref[...] * (1.0 + mod_ref[1:2, :]) + mod_ref[0:1, :]).astype(BF16)

    def proj(j):
        return jnp.dot(u, w_in_ref[:, j * D_A:(j + 1) * D_A], preferred_element_type=F32)

    cos = cos_ref[...]
    sin = sin_ref[...]
    q_ref[...] = _rope_apply(proj(2), cos, sin)
    k_ref[...] = _rope_apply(proj(3), cos, sin)
    v_ref[...] = proj(4)

    xa = proj(0)
    xp_buf[pad:pad + rows, :] = xa
    xc = conv_b_ref[...] + xa * conv_w_ref[CONV_W - 1:CONV_W, :]
    for j in range(CONV_W - 1):
        off = pad - (CONV_W - 1) + j
        xc = xc + xp_buf[off:off + rows, :] * conv_w_ref[j:j + 1, :]
    tail = xp_buf[rows + pad - (CONV_W - 1):rows + pad, :]
    conv_out_ref[...] = tail
    xp_buf[pad - (CONV_W - 1):pad, :] = tail

    a, u_in = _rglru_gates(xc, wg_ref, b_a_ref[...], b_x_ref[...], lam_ref[...])
    a_s[...] = a
    u_s[...] = u_in

    row = lax.broadcasted_iota(jnp.int32, (SUBLANES, D_A), 0)

    def group(g, h_prev):
        r0 = pl.multiple_of(g * SUBLANES, SUBLANES)
        ag = a_s[pl.ds(r0, SUBLANES), :]
        ug = u_s[pl.ds(r0, SUBLANES), :]
        for sh in (1, 2, 4):
            keep = row >= sh
            a_sh = pltpu.roll(ag, sh, 0)
            u_sh = pltpu.roll(ug, sh, 0)
            ug = jnp.where(keep, ag * u_sh + ug, ug)
            ag = jnp.where(keep, ag * a_sh, ag)
        hg = ag * h_prev + ug
        h_s[pl.ds(r0, SUBLANES), :] = hg
        return hg[SUBLANES - 1:SUBLANES, :]

    h_last = lax.fori_loop(0, rows // SUBLANES, group, h_carry[...], unroll=4)
    h_carry[...] = h_last
    h_out_ref[...] = h_last

    y = h_s[...] * jax.nn.gelu(proj(1))
    ya_ref[...] = _rms_norm(y, gna_ref[...]).astype(BF16)


def _inproj_prompt(x, mod3, w_in_bf, cos_t, sin_t, rg):
    b, s, _ = x.shape
    tile = 512
    nt = s // tile
    row_spec = lambda w: pl.BlockSpec((None, tile, w), lambda i, j: (i, j, 0))
    vec = lambda r, w: pl.BlockSpec((r, w), lambda i, j: (0, 0))
    outs = pl.pallas_call(
        _inproj_prompt_kernel,
        grid=(b, nt),
        in_specs=[row_spec(D_MODEL),
                  pl.BlockSpec((None, 6, D_MODEL), lambda i, j: (i, 0, 0)),
                  vec(D_MODEL, 5 * D_A),
                  pl.BlockSpec((tile, D_B), lambda i, j: (j, 0)),
                  pl.BlockSpec((tile, D_B), lambda i, j: (j, 0)),
                  vec(CONV_W, D_A), vec(1, D_A),
                  pl.BlockSpec((2, D_A // 2, D_A), lambda i, j: (0, 0, 0)),
                  vec(1, D_A), vec(1, D_A), vec(1, D_A), vec(1, D_A)],
        out_specs=[row_spec(D_A), row_spec(D_B), row_spec(D_B), row_spec(D_B),
                   pl.BlockSpec((None, CONV_W - 1, D_A), lambda i, j: (i, 0, 0)),
                   pl.BlockSpec((None, 1, D_A), lambda i, j: (i, 0, 0))],
        out_shape=[jax.ShapeDtypeStruct((b, s, D_A), BF16),
                   jax.ShapeDtypeStruct((b, s, D_B), F32),
                   jax.ShapeDtypeStruct((b, s, D_B), F32),
                   jax.ShapeDtypeStruct((b, s, D_B), F32),
                   jax.ShapeDtypeStruct((b, CONV_W - 1, D_A), F32),
                   jax.ShapeDtypeStruct((b, 1, D_A), F32)],
        scratch_shapes=[pltpu.VMEM((tile + SUBLANES, D_A), F32),
                        pltpu.VMEM((tile, D_A), F32),
                        pltpu.VMEM((tile, D_A), F32),
                        pltpu.VMEM((tile, D_A), F32),
                        pltpu.VMEM((1, D_A), F32)],
        compiler_params=_cparams(("arbitrary", "arbitrary")),
        name="inproj_prompt",
    )(x, mod3, w_in_bf, cos_t, sin_t, rg["conv_w"], rg["conv_b"], rg["w_gate"],
      rg["b_a"], rg["b_x"], rg["lam"], rg["g_norm_a"])
    return outs


def _attn_prompt_kernel(q_ref, k_ref, v_ref, o_ref, kwin_ref, vwin_ref, acc_s, m_s, l_s, bias_s):
    s = q_ref.shape[0]
    keep = kwin_ref.shape[0]
    kwin_ref[...] = k_ref[s - keep:s, :]
    vwin_ref[...] = v_ref[s - keep:s, :]

    lane = lax.broadcasted_iota(jnp.int32, (Q_BLK, HEAD_PAIR_W), 1)
    head0 = lane < HEAD_DIM
    nk = 2 * Q_BLK

    qi = lax.broadcasted_iota(jnp.int32, (Q_BLK, nk), 0)
    ki = lax.broadcasted_iota(jnp.int32, (Q_BLK, nk), 1)
    for slot in range(2):
        dist = slot * Q_BLK + qi - ki
        bias_s[slot] = jnp.where((dist >= 0) & (dist <= N_KEYS), 0.0, NEG_INF)

    def rows(start, n, d):
        return pl.ds(start, n) if d == 1 else pl.ds(start, n, stride=d)

    def unit(bi, d, nb, u):
        r = u // nb
        j = u % nb
        jk = jnp.maximum(j - 1, 0)
        start_q = r + d * Q_BLK * j
        start_k = r + d * Q_BLK * jk
        bias = bias_s[j - jk]
        qb = q_ref[rows(start_q, Q_BLK, d), :] * (HEAD_DIM ** -0.5)
        kb = k_ref[rows(start_k, nk, d), :].astype(BF16)
        vb = v_ref[rows(start_k, nk, d), :].astype(BF16)
        stats = []
        for sel in (head0, jnp.logical_not(head0)):
            qh = jnp.where(sel, qb, 0.0).astype(BF16)
            sc = lax.dot_general(qh, kb, (((1,), (1,)), ((), ())), preferred_element_type=F32)
            sc = sc + bias
            m = jnp.max(sc, axis=-1, keepdims=True)
            p = jnp.exp(sc - m)
            l = jnp.sum(p, axis=-1, keepdims=True)
            o = jnp.dot(p.astype(BF16), vb, preferred_element_type=F32)
            stats.append((m, l, o))
        (m0, l0, o0), (m1, l1, o1) = stats
        dst = rows(start_q, Q_BLK, d)
        acc_s[bi, dst, :] = jnp.where(head0, o0, o1)
        m_s[bi, dst, :] = jnp.where(head0, m0, m1)
        l_s[bi, dst, :] = jnp.where(head0, l0, l1)

    for bi, (_, d) in enumerate(DILATED_CFG):
        nb = s // d // Q_BLK

        def body(u, carry, bi=bi, d=d, nb=nb):
            unit(bi, d, nb, u)
            return carry

        lax.fori_loop(0, d * nb, body, 0, unroll=ATTN_UNROLL)

    chunk = 512

    def merge(i, carry):
        r0 = pl.multiple_of(i * chunk, chunk)
        sl = pl.ds(r0, chunk)
        ms = [m_s[bi, sl, :] for bi in range(len(DILATED_CFG))]
        mx = functools.reduce(jnp.maximum, ms)
        num = jnp.zeros((chunk, HEAD_PAIR_W), F32)
        den = jnp.zeros((chunk, HEAD_PAIR_W), F32)
        for bi in range(len(DILATED_CFG)):
            w = jnp.exp(ms[bi] - mx)
            num = num + w * acc_s[bi, sl, :]
            den = den + w * l_s[bi, sl, :]
        o_ref[sl, :] = num / den
        return carry

    lax.fori_loop(0, s // chunk, merge, 0)


def _attn_prompt(q, k, v):
    b, s, _ = q.shape
    keep = min(WIN_MAX, s)
    nbr = len(DILATED_CFG)
    spec = pl.BlockSpec((None, s, HEAD_PAIR_W), lambda i, j: (i, 0, j))
    wspec = pl.BlockSpec((None, keep, HEAD_PAIR_W), lambda i, j: (i, 0, j))
    return pl.pallas_call(
        _attn_prompt_kernel,
        grid=(b, N_HEAD_PAIRS),
        in_specs=[spec, spec, spec],
        out_specs=[spec, wspec, wspec],
        out_shape=[jax.ShapeDtypeStruct((b, s, D_B), F32),
                   jax.ShapeDtypeStruct((b, keep, D_B), F32),
                   jax.ShapeDtypeStruct((b, keep, D_B), F32)],
        scratch_shapes=[pltpu.VMEM((nbr, s, HEAD_PAIR_W), F32),
                        pltpu.VMEM((nbr, s, HEAD_PAIR_W), F32),
                        pltpu.VMEM((nbr, s, HEAD_PAIR_W), F32),
                        pltpu.VMEM((2, Q_BLK, 2 * Q_BLK), F32)],
        compiler_params=_cparams(("arbitrary", "arbitrary")),
        name="attn_prompt",
    )(q, k, v)


def _inproj_sample_kernel(x_ref, mod_ref, w_in_ref, cos_ref, sin_ref, conv_state_ref, h0_ref,
                          conv_w_ref, conv_b_ref, wg_ref, b_a_ref, b_x_ref, lam_ref, gna_ref,
                          ya_ref, q_ref, k_ref, v_ref, conv_out_ref, h_out_ref):
    nt, nb, _ = x_ref.shape
    sh1 = mod_ref[:, 0:D_MODEL]
    sc1 = mod_ref[:, D_MODEL:2 * D_MODEL]
    u = (x_ref[...] * (1.0 + sc1)[None] + sh1[None]).astype(BF16).reshape(nt * nb, D_MODEL)

    def proj(j):
        return jnp.dot(u, w_in_ref[:, j * D_A:(j + 1) * D_A], preferred_element_type=F32)

    cos = cos_ref[...].reshape(nt * nb, D_B)
    sin = sin_ref[...].reshape(nt * nb, D_B)
    q_ref[...] = _rope_apply(proj(2), cos, sin).reshape(nt, nb, D_B)
    k_ref[...] = _rope_apply(proj(3), cos, sin).reshape(nt, nb, D_B)
    v_ref[...] = proj(4).reshape(nt, nb, D_B)

    xa = proj(0).reshape(nt, nb, D_A)
    xp = [conv_state_ref[j] for j in range(CONV_W - 1)] + [xa[t] for t in range(nt)]
    xc = jnp.concatenate(
        [conv_b_ref[...] + sum(xp[t + j] * conv_w_ref[j:j + 1, :] for j in range(CONV_W))
         for t in range(nt)], axis=0)
    for j in range(CONV_W - 1):
        conv_out_ref[j] = xp[nt + j]

    a, u_in = _rglru_gates(xc, wg_ref, b_a_ref[...], b_x_ref[...], lam_ref[...])
    h = h0_ref[...]
    hs = []
    for t in range(nt):
        h = a[t * nb:(t + 1) * nb] * h + u_in[t * nb:(t + 1) * nb]
        hs.append(h)
    h_out_ref[...] = h
    y = jnp.concatenate(hs, axis=0) * jax.nn.gelu(proj(1))
    ya_ref[...] = _rms_norm(y, gna_ref[...]).astype(BF16).reshape(nt, nb, D_A)


def _inproj_sample(x_tb, mod_s, w_in_bf, cos_t, sin_t, conv_state_tb, h0, rg):
    nt, nb, _ = x_tb.shape
    full = lambda shape: pl.BlockSpec(shape, lambda i: (0,) * len(shape))
    return pl.pallas_call(
        _inproj_sample_kernel,
        grid=(1,),
        in_specs=[full((nt, nb, D_MODEL)), full((nb, 6 * D_MODEL)), full((D_MODEL, 5 * D_A)),
                  full((nt, nb, D_B)), full((nt, nb, D_B)),
                  full((CONV_W - 1, nb, D_A)), full((nb, D_A)),
                  full((CONV_W, D_A)), full((1, D_A)), full((2, D_A // 2, D_A)),
                  full((1, D_A)), full((1, D_A)), full((1, D_A)), full((1, D_A))],
        out_specs=[full((nt, nb, D_A)), full((nt, nb, D_B)), full((nt, nb, D_B)),
                   full((nt, nb, D_B)), full((CONV_W - 1, nb, D_A)), full((nb, D_A))],
        out_shape=[jax.ShapeDtypeStruct((nt, nb, D_A), BF16),
                   jax.ShapeDtypeStruct((nt, nb, D_B), F32),
                   jax.ShapeDtypeStruct((nt, nb, D_B), F32),
                   jax.ShapeDtypeStruct((nt, nb, D_B), F32),
                   jax.ShapeDtypeStruct((CONV_W - 1, nb, D_A), F32),
                   jax.ShapeDtypeStruct((nb, D_A), F32)],
        compiler_params=_cparams(("arbitrary",)),
        name="inproj_sample",
    )(x_tb, mod_s, w_in_bf, cos_t, sin_t, conv_state_tb, h0, rg["conv_w"], rg["conv_b"],
      rg["w_gate"], rg["b_a"], rg["b_x"], rg["lam"], rg["g_norm_a"])


def _attn_sample_kernel(q_ref, kn_ref, vn_ref, ck_ref, cv_ref, o_ref, kwin_ref, vwin_ref,
                        kn_pad, vn_pad):
    nt = q_ref.shape[0]
    n_buf = ck_ref.shape[-1]
    n_rows = N_HEADS_B * nt

    @pl.when(pl.program_id(0) == 0)
    def _():
        kn_pad[...] = jnp.zeros_like(kn_pad)
        vn_pad[...] = jnp.zeros_like(vn_pad)

    kn_pad[0:nt, :] = kn_ref[...]
    vn_pad[0:nt, :] = vn_ref[...]
    ck_t = ck_ref[...].reshape(D_B, n_buf)
    cv_t = cv_ref[...].reshape(D_B, n_buf)

    tail_lane = lax.broadcasted_iota(jnp.int32, (D_B, LANES), 1)

    def shift_in(old_t, new_pad, out_ref):
        rolled = pltpu.roll(old_t, n_buf - nt, 1)
        new_t = pltpu.roll(new_pad.T, LANES - nt, 1)
        last = jnp.where(tail_lane < LANES - nt, rolled[:, n_buf - LANES:n_buf], new_t)
        out_ref[:, :, 0:n_buf - LANES] = rolled[:, 0:n_buf - LANES].reshape(
            N_HEADS_B, HEAD_DIM, n_buf - LANES)
        out_ref[:, :, n_buf - LANES:n_buf] = last.reshape(N_HEADS_B, HEAD_DIM, LANES)

    shift_in(ck_t, kn_pad[...], kwin_ref)
    shift_in(cv_t, vn_pad[...], vwin_ref)

    ri = lax.broadcasted_iota(jnp.int32, (n_rows, nt), 0)
    ci = lax.broadcasted_iota(jnp.int32, (n_rows, nt), 1)
    pick = (ri % nt == ci).astype(BF16)
    qs = (q_ref[...] * (HEAD_DIM ** -0.5)).astype(BF16)
    q_rep = jnp.dot(pick, qs, preferred_element_type=F32)
    row_h = lax.broadcasted_iota(jnp.int32, (n_rows, D_B), 0) // nt
    lane_h = lax.broadcasted_iota(jnp.int32, (n_rows, D_B), 1) // HEAD_DIM
    own = row_h == lane_h
    qbd = jnp.where(own, q_rep, 0.0).astype(BF16)

    nt_dims = (((1,), (1,)), ((), ()))
    sc_c = jnp.dot(qbd, ck_t.astype(BF16), preferred_element_type=F32)
    sc_n = lax.dot_general(qbd, kn_pad[...].astype(BF16), nt_dims, preferred_element_type=F32)

    def mult(dist, limit_ok):
        c = jnp.zeros(dist.shape, F32)
        for win, d in DILATED_CFG:
            hit = (dist >= 0) & (dist <= win) & (dist % d == 0) & limit_ok
            c = c + hit.astype(F32)
        return c

    t_c = lax.broadcasted_iota(jnp.int32, sc_c.shape, 0) % nt
    dist_c = n_buf + t_c - lax.broadcasted_iota(jnp.int32, sc_c.shape, 1)
    mult_c = mult(dist_c, dist_c >= 0)
    t_n = lax.broadcasted_iota(jnp.int32, sc_n.shape, 0) % nt
    col_n = lax.broadcasted_iota(jnp.int32, sc_n.shape, 1)
    mult_n = mult(t_n - col_n, col_n < nt)

    sc_c = jnp.where(mult_c > 0, sc_c, NEG_INF)
    sc_n = jnp.where(mult_n > 0, sc_n, NEG_INF)
    m = jnp.maximum(jnp.max(sc_c, axis=-1, keepdims=True), jnp.max(sc_n, axis=-1, keepdims=True))
    p_c = mult_c * jnp.exp(sc_c - m)
    p_n = mult_n * jnp.exp(sc_n - m)
    l = jnp.sum(p_c, axis=-1, keepdims=True) + jnp.sum(p_n, axis=-1, keepdims=True)
    acc = (lax.dot_general(p_c.astype(BF16), cv_t.astype(BF16), nt_dims,
                           preferred_element_type=F32)
           + jnp.dot(p_n.astype(BF16), vn_pad[...].astype(BF16), preferred_element_type=F32))
    o_full = jnp.where(own, acc / l, 0.0)
    out = o_full[0:nt, :]
    for h in range(1, N_HEADS_B):
        out = out + o_full[h * nt:(h + 1) * nt, :]
    o_ref[...] = out


def _attn_sample(q, k_new, v_new, cache_k_t, cache_v_t):
    b, nt, _ = q.shape
    n_buf = cache_k_t.shape[-1]
    small = pl.BlockSpec((None, nt, D_B), lambda i: (i, 0, 0))
    big = pl.BlockSpec((None, N_HEADS_B, HEAD_DIM, n_buf), lambda i: (i, 0, 0, 0))
    win = jax.ShapeDtypeStruct((b, N_HEADS_B, HEAD_DIM, n_buf), F32)
    return pl.pallas_call(
        _attn_sample_kernel,
        grid=(b,),
        in_specs=[small, small, small, big, big],
        out_specs=[small, big, big],
        out_shape=[jax.ShapeDtypeStruct((b, nt, D_B), F32), win, win],
        scratch_shapes=[pltpu.VMEM((LANES, D_B), F32), pltpu.VMEM((LANES, D_B), F32)],
        compiler_params=_cparams(("arbitrary",)),
        name="attn_sample",
    )(q, k_new, v_new, cache_k_t, cache_v_t)


def _split_bf16(x):
    hi = x.astype(BF16)
    lo = (x - hi.astype(F32)).astype(BF16)
    return hi, lo


def _first_argmax(vals):
    mx = functools.reduce(jnp.maximum, vals)
    idx = jnp.full(mx.shape, float(len(vals) - 1), F32)
    for j in range(len(vals) - 2, -1, -1):
        idx = jnp.where(vals[j] == mx, float(j), idx)
    return mx, idx


def _outproj_kernel(ya_ref, yb_ref, x_ref, gt1_ref, sh2_ref, sc2_ref, w_out_ref, gnb_ref,
                    ln_g_ref, ln_b_ref, w_r_ref, b_r_ref, x1_ref, u2_ref, route_ref):
    yb = _rms_norm(yb_ref[...], gnb_ref[...]).astype(BF16)
    mixed = (jnp.dot(ya_ref[...], w_out_ref[0:D_A, :], preferred_element_type=F32)
             + jnp.dot(yb, w_out_ref[D_A:D_A + D_B, :], preferred_element_type=F32))
    x1 = _layer_norm(DN_ALPHA * x_ref[...] + gt1_ref[...] * mixed, ln_g_ref[...], ln_b_ref[...])
    x1_ref[...] = x1
    u2 = x1 * (1.0 + sc2_ref[...]) + sh2_ref[...]
    u2_ref[:, 0:D_MODEL] = u2

    u_hi, u_lo = _split_bf16(u2)
    w_hi, w_lo = _split_bf16(w_r_ref[...])
    logits = (jnp.dot(u_hi, w_hi, preferred_element_type=F32)
              + jnp.dot(u_lo, w_hi, preferred_element_type=F32)
              + jnp.dot(u_hi, w_lo, preferred_element_type=F32)) + b_r_ref[...]
    lt = logits.T

    g_rows = [lt[j:j + 1, :] for j in range(N_GROUPS)]
    g_max, g_idx = _first_argmax(g_rows)
    p_group = 1.0 / sum(jnp.exp(g - g_max) for g in g_rows)
    e_rows = []
    for e in range(N_EXP_PER_GROUP):
        acc = jnp.zeros_like(g_max)
        for g in range(N_GROUPS):
            r = N_GROUPS + g * N_EXP_PER_GROUP + e
            acc = jnp.where(g_idx == float(g), lt[r:r + 1, :], acc)
        e_rows.append(acc)
    v1, i1 = _first_argmax(e_rows)
    rest = [jnp.where(i1 == float(e), -jnp.inf, e_rows[e]) for e in range(N_EXP_PER_GROUP)]
    v2, i2 = _first_argmax(rest)
    ex = jnp.exp(v2 - v1)
    w1 = p_group / (1.0 + ex)
    w2 = p_group * ex / (1.0 + ex)
    lo = jnp.minimum(i1, i2)
    hi = jnp.maximum(i1, i2)
    pair = jnp.where(lo == 0.0, hi - 1.0, jnp.where(lo == 1.0, hi + 1.0, 5.0))
    cls = g_idx * float(N_PAIRS) + pair
    w_of_lo = jnp.where(i1 < i2, w1, w2)
    w_of_hi = jnp.where(i1 < i2, w2, w1)
    e_lo = g_idx * float(N_EXP_PER_GROUP) + lo
    e_hi = g_idx * float(N_EXP_PER_GROUP) + hi
    route = jnp.concatenate(
        [cls, w_of_lo, w_of_hi, e_lo, e_hi, jnp.zeros((ROUTE_W - 5, cls.shape[1]), F32)], axis=0)
    route_ref[...] = route[0:SUBLANES, :]
    u2_ref[:, D_MODEL:D_EXT] = route.T


def _outproj(ya, yb, x, mods, w_out_bf, gnb, ln_g, ln_b, w_r, b_r, tile):
    n = x.shape[0]
    nt = n // tile
    mod_arr, gt1_spec, sh2_spec, sc2_spec = mods
    row = lambda w: pl.BlockSpec((tile, w), lambda i: (i, 0))
    vec = lambda r, w: pl.BlockSpec((r, w), lambda i: (0, 0))
    return pl.pallas_call(
        _outproj_kernel,
        grid=(nt,),
        in_specs=[row(D_A), row(D_B), row(D_MODEL), gt1_spec, sh2_spec, sc2_spec,
                  vec(D_MODEL, D_MODEL), vec(1, D_B), vec(1, D_MODEL), vec(1, D_MODEL),
                  vec(D_MODEL, ROUTE_W), vec(1, ROUTE_W)],
        out_specs=[row(D_MODEL), row(D_EXT),
                   pl.BlockSpec((None, SUBLANES, tile), lambda i: (i, 0, 0))],
        out_shape=[jax.ShapeDtypeStruct((n, D_MODEL), F32),
                   jax.ShapeDtypeStruct((n, D_EXT), F32),
                   jax.ShapeDtypeStruct((nt, SUBLANES, tile), F32)],
        compiler_params=_cparams(("arbitrary",)),
        name="outproj_router",
    )(ya, yb, x, mod_arr, mod_arr, mod_arr, w_out_bf, gnb, ln_g, ln_b, w_r, b_r)


def _moe_kernel(e_lo_ref, e_hi_ref, n_used_ref, src_ref, dst_ref,
                x_hbm, wg_lo, wg_hi, wu_lo, wu_hi, wd_lo, wd_hi, o_hbm,
                xg, og, gsem, ssem, *, n_tokens, tile):
    i = pl.program_id(0)
    n_steps = pl.num_programs(0)
    n_used = n_used_ref[0]
    slot = i % 2
    other = 1 - slot

    def gather_row(base, r, sl):
        tok = src_ref[base + r]
        pltpu.make_async_copy(x_hbm.at[pl.ds(tok, 1)], xg.at[sl, pl.ds(r, 1)],
                              gsem.at[sl]).start()

    def gather_wait(sl):
        pltpu.make_async_copy(x_hbm.at[pl.ds(0, tile)], xg.at[sl], gsem.at[sl]).wait()

    def scatter_wait(sl):
        pltpu.make_async_copy(og.at[sl], o_hbm.at[pl.ds(0, tile)], ssem.at[sl]).wait()

    @pl.when(i == 0)
    def _():
        def first(r, c):
            gather_row(0, r, 0)
            return c
        lax.fori_loop(0, tile, first, 0, unroll=DMA_UNROLL)
        og[...] = jnp.zeros_like(og)
        for sl in range(2):
            pad_rows = pltpu.make_async_copy(
                og.at[sl], o_hbm.at[pl.ds(n_tokens + sl * tile, tile)], ssem.at[sl])
            pad_rows.start()
            pad_rows.wait()

    @pl.when(i <= n_used)
    def _():
        gather_wait(slot)

        @pl.when(i >= 1)
        def _():
            scatter_wait(slot)

        nxt = jnp.minimum(i + 1, n_steps - 1) * tile
        for r in range(tile):
            gather_row(nxt, r, other)
        prev = i * tile
        for r in range(tile):
            pltpu.make_async_copy(og.at[other, pl.ds(r, 1)],
                                  o_hbm.at[pl.ds(dst_ref[prev + r], 1)], ssem.at[other]).start()

        xe = xg[slot]
        xb = xe[:, 0:D_MODEL].astype(BF16)
        out = jnp.zeros((tile, D_MODEL), F32)
        for col, wg, wu, wd in ((1, wg_lo, wu_lo, wd_lo), (2, wg_hi, wu_hi, wd_hi)):
            hg = jnp.dot(xb, wg[...], preferred_element_type=F32)
            hu = jnp.dot(xb, wu[...], preferred_element_type=F32)
            act = (hg * jax.nn.sigmoid(hg)) * hu * xe[:, D_MODEL + col:D_MODEL + col + 1]
            out = out + jnp.dot(act.astype(BF16), wd[...], preferred_element_type=F32)
        og[slot] = out

        @pl.when(i == n_used)
        def _():
            gather_wait(other)
            scatter_wait(other)


def _moe(u2ext, cls, w_gate_bf, w_up_bf, w_down_bf, tile):
    n = u2ext.shape[0]
    n_steps = n // tile + N_CLASSES
    cls = cls.astype(jnp.int32)
    order = jnp.argsort(cls, stable=True).astype(jnp.int32)
    counts = jnp.zeros((N_CLASSES,), jnp.int32).at[cls].add(1)
    tiles_per = (counts + tile - 1) // tile
    tile_end = jnp.cumsum(tiles_per)
    tile_off = tile_end - tiles_per
    n_used = tile_end[-1]
    class_start = jnp.cumsum(counts) - counts
    step = jnp.arange(n_steps, dtype=jnp.int32)
    used = step < n_used
    cls_of = jnp.minimum(jnp.searchsorted(tile_end, step, side="right"), N_CLASSES - 1)
    last_cls = cls_of[jnp.maximum(n_used - 1, 0)]
    cls_of = jnp.where(used, cls_of, last_cls).astype(jnp.int32)
    local = step - tile_off[cls_of]
    nvalid = jnp.where(used, jnp.clip(counts[cls_of] - local * tile, 0, tile), 0).astype(jnp.int32)
    r = jnp.arange(tile, dtype=jnp.int32)
    pos = class_start[cls_of][:, None] + local[:, None] * tile + r[None, :]
    valid = r[None, :] < nvalid[:, None]
    tok = order[jnp.clip(pos, 0, n - 1)]
    src = jnp.where(valid, tok, 0).astype(jnp.int32).reshape(-1)
    spare = n + (step[:, None] % 2) * tile + r[None, :]
    dst = jnp.where(valid, tok, spare).astype(jnp.int32)
    dst = jnp.concatenate([(n + tile + r)[None, :], dst], axis=0).reshape(-1)
    grp = cls_of // N_PAIRS
    pair = cls_of % N_PAIRS
    pair_lo = jnp.array([0, 0, 0, 1, 1, 2], jnp.int32)[pair]
    pair_hi = jnp.array([1, 2, 3, 2, 3, 3], jnp.int32)[pair]
    e_lo = (grp * N_EXP_PER_GROUP + pair_lo).astype(jnp.int32)
    e_hi = (grp * N_EXP_PER_GROUP + pair_hi).astype(jnp.int32)

    w_in_spec = lambda which: pl.BlockSpec(
        (None, D_MODEL, D_EXPERT), lambda i, elo, ehi, nu, s, d: ((elo, ehi)[which][i], 0, 0))
    w_dn_spec = lambda which: pl.BlockSpec(
        (None, D_EXPERT, D_MODEL), lambda i, elo, ehi, nu, s, d: ((elo, ehi)[which][i], 0, 0))
    grid_spec = pltpu.PrefetchScalarGridSpec(
        num_scalar_prefetch=5,
        grid=(n_steps,),
        in_specs=[pl.BlockSpec(memory_space=pl.ANY),
                  w_in_spec(0), w_in_spec(1), w_in_spec(0), w_in_spec(1),
                  w_dn_spec(0), w_dn_spec(1)],
        out_specs=pl.BlockSpec(memory_space=pl.ANY),
        scratch_shapes=[pltpu.VMEM((2, tile, D_EXT), F32),
                        pltpu.VMEM((2, tile, D_MODEL), F32),
                        pltpu.SemaphoreType.DMA((2,)),
                        pltpu.SemaphoreType.DMA((2,))],
    )
    return pl.pallas_call(
        functools.partial(_moe_kernel, n_tokens=n, tile=tile),
        grid_spec=grid_spec,
        out_shape=jax.ShapeDtypeStruct((n + 2 * tile, D_MODEL), F32),
        compiler_params=_cparams(("arbitrary",)),
        name="moe_sparse",
    )(e_lo, e_hi, n_used.reshape(1).astype(jnp.int32), src, dst,
      u2ext, w_gate_bf, w_gate_bf, w_up_bf, w_up_bf, w_down_bf, w_down_bf)


def _moe_dense_kernel(x_ref, wg_ref, wu_ref, wd_ref, o_ref):
    e = pl.program_id(0)

    @pl.when(e == 0)
    def _():
        o_ref[...] = jnp.zeros_like(o_ref)

    xe = x_ref[...]
    xb = xe[:, 0:D_MODEL].astype(BF16)
    ef = e.astype(F32)
    col = lambda c: xe[:, D_MODEL + c:D_MODEL + c + 1]
    comb = jnp.where(col(3) == ef, col(1), 0.0) + jnp.where(col(4) == ef, col(2), 0.0)
    hg = jnp.dot(xb, wg_ref[...], preferred_element_type=F32)
    hu = jnp.dot(xb, wu_ref[...], preferred_element_type=F32)
    act = (hg * jax.nn.sigmoid(hg)) * hu * comb
    o_ref[...] += jnp.dot(act.astype(BF16), wd_ref[...], preferred_element_type=F32)


def _moe_dense(u2ext, w_gate_bf, w_up_bf, w_down_bf):
    n = u2ext.shape[0]
    return pl.pallas_call(
        _moe_dense_kernel,
        grid=(N_EXPERTS,),
        in_specs=[pl.BlockSpec((n, D_EXT), lambda e: (0, 0)),
                  pl.BlockSpec((None, D_MODEL, D_EXPERT), lambda e: (e, 0, 0)),
                  pl.BlockSpec((None, D_MODEL, D_EXPERT), lambda e: (e, 0, 0)),
                  pl.BlockSpec((None, D_EXPERT, D_MODEL), lambda e: (e, 0, 0))],
        out_specs=pl.BlockSpec((n, D_MODEL), lambda e: (0, 0)),
        out_shape=jax.ShapeDtypeStruct((n, D_MODEL), F32),
        compiler_params=_cparams(("arbitrary",)),
        name="moe_dense",
    )(u2ext, w_gate_bf, w_up_bf, w_down_bf)


def _final_kernel(x1_ref, ffn_ref, gt2_ref, g_ref, b_ref, o_ref):
    o_ref[...] = _layer_norm(DN_ALPHA * x1_ref[...] + gt2_ref[...] * ffn_ref[...],
                             g_ref[...], b_ref[...])


def _final_norm(x1, ffn_padded, mod_arr, gt2_spec, ln_g, ln_b, tile):
    n = x1.shape[0]
    row = pl.BlockSpec((tile, D_MODEL), lambda i: (i, 0))
    vec = pl.BlockSpec((1, D_MODEL), lambda i: (0, 0))
    return pl.pallas_call(
        _final_kernel,
        grid=(n // tile,),
        in_specs=[row, row, gt2_spec, vec, vec],
        out_specs=row,
        out_shape=jax.ShapeDtypeStruct((n, D_MODEL), F32),
        compiler_params=_cparams(("arbitrary",)),
        name="final_norm",
    )(x1, ffn_padded, mod_arr, ln_g, ln_b)


def _rope_tables(pos):
    half = HEAD_DIM // 2
    inv = ROPE_THETA ** (-jnp.arange(half, dtype=F32) * 2.0 / HEAD_DIM)
    ang = pos.astype(F32)[:, None] * inv[None, :]
    cos = jnp.cos(ang)
    sin = jnp.sin(ang)
    cos_t = jnp.tile(jnp.concatenate([cos, cos], axis=-1), (1, N_HEADS_B))
    sin_t = jnp.tile(jnp.concatenate([-sin, sin], axis=-1), (1, N_HEADS_B))
    return cos_t, sin_t


def _block_diag(w):
    n, a, b = w.shape
    eye = jnp.eye(n, dtype=w.dtype)
    return (eye[:, None, :, None] * w[:, :, None, :]).reshape(n * a, n * b)


def _prepare_weights(w_in, conv_w, conv_b, w_rg_a, b_rg_a, w_rg_x, b_rg_x, rg_lambda, g_norm_a,
                     w_router_group, b_router_group, w_router_expert, b_router_expert):
    half_blocks = N_BLK_A // 2
    w_gate = jnp.stack([
        jnp.concatenate([_block_diag(w_rg_a[h * half_blocks:(h + 1) * half_blocks]),
                         _block_diag(w_rg_x[h * half_blocks:(h + 1) * half_blocks])], axis=1)
        for h in range(2)]).astype(BF16)
    rg = dict(conv_w=conv_w, conv_b=conv_b.reshape(1, D_A), w_gate=w_gate,
              b_a=b_rg_a.reshape(1, D_A), b_x=b_rg_x.reshape(1, D_A),
              lam=rg_lambda.reshape(1, D_A), g_norm_a=g_norm_a.reshape(1, D_A))
    n_logits = N_GROUPS + N_EXPERTS
    w_r = jnp.concatenate(
        [w_router_group,
         w_router_expert.transpose(1, 0, 2).reshape(D_MODEL, N_EXPERTS),
         jnp.zeros((D_MODEL, ROUTE_W - n_logits), F32)], axis=1)
    b_r = jnp.concatenate([b_router_group, b_router_expert.reshape(-1),
                           jnp.zeros((ROUTE_W - n_logits,), F32)]).reshape(1, ROUTE_W)
    return rg, w_r, b_r


def _channel_mixing(ya, yb, x, mods, gt2_spec, weights, tile, moe_tile):
    (w_out_bf, gnb, ln1_g, ln1_b, w_r, b_r, wg_bf, wu_bf, wd_bf, ln2_g, ln2_b) = weights
    x1, u2ext, route = _outproj(ya, yb, x, mods, w_out_bf, gnb, ln1_g, ln1_b, w_r, b_r, tile)
    if moe_tile is None:
        ffn = _moe_dense(u2ext, wg_bf, wu_bf, wd_bf)
    else:
        cls = route[:, 0, :].reshape(-1)
        ffn = _moe(u2ext, cls, wg_bf, wu_bf, wd_bf, moe_tile)
    return _final_norm(x1, ffn, mods[0], gt2_spec, ln2_g, ln2_b, tile)


def kernel(x_prompt, x_sample, state_conv, state_rglru, cache_win_k, cache_win_v, c_prompt, c_sample, w_ada, b_ada, w_in, conv_w, conv_b, w_rg_a, b_rg_a, w_rg_x, b_rg_x, rg_lambda, g_norm_a, g_norm_b, w_out, ln1_g, ln1_b, w_router_group, b_router_group, w_router_expert, b_router_expert, w_exp_gate, w_exp_up, w_exp_down, ln2_g, ln2_b):
    bp, sp, _ = x_prompt.shape
    bs, ts, _ = x_sample.shape
    n_buf = cache_win_k.shape[1]

    rg, w_r, b_r = _prepare_weights(w_in, conv_w, conv_b, w_rg_a, b_rg_a, w_rg_x, b_rg_x,
                                    rg_lambda, g_norm_a, w_router_group, b_router_group,
                                    w_router_expert, b_router_expert)
    w_in_bf = w_in.astype(BF16)
    mix_weights = (w_out.astype(BF16), g_norm_b.reshape(1, D_B), ln1_g.reshape(1, D_MODEL),
                   ln1_b.reshape(1, D_MODEL), w_r, b_r, w_exp_gate.astype(BF16),
                   w_exp_up.astype(BF16), w_exp_down.astype(BF16),
                   ln2_g.reshape(1, D_MODEL), ln2_b.reshape(1, D_MODEL))

    mod = _modulation(jnp.concatenate([c_prompt, c_sample], axis=0), w_ada, b_ada)
    mod_p, mod_s = mod[:bp], mod[bp:]

    tile_p = 512
    cos_p, sin_p = _rope_tables(jnp.arange(sp))
    ya_p, q_p, k_p, v_p, conv_p, h_p = _inproj_prompt(
        x_prompt, mod_p.reshape(bp, 6, D_MODEL), w_in_bf, cos_p, sin_p, rg)
    yb_p, kwin_p, vwin_p = _attn_prompt(q_p, k_p, v_p)
    tiles_per_seq = sp // tile_p
    mod_p3 = mod_p.reshape(bp * 6, 1, D_MODEL)
    mod_spec_p = lambda j: pl.BlockSpec((None, 1, D_MODEL),
                                        lambda i: ((i // tiles_per_seq) * 6 + j, 0, 0))
    y_p = _channel_mixing(
        ya_p.reshape(bp * sp, D_A), yb_p.reshape(bp * sp, D_B), x_prompt.reshape(bp * sp, D_MODEL),
        (mod_p3, mod_spec_p(2), mod_spec_p(3), mod_spec_p(4)), mod_spec_p(5),
        mix_weights, tile_p, 256)

    cos_s, sin_s = _rope_tables(PAST_LEN + jnp.arange(ts))
    tb = lambda t: jnp.broadcast_to(t[:, None, :], (ts, bs, D_B))
    ya_s, q_s, k_s, v_s, conv_s, h_s = _inproj_sample(
        x_sample.transpose(1, 0, 2), mod_s, w_in_bf, tb(cos_s), tb(sin_s),
        state_conv.transpose(1, 0, 2), state_rglru, rg)
    bt = lambda t: t.transpose(1, 0, 2)
    yb_s, kwin_s, vwin_s = _attn_sample(
        bt(q_s), bt(k_s), bt(v_s), cache_win_k.transpose(0, 2, 3, 1),
        cache_win_v.transpose(0, 2, 3, 1))
    mod_spec_s = lambda j: pl.BlockSpec((bs, D_MODEL), lambda i: (0, j))
    y_s = _channel_mixing(
        ya_s.reshape(ts * bs, D_A), bt(yb_s).reshape(ts * bs, D_B),
        x_sample.transpose(1, 0, 2).reshape(ts * bs, D_MODEL),
        (mod_s, mod_spec_s(2), mod_spec_s(3), mod_spec_s(4)), mod_spec_s(5),
        mix_weights, bs, None)

    heads = lambda t: t.reshape(t.shape[0], t.shape[1], N_HEADS_B, HEAD_DIM)
    return (y_p.reshape(bp, sp, D_MODEL), bt(y_s.reshape(ts, bs, D_MODEL)),
            conv_p, h_p.reshape(bp, D_A), heads(kwin_p), heads(vwin_p),
            bt(conv_s), h_s, kwin_s.transpose(0, 3, 1, 2), vwin_s.transpose(0, 3, 1, 2))
```

```python
import functools
import math

import jax
import jax.numpy as jnp
from jax import lax
from jax.experimental import pallas as pl
from jax.experimental.pallas import tpu as pltpu

F32 = jnp.float32
BF16 = jnp.bfloat16

D_MODEL = 1024
D_A = 512
N_BLK_A = 8
BLK_W_A = D_A // N_BLK_A
CONV_W = 4
RG_C = 8.0
D_B = 512
HEAD_DIM = 64
N_HEADS_B = D_B // HEAD_DIM
DILATED_CFG = ((128, 1), (512, 4), (2048, 16))
WIN_MAX = 2048
N_KEYS = 128
ROPE_THETA = 10000.0
PAST_LEN = 8192
N_GROUPS = 4
N_EXP_PER_GROUP = 4
N_EXPERTS = N_GROUPS * N_EXP_PER_GROUP
D_EXPERT = 512
DN_ALPHA = 2.0 ** 0.25
LN_EPS = 1e-5
NEG_INF = -1e30

N_PAIRS = 6
N_CLASSES = N_GROUPS * N_PAIRS
LANES = 128
SUBLANES = 8
HEAD_PAIR_W = 2 * HEAD_DIM
N_HEAD_PAIRS = N_HEADS_B // 2
Q_BLK = 128
ATTN_UNROLL = 8
DMA_UNROLL = 8
ROUTE_W = LANES
TOKEN_ROWS = D_MODEL // LANES
N_BUF = 3
VMEM_LIMIT = 56 * 1024 * 1024


def _cparams(sem):
    return pltpu.CompilerParams(dimension_semantics=sem, vmem_limit_bytes=VMEM_LIMIT)


def _mod_kernel(c_ref, w_ref, b_ref, o_ref):
    c = c_ref[...]
    s = (c * jax.nn.sigmoid(c)).astype(BF16)
    o_ref[...] = jnp.dot(s, w_ref[...].astype(BF16), preferred_element_type=F32) + b_ref[...]


def _modulation(c_all, w_ada, b_ada):
    n = c_all.shape[0]
    tn = 1024
    return pl.pallas_call(
        _mod_kernel,
        grid=(6 * D_MODEL // tn,),
        in_specs=[pl.BlockSpec((n, D_MODEL), lambda j: (0, 0)),
                  pl.BlockSpec((D_MODEL, tn), lambda j: (0, j)),
                  pl.BlockSpec((1, tn), lambda j: (0, j))],
        out_specs=pl.BlockSpec((n, tn), lambda j: (0, j)),
        out_shape=jax.ShapeDtypeStruct((n, 6 * D_MODEL), F32),
        compiler_params=_cparams(("arbitrary",)),
        name="adaln_mod",
    )(c_all, w_ada, b_ada.reshape(1, -1))


def _rope_apply(t, cos, sin_signed):
    lane = lax.broadcasted_iota(jnp.int32, t.shape, t.ndim - 1)
    first_half = (lane & (HEAD_DIM - 1)) < HEAD_DIM // 2
    width = t.shape[-1]
    swapped = jnp.where(first_half,
                        pltpu.roll(t, width - HEAD_DIM // 2, t.ndim - 1),
                        pltpu.roll(t, HEAD_DIM // 2, t.ndim - 1))
    return t * cos + swapped * sin_signed


def _rglru_gates(xc, wg_ref, b_a, b_x, lam):
    half = D_A // 2
    xcb = xc.astype(BF16)
    g0 = jnp.dot(xcb[:, :half], wg_ref[0], preferred_element_type=F32)
    g1 = jnp.dot(xcb[:, half:], wg_ref[1], preferred_element_type=F32)
    r = jax.nn.sigmoid(jnp.concatenate([g0[:, :half], g1[:, :half]], axis=1) + b_a)
    i = jax.nn.sigmoid(jnp.concatenate([g0[:, half:], g1[:, half:]], axis=1) + b_x)
    z = -lam
    softplus = jnp.maximum(z, 0.0) + jnp.log1p(jnp.exp(-jnp.abs(z)))
    log_a = -RG_C * r * softplus
    a = jnp.exp(log_a)
    one_minus_a2 = -jnp.tanh(log_a) * (a * a + 1.0)
    u = jnp.sqrt(one_minus_a2) * (i * xc)
    return a, u


def _store_token_tiles(ref, x):
    n = x.shape[0]
    for c in range(TOKEN_ROWS):
        ref[pl.ds(c, n, stride=TOKEN_ROWS), :] = x[:, c * LANES:(c + 1) * LANES]


def _load_token_tiles(ref, n):
    return jnp.concatenate(
        [ref[pl.ds(c, n, stride=TOKEN_ROWS), :] for c in range(TOKEN_ROWS)], axis=1)


def _rms_norm(y, g):
    return y * lax.rsqrt(jnp.mean(y * y, axis=-1, keepdims=True) + LN_EPS) * g


def _layer_norm(x, g, b):
    mu = jnp.mean(x, axis=-1, keepdims=True)
    xc = x - mu
    var = jnp.mean(xc * xc, axis=-1, keepdims=True)
    return xc * lax.rsqrt(var + LN_EPS) * g + b


def _inproj_prompt_kernel(x_ref, mod_ref, w_in_ref, cos_ref, sin_ref, conv_w_ref, conv_b_ref,
                          wg_ref, b_a_ref, b_x_ref, lam_ref, gna_ref,
                          ya_ref, q_ref, k_ref, v_ref, conv_out_ref, h_out_ref,
                          xp_buf, a_s, u_s, h_s, h_carry):
    t = pl.program_id(1)
    rows = x_ref.shape[0]
    pad = SUBLANES

    @pl.when(t == 0)
    def _():
        xp_buf[0:pad, :] = jnp.zeros((pad, D_A), F32)
        h_carry[...] = jnp.zeros_like(h_carry)

    u = (x_ref[...] * (1.0 + mod_ref[1:2, :]) + mod_ref[0:1, :]).astype(BF16)

    def proj(j):
        return jnp.dot(u, w_in_ref[:, j * D_A:(j + 1) * D_A], preferred_element_type=F32)

    cos = cos_ref[...]
    sin = sin_ref[...]
    q_ref[...] = _rope_apply(proj(2), cos, sin)
    k_ref[...] = _rope_apply(proj(3), cos, sin)
    v_ref[...] = proj(4)

    xa = proj(0)
    xp_buf[pad:pad + rows, :] = xa
    xc = conv_b_ref[...] + xa * conv_w_ref[CONV_W - 1:CONV_W, :]
    for j in range(CONV_W - 1):
        off = pad - (CONV_W - 1) + j
        xc = xc + xp_buf[off:off + rows, :] * conv_w_ref[j:j + 1, :]
    tail = xp_buf[rows + pad - (CONV_W - 1):rows + pad, :]
    conv_out_ref[...] = tail
    xp_buf[pad - (CONV_W - 1):pad, :] = tail

    a, u_in = _rglru_gates(xc, wg_ref, b_a_ref[...], b_x_ref[...], lam_ref[...])
    a_s[...] = a
    u_s[...] = u_in

    row = lax.broadcasted_iota(jnp.int32, (SUBLANES, D_A), 0)

    def group(g, h_prev):
        r0 = pl.multiple_of(g * SUBLANES, SUBLANES)
        ag = a_s[pl.ds(r0, SUBLANES), :]
        ug = u_s[pl.ds(r0, SUBLANES), :]
        for sh in (1, 2, 4):
            keep = row >= sh
            a_sh = pltpu.roll(ag, sh, 0)
            u_sh = pltpu.roll(ug, sh, 0)
            ug = jnp.where(keep, ag * u_sh + ug, ug)
            ag = jnp.where(keep, ag * a_sh, ag)
        hg = ag * h_prev + ug
        h_s[pl.ds(r0, SUBLANES), :] = hg
        return hg[SUBLANES - 1:SUBLANES, :]

    h_last = lax.fori_loop(0, rows // SUBLANES, group, h_carry[...], unroll=4)
    h_carry[...] = h_last
    h_out_ref[...] = h_last

    y = h_s[...] * jax.nn.gelu(proj(1))
    ya_ref[...] = _rms_norm(y, gna_ref[...]).astype(BF16)


def _inproj_prompt(x, mod3, w_in_bf, cos_t, sin_t, rg):
    b, s, _ = x.shape
    tile = 512
    nt = s // tile
    row_spec = lambda w: pl.BlockSpec((None, tile, w), lambda i, j: (i, j, 0))
    vec = lambda r, w: pl.BlockSpec((r, w), lambda i, j: (0, 0))
    outs = pl.pallas_call(
        _inproj_prompt_kernel,
        grid=(b, nt),
        in_specs=[row_spec(D_MODEL),
                  pl.BlockSpec((None, 6, D_MODEL), lambda i, j: (i, 0, 0)),
                  vec(D_MODEL, 5 * D_A),
                  pl.BlockSpec((tile, D_B), lambda i, j: (j, 0)),
                  pl.BlockSpec((tile, D_B), lambda i, j: (j, 0)),
                  vec(CONV_W, D_A), vec(1, D_A),
                  pl.BlockSpec((2, D_A // 2, D_A), lambda i, j: (0, 0, 0)),
                  vec(1, D_A), vec(1, D_A), vec(1, D_A), vec(1, D_A)],
        out_specs=[row_spec(D_A), row_spec(D_B), row_spec(D_B), row_spec(D_B),
                   pl.BlockSpec((None, CONV_W - 1, D_A), lambda i, j: (i, 0, 0)),
                   pl.BlockSpec((None, 1, D_A), lambda i, j: (i, 0, 0))],
        out_shape=[jax.ShapeDtypeStruct((b, s, D_A), BF16),
                   jax.ShapeDtypeStruct((b, s, D_B), F32),
                   jax.ShapeDtypeStruct((b, s, D_B), F32),
                   jax.ShapeDtypeStruct((b, s, D_B), F32),
                   jax.ShapeDtypeStruct((b, CONV_W - 1, D_A), F32),
                   jax.ShapeDtypeStruct((b, 1, D_A), F32)],
        scratch_shapes=[pltpu.VMEM((tile + SUBLANES, D_A), F32),
                        pltpu.VMEM((tile, D_A), F32),
                        pltpu.VMEM((tile, D_A), F32),
                        pltpu.VMEM((tile, D_A), F32),
                        pltpu.VMEM((1, D_A), F32)],
        compiler_params=_cparams(("arbitrary", "arbitrary")),
        name="inproj_prompt",
    )(x, mod3, w_in_bf, cos_t, sin_t, rg["conv_w"], rg["conv_b"], rg["w_gate"],
      rg["b_a"], rg["b_x"], rg["lam"], rg["g_norm_a"])
    return outs


def _attn_prompt_kernel(q_ref, k_ref, v_ref, o_ref, kwin_ref, vwin_ref, acc_s, m_s, l_s, bias_s):
    s = q_ref.shape[0]
    keep = kwin_ref.shape[0]
    kwin_ref[...] = k_ref[s - keep:s, :]
    vwin_ref[...] = v_ref[s - keep:s, :]

    lane = lax.broadcasted_iota(jnp.int32, (Q_BLK, HEAD_PAIR_W), 1)
    head0 = lane < HEAD_DIM
    nk = 2 * Q_BLK

    qi = lax.broadcasted_iota(jnp.int32, (Q_BLK, nk), 0)
    ki = lax.broadcasted_iota(jnp.int32, (Q_BLK, nk), 1)
    for slot in range(2):
        dist = slot * Q_BLK + qi - ki
        bias_s[slot] = jnp.where((dist >= 0) & (dist <= N_KEYS), 0.0, NEG_INF)

    def rows(start, n, d):
        return pl.ds(start, n) if d == 1 else pl.ds(start, n, stride=d)

    def unit(bi, d, nb, u):
        r = u // nb
        j = u % nb
        jk = jnp.maximum(j - 1, 0)
        start_q = r + d * Q_BLK * j
        start_k = r + d * Q_BLK * jk
        bias = bias_s[j - jk]
        qb = q_ref[rows(start_q, Q_BLK, d), :] * (HEAD_DIM ** -0.5)
        kb = k_ref[rows(start_k, nk, d), :].astype(BF16)
        vb = v_ref[rows(start_k, nk, d), :].astype(BF16)
        stats = []
        for sel in (head0, jnp.logical_not(head0)):
            qh = jnp.where(sel, qb, 0.0).astype(BF16)
            sc = lax.dot_general(qh, kb, (((1,), (1,)), ((), ())), preferred_element_type=F32)
            sc = sc + bias
            m = jnp.max(sc, axis=-1, keepdims=True)
            p = jnp.exp(sc - m)
            l = jnp.sum(p, axis=-1, keepdims=True)
            o = jnp.dot(p.astype(BF16), vb, preferred_element_type=F32)
            stats.append((m, l, o))
        (m0, l0, o0), (m1, l1, o1) = stats
        dst = rows(start_q, Q_BLK, d)
        acc_s[bi, dst, :] = jnp.where(head0, o0, o1)
        m_s[bi, dst, :] = jnp.where(head0, m0, m1)
        l_s[bi, dst, :] = jnp.where(head0, l0, l1)

    for bi, (_, d) in enumerate(DILATED_CFG):
        nb = s // d // Q_BLK

        def body(u, carry, bi=bi, d=d, nb=nb):
            unit(bi, d, nb, u)
            return carry

        lax.fori_loop(0, d * nb, body, 0, unroll=ATTN_UNROLL)

    chunk = 512

    def merge(i, carry):
        r0 = pl.multiple_of(i * chunk, chunk)
        sl = pl.ds(r0, chunk)
        ms = [m_s[bi, sl, :] for bi in range(len(DILATED_CFG))]
        mx = functools.reduce(jnp.maximum, ms)
        num = jnp.zeros((chunk, HEAD_PAIR_W), F32)
        den = jnp.zeros((chunk, HEAD_PAIR_W), F32)
        for bi in range(len(DILATED_CFG)):
            w = jnp.exp(ms[bi] - mx)
            num = num + w * acc_s[bi, sl, :]
            den = den + w * l_s[bi, sl, :]
        o_ref[sl, :] = num / den
        return carry

    lax.fori_loop(0, s // chunk, merge, 0)


def _attn_prompt(q, k, v):
    b, s, _ = q.shape
    keep = min(WIN_MAX, s)
    nbr = len(DILATED_CFG)
    spec = pl.BlockSpec((None, s, HEAD_PAIR_W), lambda i, j: (i, 0, j))
    wspec = pl.BlockSpec((None, keep, HEAD_PAIR_W), lambda i, j: (i, 0, j))
    return pl.pallas_call(
        _attn_prompt_kernel,
        grid=(b, N_HEAD_PAIRS),
        in_specs=[spec, spec, spec],
        out_specs=[spec, wspec, wspec],
        out_shape=[jax.ShapeDtypeStruct((b, s, D_B), F32),
                   jax.ShapeDtypeStruct((b, keep, D_B), F32),
                   jax.ShapeDtypeStruct((b, keep, D_B), F32)],
        scratch_shapes=[pltpu.VMEM((nbr, s, HEAD_PAIR_W), F32),
                        pltpu.VMEM((nbr, s, HEAD_PAIR_W), F32),
                        pltpu.VMEM((nbr, s, HEAD_PAIR_W), F32),
                        pltpu.VMEM((2, Q_BLK, 2 * Q_BLK), F32)],
        compiler_params=_cparams(("arbitrary", "arbitrary")),
        name="attn_prompt",
    )(q, k, v)


def _inproj_sample_kernel(x_ref, mod_ref, w_in_ref, cos_ref, sin_ref, conv_state_ref, h0_ref,
                          conv_w_ref, conv_b_ref, wg_ref, b_a_ref, b_x_ref, lam_ref, gna_ref,
                          ya_ref, q_ref, k_ref, v_ref, conv_out_ref, h_out_ref):
    nt, nb, _ = x_ref.shape
    sh1 = mod_ref[:, 0:D_MODEL]
    sc1 = mod_ref[:, D_MODEL:2 * D_MODEL]
    u = (x_ref[...] * (1.0 + sc1)[None] + sh1[None]).astype(BF16).reshape(nt * nb, D_MODEL)

    def proj(j):
        return jnp.dot(u, w_in_ref[:, j * D_A:(j + 1) * D_A], preferred_element_type=F32)

    cos = cos_ref[...].reshape(nt * nb, D_B)
    sin = sin_ref[...].reshape(nt * nb, D_B)
    q_ref[...] = _rope_apply(proj(2), cos, sin).reshape(nt, nb, D_B)
    k_ref[...] = _rope_apply(proj(3), cos, sin).reshape(nt, nb, D_B)
    v_ref[...] = proj(4).reshape(nt, nb, D_B)

    xa = proj(0).reshape(nt, nb, D_A)
    xp = [conv_state_ref[j] for j in range(CONV_W - 1)] + [xa[t] for t in range(nt)]
    xc = jnp.concatenate(
        [conv_b_ref[...] + sum(xp[t + j] * conv_w_ref[j:j + 1, :] for j in range(CONV_W))
         for t in range(nt)], axis=0)
    for j in range(CONV_W - 1):
        conv_out_ref[j] = xp[nt + j]

    a, u_in = _rglru_gates(xc, wg_ref, b_a_ref[...], b_x_ref[...], lam_ref[...])
    h = h0_ref[...]
    hs = []
    for t in range(nt):
        h = a[t * nb:(t + 1) * nb] * h + u_in[t * nb:(t + 1) * nb]
        hs.append(h)
    h_out_ref[...] = h
    y = jnp.concatenate(hs, axis=0) * jax.nn.gelu(proj(1))
    ya_ref[...] = _rms_norm(y, gna_ref[...]).astype(BF16).reshape(nt, nb, D_A)


def _inproj_sample(x_tb, mod_s, w_in_bf, cos_t, sin_t, conv_state_tb, h0, rg):
    nt, nb, _ = x_tb.shape
    full = lambda shape: pl.BlockSpec(shape, lambda i: (0,) * len(shape))
    return pl.pallas_call(
        _inproj_sample_kernel,
        grid=(1,),
        in_specs=[full((nt, nb, D_MODEL)), full((nb, 6 * D_MODEL)), full((D_MODEL, 5 * D_A)),
                  full((nt, nb, D_B)), full((nt, nb, D_B)),
                  full((CONV_W - 1, nb, D_A)), full((nb, D_A)),
                  full((CONV_W, D_A)), full((1, D_A)), full((2, D_A // 2, D_A)),
                  full((1, D_A)), full((1, D_A)), full((1, D_A)), full((1, D_A))],
        out_specs=[full((nt, nb, D_A)), full((nt, nb, D_B)), full((nt, nb, D_B)),
                   full((nt, nb, D_B)), full((CONV_W - 1, nb, D_A)), full((nb, D_A))],
        out_shape=[jax.ShapeDtypeStruct((nt, nb, D_A), BF16),
                   jax.ShapeDtypeStruct((nt, nb, D_B), F32),
                   jax.ShapeDtypeStruct((nt, nb, D_B), F32),
                   jax.ShapeDtypeStruct((nt, nb, D_B), F32),
                   jax.ShapeDtypeStruct((CONV_W - 1, nb, D_A), F32),
                   jax.ShapeDtypeStruct((nb, D_A), F32)],
        compiler_params=_cparams(("arbitrary",)),
        name="inproj_sample",
    )(x_tb, mod_s, w_in_bf, cos_t, sin_t, conv_state_tb, h0, rg["conv_w"], rg["conv_b"],
      rg["w_gate"], rg["b_a"], rg["b_x"], rg["lam"], rg["g_norm_a"])


def _attn_sample_kernel(q_ref, kn_ref, vn_ref, ck_ref, cv_ref, o_ref, kwin_ref, vwin_ref,
                        kn_pad, vn_pad):
    nt = q_ref.shape[0]
    n_buf = ck_ref.shape[-1]
    n_rows = N_HEADS_B * nt

    @pl.when(pl.program_id(0) == 0)
    def _():
        kn_pad[...] = jnp.zeros_like(kn_pad)
        vn_pad[...] = jnp.zeros_like(vn_pad)

    kn_pad[0:nt, :] = kn_ref[...]
    vn_pad[0:nt, :] = vn_ref[...]
    ck_t = ck_ref[...].reshape(D_B, n_buf)
    cv_t = cv_ref[...].reshape(D_B, n_buf)

    tail_lane = lax.broadcasted_iota(jnp.int32, (D_B, LANES), 1)

    def shift_in(old_t, new_pad, out_ref):
        rolled = pltpu.roll(old_t, n_buf - nt, 1)
        new_t = pltpu.roll(new_pad.T, LANES - nt, 1)
        last = jnp.where(tail_lane < LANES - nt, rolled[:, n_buf - LANES:n_buf], new_t)
        out_ref[:, :, 0:n_buf - LANES] = rolled[:, 0:n_buf - LANES].reshape(
            N_HEADS_B, HEAD_DIM, n_buf - LANES)
        out_ref[:, :, n_buf - LANES:n_buf] = last.reshape(N_HEADS_B, HEAD_DIM, LANES)

    shift_in(ck_t, kn_pad[...], kwin_ref)
    shift_in(cv_t, vn_pad[...], vwin_ref)

    ri = lax.broadcasted_iota(jnp.int32, (n_rows, nt), 0)
    ci = lax.broadcasted_iota(jnp.int32, (n_rows, nt), 1)
    pick = (ri % nt == ci).astype(BF16)
    qs = (q_ref[...] * (HEAD_DIM ** -0.5)).astype(BF16)
    q_rep = jnp.dot(pick, qs, preferred_element_type=F32)
    row_h = lax.broadcasted_iota(jnp.int32, (n_rows, D_B), 0) // nt
    lane_h = lax.broadcasted_iota(jnp.int32, (n_rows, D_B), 1) // HEAD_DIM
    own = row_h == lane_h
    qbd = jnp.where(own, q_rep, 0.0).astype(BF16)

    nt_dims = (((1,), (1,)), ((), ()))
    sc_c = jnp.dot(qbd, ck_t.astype(BF16), preferred_element_type=F32)
    sc_n = lax.dot_general(qbd, kn_pad[...].astype(BF16), nt_dims, preferred_element_type=F32)

    def mult(dist, limit_ok):
        c = jnp.zeros(dist.shape, F32)
        for win, d in DILATED_CFG:
            hit = (dist >= 0) & (dist <= win) & (dist % d == 0) & limit_ok
            c = c + hit.astype(F32)
        return c

    t_c = lax.broadcasted_iota(jnp.int32, sc_c.shape, 0) % nt
    dist_c = n_buf + t_c - lax.broadcasted_iota(jnp.int32, sc_c.shape, 1)
    mult_c = mult(dist_c, dist_c >= 0)
    t_n = lax.broadcasted_iota(jnp.int32, sc_n.shape, 0) % nt
    col_n = lax.broadcasted_iota(jnp.int32, sc_n.shape, 1)
    mult_n = mult(t_n - col_n, col_n < nt)

    sc_c = jnp.where(mult_c > 0, sc_c, NEG_INF)
    sc_n = jnp.where(mult_n > 0, sc_n, NEG_INF)
    m = jnp.maximum(jnp.max(sc_c, axis=-1, keepdims=True), jnp.max(sc_n, axis=-1, keepdims=True))
    p_c = mult_c * jnp.exp(sc_c - m)
    p_n = mult_n * jnp.exp(sc_n - m)
    l = jnp.sum(p_c, axis=-1, keepdims=True) + jnp.sum(p_n, axis=-1, keepdims=True)
    acc = (lax.dot_general(p_c.astype(BF16), cv_t.astype(BF16), nt_dims,
                           preferred_element_type=F32)
           + jnp.dot(p_n.astype(BF16), vn_pad[...].astype(BF16), preferred_element_type=F32))
    o_full = jnp.where(own, acc / l, 0.0)
    out = o_full[0:nt, :]
    for h in range(1, N_HEADS_B):
        out = out + o_full[h * nt:(h + 1) * nt, :]
    o_ref[...] = out


def _attn_sample(q, k_new, v_new, cache_k_t, cache_v_t):
    b, nt, _ = q.shape
    n_buf = cache_k_t.shape[-1]
    small = pl.BlockSpec((None, nt, D_B), lambda i: (i, 0, 0))
    big = pl.BlockSpec((None, N_HEADS_B, HEAD_DIM, n_buf), lambda i: (i, 0, 0, 0))
    win = jax.ShapeDtypeStruct((b, N_HEADS_B, HEAD_DIM, n_buf), F32)
    return pl.pallas_call(
        _attn_sample_kernel,
        grid=(b,),
        in_specs=[small, small, small, big, big],
        out_specs=[small, big, big],
        out_shape=[jax.ShapeDtypeStruct((b, nt, D_B), F32), win, win],
        scratch_shapes=[pltpu.VMEM((LANES, D_B), F32), pltpu.VMEM((LANES, D_B), F32)],
        compiler_params=_cparams(("arbitrary",)),
        name="attn_sample",
    )(q, k_new, v_new, cache_k_t, cache_v_t)


def _split_bf16(x):
    hi = x.astype(BF16)
    lo = (x - hi.astype(F32)).astype(BF16)
    return hi, lo


def _first_argmax(vals):
    mx = functools.reduce(jnp.maximum, vals)
    idx = jnp.full(mx.shape, float(len(vals) - 1), F32)
    for j in range(len(vals) - 2, -1, -1):
        idx = jnp.where(vals[j] == mx, float(j), idx)
    return mx, idx


def _outproj_kernel(ya_ref, yb_ref, x_ref, gt1_ref, sh2_ref, sc2_ref, w_out_ref, gnb_ref,
                    ln_g_ref, ln_b_ref, w_r_ref, b_r_ref, x1_ref, u2_ref, route_ref):
    yb = _rms_norm(yb_ref[...], gnb_ref[...]).astype(BF16)
    mixed = (jnp.dot(ya_ref[...], w_out_ref[0:D_A, :], preferred_element_type=F32)
             + jnp.dot(yb, w_out_ref[D_A:D_A + D_B, :], preferred_element_type=F32))
    x1 = _layer_norm(DN_ALPHA * x_ref[...] + gt1_ref[...] * mixed, ln_g_ref[...], ln_b_ref[...])
    x1_ref[...] = x1
    u2 = x1 * (1.0 + sc2_ref[...]) + sh2_ref[...]
    _store_token_tiles(u2_ref, u2)

    u_hi, u_lo = _split_bf16(u2)
    w_hi, w_lo = _split_bf16(w_r_ref[...])
    logits = (jnp.dot(u_hi, w_hi, preferred_element_type=F32)
              + jnp.dot(u_lo, w_hi, preferred_element_type=F32)
              + jnp.dot(u_hi, w_lo, preferred_element_type=F32)) + b_r_ref[...]
    lt = logits.T

    g_rows = [lt[j:j + 1, :] for j in range(N_GROUPS)]
    g_max, g_idx = _first_argmax(g_rows)
    p_group = 1.0 / sum(jnp.exp(g - g_max) for g in g_rows)
    e_rows = []
    for e in range(N_EXP_PER_GROUP):
        acc = jnp.zeros_like(g_max)
        for g in range(N_GROUPS):
            r = N_GROUPS + g * N_EXP_PER_GROUP + e
            acc = jnp.where(g_idx == float(g), lt[r:r + 1, :], acc)
        e_rows.append(acc)
    v1, i1 = _first_argmax(e_rows)
    rest = [jnp.where(i1 == float(e), -jnp.inf, e_rows[e]) for e in range(N_EXP_PER_GROUP)]
    v2, i2 = _first_argmax(rest)
    ex = jnp.exp(v2 - v1)
    w1 = p_group / (1.0 + ex)
    w2 = p_group * ex / (1.0 + ex)
    lo = jnp.minimum(i1, i2)
    hi = jnp.maximum(i1, i2)
    pair = jnp.where(lo == 0.0, hi - 1.0, jnp.where(lo == 1.0, hi + 1.0, 5.0))
    cls = g_idx * float(N_PAIRS) + pair
    w_of_lo = jnp.where(i1 < i2, w1, w2)
    w_of_hi = jnp.where(i1 < i2, w2, w1)
    e_lo = g_idx * float(N_EXP_PER_GROUP) + lo
    e_hi = g_idx * float(N_EXP_PER_GROUP) + hi
    route_ref[...] = jnp.concatenate(
        [cls, w_of_lo, w_of_hi, e_lo, e_hi, jnp.zeros((SUBLANES - 5, cls.shape[1]), F32)], axis=0)


def _outproj(ya, yb, x, mods, w_out_bf, gnb, ln_g, ln_b, w_r, b_r, tile):
    n = x.shape[0]
    nt = n // tile
    mod_arr, gt1_spec, sh2_spec, sc2_spec = mods
    row = lambda w: pl.BlockSpec((tile, w), lambda i: (i, 0))
    vec = lambda r, w: pl.BlockSpec((r, w), lambda i: (0, 0))
    return pl.pallas_call(
        _outproj_kernel,
        grid=(nt,),
        in_specs=[row(D_A), row(D_B), row(D_MODEL), gt1_spec, sh2_spec, sc2_spec,
                  vec(D_MODEL, D_MODEL), vec(1, D_B), vec(1, D_MODEL), vec(1, D_MODEL),
                  vec(D_MODEL, ROUTE_W), vec(1, ROUTE_W)],
        out_specs=[row(D_MODEL),
                   pl.BlockSpec((tile * TOKEN_ROWS, LANES), lambda i: (i, 0)),
                   pl.BlockSpec((None, SUBLANES, tile), lambda i: (i, 0, 0))],
        out_shape=[jax.ShapeDtypeStruct((n, D_MODEL), F32),
                   jax.ShapeDtypeStruct((n * TOKEN_ROWS, LANES), F32),
                   jax.ShapeDtypeStruct((nt, SUBLANES, tile), F32)],
        compiler_params=_cparams(("arbitrary",)),
        name="outproj_router",
    )(ya, yb, x, mod_arr, mod_arr, mod_arr, w_out_bf, gnb, ln_g, ln_b, w_r, b_r)


def _moe_kernel(e_lo_ref, e_hi_ref, n_used_ref, src_ref, dst_ref,
                x_hbm, w2_ref, wg_lo, wg_hi, wu_lo, wu_hi, wd_lo, wd_hi, o_hbm,
                xg, og, gsem, ssem, *, n_tokens, tile):
    i = pl.program_id(0)
    n_steps = pl.num_programs(0)
    n_used = n_used_ref[0]
    cur = i % N_BUF
    ahead = (i + 2) % N_BUF
    rows = tile * TOKEN_ROWS

    def gather_token(base, r, buf):
        row = pl.multiple_of(src_ref[base + r], TOKEN_ROWS)
        pltpu.make_async_copy(x_hbm.at[pl.ds(row, TOKEN_ROWS)],
                              xg.at[buf, pl.ds(r * TOKEN_ROWS, TOKEN_ROWS)], gsem.at[buf]).start()

    def gather_wait(buf):
        pltpu.make_async_copy(x_hbm.at[pl.ds(0, rows)], xg.at[buf], gsem.at[buf]).wait()

    def scatter_wait(buf):
        pltpu.make_async_copy(og.at[buf], o_hbm.at[pl.ds(0, rows)], ssem.at[buf]).wait()

    @pl.when(i == 0)
    def _():
        def first(r, c):
            gather_token(0, r, 0)
            gather_token(jnp.minimum(1, n_steps - 1) * tile, r, 1)
            return c
        lax.fori_loop(0, tile, first, 0, unroll=DMA_UNROLL)
        og[...] = jnp.zeros_like(og)
        for buf in range(N_BUF):
            pad_rows = pltpu.make_async_copy(
                og.at[buf], o_hbm.at[pl.ds((n_tokens + buf * tile) * TOKEN_ROWS, rows)],
                ssem.at[buf])
            pad_rows.start()
            pad_rows.wait()

    @pl.when(i <= n_used)
    def _():
        gather_wait(cur)

        @pl.when(i >= 2)
        def _():
            scatter_wait(cur)

        nxt = jnp.minimum(i + 2, n_steps - 1) * tile
        for r in range(tile):
            gather_token(nxt, r, ahead)
        prev = i * tile
        for r in range(tile):
            row = pl.multiple_of(dst_ref[prev + r], TOKEN_ROWS)
            pltpu.make_async_copy(og.at[ahead, pl.ds(r * TOKEN_ROWS, TOKEN_ROWS)],
                                  o_hbm.at[pl.ds(row, TOKEN_ROWS)], ssem.at[ahead]).start()

        xb = _load_token_tiles(xg.at[cur], tile).astype(BF16)
        w2 = w2_ref[...]
        out = jnp.zeros((tile, D_MODEL), F32)
        for col, wg, wu, wd in ((1, wg_lo, wu_lo, wd_lo), (2, wg_hi, wu_hi, wd_hi)):
            hg = jnp.dot(xb, wg[...], preferred_element_type=F32)
            hu = jnp.dot(xb, wu[...], preferred_element_type=F32)
            act = (hg * jax.nn.sigmoid(hg)) * hu * w2[:, col:col + 1]
            out = out + jnp.dot(act.astype(BF16), wd[...], preferred_element_type=F32)
        _store_token_tiles(og.at[cur], out)

        @pl.when(i == n_used)
        def _():
            for buf in range(N_BUF):
                @pl.when(buf != cur)
                def _():
                    gather_wait(buf)
                    scatter_wait(buf)


def _moe(u2t, route, w_gate_bf, w_up_bf, w_down_bf, tile):
    n = route.shape[0]
    n_steps = n // tile + N_CLASSES
    cls = route[:, 0].astype(jnp.int32)
    order = jnp.argsort(cls, stable=True).astype(jnp.int32)
    class_ids = jnp.arange(N_CLASSES, dtype=jnp.int32)
    counts = jnp.sum((cls[:, None] == class_ids[None, :]).astype(jnp.int32), axis=0)
    tiles_per = (counts + tile - 1) // tile
    tile_end = jnp.cumsum(tiles_per)
    tile_off = tile_end - tiles_per
    n_used = tile_end[-1]
    class_start = jnp.cumsum(counts) - counts
    step = jnp.arange(n_steps, dtype=jnp.int32)
    step_c = jnp.minimum(step, n_used - 1)
    cls_of = jnp.sum((step_c[:, None] >= tile_end[None, :]).astype(jnp.int32), axis=1)
    onehot = (cls_of[:, None] == class_ids[None, :]).astype(jnp.int32)
    pick = lambda table: jnp.sum(onehot * table[None, :], axis=1)
    local = step - pick(tile_off)
    nvalid = jnp.where(step < n_used, jnp.clip(pick(counts) - local * tile, 0, tile), 0)
    r = jnp.arange(tile, dtype=jnp.int32)
    pos = pick(class_start)[:, None] + local[:, None] * tile + r[None, :]
    valid = r[None, :] < nvalid[:, None]
    tok = order[jnp.clip(pos, 0, n - 1)]
    tok_src = jnp.where(valid, tok, 0)
    src = (tok_src * TOKEN_ROWS).astype(jnp.int32).reshape(-1)
    w2 = route[tok_src.reshape(-1)]
    spare = n + (step[:, None] % N_BUF) * tile + r[None, :]
    dst = jnp.where(valid, tok, spare)
    dst = jnp.concatenate([(n + (N_BUF - 1) * tile + r)[None, :], dst], axis=0)
    dst = (dst * TOKEN_ROWS).astype(jnp.int32).reshape(-1)
    grp = cls_of // N_PAIRS
    pair = cls_of % N_PAIRS
    pair_lo = (pair >= 3).astype(jnp.int32) + (pair >= 5).astype(jnp.int32)
    pair_hi = pair + 1 - 2 * (pair >= 3).astype(jnp.int32) - (pair >= 5).astype(jnp.int32)
    e_lo = (grp * N_EXP_PER_GROUP + pair_lo).astype(jnp.int32)
    e_hi = (grp * N_EXP_PER_GROUP + pair_hi).astype(jnp.int32)

    w_in_spec = lambda which: pl.BlockSpec(
        (None, D_MODEL, D_EXPERT), lambda i, elo, ehi, nu, s, d: ((elo, ehi)[which][i], 0, 0))
    w_dn_spec = lambda which: pl.BlockSpec(
        (None, D_EXPERT, D_MODEL), lambda i, elo, ehi, nu, s, d: ((elo, ehi)[which][i], 0, 0))
    grid_spec = pltpu.PrefetchScalarGridSpec(
        num_scalar_prefetch=5,
        grid=(n_steps,),
        in_specs=[pl.BlockSpec(memory_space=pl.ANY),
                  pl.BlockSpec((tile, SUBLANES), lambda i, elo, ehi, nu, s, d: (i, 0)),
                  w_in_spec(0), w_in_spec(1), w_in_spec(0), w_in_spec(1),
                  w_dn_spec(0), w_dn_spec(1)],
        out_specs=pl.BlockSpec(memory_space=pl.ANY),
        scratch_shapes=[pltpu.VMEM((N_BUF, tile * TOKEN_ROWS, LANES), F32),
                        pltpu.VMEM((N_BUF, tile * TOKEN_ROWS, LANES), F32),
                        pltpu.SemaphoreType.DMA((N_BUF,)),
                        pltpu.SemaphoreType.DMA((N_BUF,))],
    )
    return pl.pallas_call(
        functools.partial(_moe_kernel, n_tokens=n, tile=tile),
        grid_spec=grid_spec,
        out_shape=jax.ShapeDtypeStruct(((n + N_BUF * tile) * TOKEN_ROWS, LANES), F32),
        compiler_params=_cparams(("arbitrary",)),
        name="moe_sparse",
    )(e_lo, e_hi, n_used.reshape(1).astype(jnp.int32), src, dst,
      u2t, w2, w_gate_bf, w_gate_bf, w_up_bf, w_up_bf, w_down_bf, w_down_bf)


def _moe_dense_kernel(x_ref, route_ref, wg_ref, wu_ref, wd_ref, o_ref):
    e = pl.program_id(0)

    @pl.when(e == 0)
    def _():
        o_ref[...] = jnp.zeros_like(o_ref)

    xb = _load_token_tiles(x_ref, o_ref.shape[0]).astype(BF16)
    ef = e.astype(F32)
    route = route_ref[...]
    col = lambda c: route[:, c:c + 1]
    comb = jnp.where(col(3) == ef, col(1), 0.0) + jnp.where(col(4) == ef, col(2), 0.0)
    hg = jnp.dot(xb, wg_ref[...], preferred_element_type=F32)
    hu = jnp.dot(xb, wu_ref[...], preferred_element_type=F32)
    act = (hg * jax.nn.sigmoid(hg)) * hu * comb
    o_ref[...] += jnp.dot(act.astype(BF16), wd_ref[...], preferred_element_type=F32)


def _moe_dense(u2t, route, w_gate_bf, w_up_bf, w_down_bf):
    n = route.shape[0]
    return pl.pallas_call(
        _moe_dense_kernel,
        grid=(N_EXPERTS,),
        in_specs=[pl.BlockSpec((n * TOKEN_ROWS, LANES), lambda e: (0, 0)),
                  pl.BlockSpec((n, SUBLANES), lambda e: (0, 0)),
                  pl.BlockSpec((None, D_MODEL, D_EXPERT), lambda e: (e, 0, 0)),
                  pl.BlockSpec((None, D_MODEL, D_EXPERT), lambda e: (e, 0, 0)),
                  pl.BlockSpec((None, D_EXPERT, D_MODEL), lambda e: (e, 0, 0))],
        out_specs=pl.BlockSpec((n, D_MODEL), lambda e: (0, 0)),
        out_shape=jax.ShapeDtypeStruct((n, D_MODEL), F32),
        compiler_params=_cparams(("arbitrary",)),
        name="moe_dense",
    )(u2t, route, w_gate_bf, w_up_bf, w_down_bf)


def _final_kernel(x1_ref, ffn_ref, gt2_ref, g_ref, b_ref, o_ref, *, token_tiled):
    rows = x1_ref.shape[0]
    ffn = _load_token_tiles(ffn_ref, rows) if token_tiled else ffn_ref[...]
    o_ref[...] = _layer_norm(DN_ALPHA * x1_ref[...] + gt2_ref[...] * ffn, g_ref[...], b_ref[...])


def _final_norm(x1, ffn, mod_arr, gt2_spec, ln_g, ln_b, tile):
    n = x1.shape[0]
    token_tiled = ffn.shape[-1] == LANES
    row = pl.BlockSpec((tile, D_MODEL), lambda i: (i, 0))
    ffn_spec = pl.BlockSpec((tile * TOKEN_ROWS, LANES), lambda i: (i, 0)) if token_tiled else row
    vec = pl.BlockSpec((1, D_MODEL), lambda i: (0, 0))
    return pl.pallas_call(
        functools.partial(_final_kernel, token_tiled=token_tiled),
        grid=(n // tile,),
        in_specs=[row, ffn_spec, gt2_spec, vec, vec],
        out_specs=row,
        out_shape=jax.ShapeDtypeStruct((n, D_MODEL), F32),
        compiler_params=_cparams(("arbitrary",)),
        name="final_norm",
    )(x1, ffn, mod_arr, ln_g, ln_b)


def _rope_tables(pos):
    half = HEAD_DIM // 2
    inv = ROPE_THETA ** (-jnp.arange(half, dtype=F32) * 2.0 / HEAD_DIM)
    ang = pos.astype(F32)[:, None] * inv[None, :]
    cos = jnp.cos(ang)
    sin = jnp.sin(ang)
    cos_t = jnp.tile(jnp.concatenate([cos, cos], axis=-1), (1, N_HEADS_B))
    sin_t = jnp.tile(jnp.concatenate([-sin, sin], axis=-1), (1, N_HEADS_B))
    return cos_t, sin_t


def _block_diag(w):
    n, a, b = w.shape
    eye = jnp.eye(n, dtype=w.dtype)
    return (eye[:, None, :, None] * w[:, :, None, :]).reshape(n * a, n * b)


def _prepare_weights(w_in, conv_w, conv_b, w_rg_a, b_rg_a, w_rg_x, b_rg_x, rg_lambda, g_norm_a,
                     w_router_group, b_router_group, w_router_expert, b_router_expert):
    half_blocks = N_BLK_A // 2
    w_gate = jnp.stack([
        jnp.concatenate([_block_diag(w_rg_a[h * half_blocks:(h + 1) * half_blocks]),
                         _block_diag(w_rg_x[h * half_blocks:(h + 1) * half_blocks])], axis=1)
        for h in range(2)]).astype(BF16)
    rg = dict(conv_w=conv_w, conv_b=conv_b.reshape(1, D_A), w_gate=w_gate,
              b_a=b_rg_a.reshape(1, D_A), b_x=b_rg_x.reshape(1, D_A),
              lam=rg_lambda.reshape(1, D_A), g_norm_a=g_norm_a.reshape(1, D_A))
    n_logits = N_GROUPS + N_EXPERTS
    w_r = jnp.concatenate(
        [w_router_group,
         w_router_expert.transpose(1, 0, 2).reshape(D_MODEL, N_EXPERTS),
         jnp.zeros((D_MODEL, ROUTE_W - n_logits), F32)], axis=1)
    b_r = jnp.concatenate([b_router_group, b_router_expert.reshape(-1),
                           jnp.zeros((ROUTE_W - n_logits,), F32)]).reshape(1, ROUTE_W)
    return rg, w_r, b_r


def _channel_mixing(ya, yb, x, mods, gt2_spec, weights, tile, moe_tile):
    (w_out_bf, gnb, ln1_g, ln1_b, w_r, b_r, wg_bf, wu_bf, wd_bf, ln2_g, ln2_b) = weights
    x1, u2t, route = _outproj(ya, yb, x, mods, w_out_bf, gnb, ln1_g, ln1_b, w_r, b_r, tile)
    route = route.transpose(0, 2, 1).reshape(x.shape[0], SUBLANES)
    if moe_tile is None:
        ffn = _moe_dense(u2t, route, wg_bf, wu_bf, wd_bf)
    else:
        ffn = _moe(u2t, route, wg_bf, wu_bf, wd_bf, moe_tile)
    return _final_norm(x1, ffn, mods[0], gt2_spec, ln2_g, ln2_b, tile)


def kernel(x_prompt, x_sample, state_conv, state_rglru, cache_win_k, cache_win_v, c_prompt, c_sample, w_ada, b_ada, w_in, conv_w, conv_b, w_rg_a, b_rg_a, w_rg_x, b_rg_x, rg_lambda, g_norm_a, g_norm_b, w_out, ln1_g, ln1_b, w_router_group, b_router_group, w_router_expert, b_router_expert, w_exp_gate, w_exp_up, w_exp_down, ln2_g, ln2_b):
    bp, sp, _ = x_prompt.shape
    bs, ts, _ = x_sample.shape
    n_buf = cache_win_k.shape[1]

    rg, w_r, b_r = _prepare_weights(w_in, conv_w, conv_b, w_rg_a, b_rg_a, w_rg_x, b_rg_x,
                                    rg_lambda, g_norm_a, w_router_group, b_router_group,
                                    w_router_expert, b_router_expert)
    w_in_bf = w_in.astype(BF16)
    mix_weights = (w_out.astype(BF16), g_norm_b.reshape(1, D_B), ln1_g.reshape(1, D_MODEL),
                   ln1_b.reshape(1, D_MODEL), w_r, b_r, w_exp_gate.astype(BF16),
                   w_exp_up.astype(BF16), w_exp_down.astype(BF16),
                   ln2_g.reshape(1, D_MODEL), ln2_b.reshape(1, D_MODEL))

    mod = _modulation(jnp.concatenate([c_prompt, c_sample], axis=0), w_ada, b_ada)
    mod_p, mod_s = mod[:bp], mod[bp:]

    tile_p = 512
    cos_p, sin_p = _rope_tables(jnp.arange(sp))
    ya_p, q_p, k_p, v_p, conv_p, h_p = _inproj_prompt(
        x_prompt, mod_p.reshape(bp, 6, D_MODEL), w_in_bf, cos_p, sin_p, rg)
    yb_p, kwin_p, vwin_p = _attn_prompt(q_p, k_p, v_p)
    tiles_per_seq = sp // tile_p
    mod_p3 = mod_p.reshape(bp * 6, 1, D_MODEL)
    mod_spec_p = lambda j: pl.BlockSpec((None, 1, D_MODEL),
                                        lambda i: ((i // tiles_per_seq) * 6 + j, 0, 0))
    y_p = _channel_mixing(
        ya_p.reshape(bp * sp, D_A), yb_p.reshape(bp * sp, D_B), x_prompt.reshape(bp * sp, D_MODEL),
        (mod_p3, mod_spec_p(2), mod_spec_p(3), mod_spec_p(4)), mod_spec_p(5),
        mix_weights, tile_p, 256)

    cos_s, sin_s = _rope_tables(PAST_LEN + jnp.arange(ts))
    tb = lambda t: jnp.broadcast_to(t[:, None, :], (ts, bs, D_B))
    ya_s, q_s, k_s, v_s, conv_s, h_s = _inproj_sample(
        x_sample.transpose(1, 0, 2), mod_s, w_in_bf, tb(cos_s), tb(sin_s),
        state_conv.transpose(1, 0, 2), state_rglru, rg)
    bt = lambda t: t.transpose(1, 0, 2)
    yb_s, kwin_s, vwin_s = _attn_sample(
        bt(q_s), bt(k_s), bt(v_s), cache_win_k.transpose(0, 2, 3, 1),
        cache_win_v.transpose(0, 2, 3, 1))
    mod_spec_s = lambda j: pl.BlockSpec((bs, D_MODEL), lambda i: (0, j))
    y_s = _channel_mixing(
        ya_s.reshape(ts * bs, D_A), bt(yb_s).reshape(ts * bs, D_B),
        x_sample.transpose(1, 0, 2).reshape(ts * bs, D_MODEL),
        (mod_s, mod_spec_s(2), mod_spec_s(3), mod_spec_s(4)), mod_spec_s(5),
        mix_weights, bs, None)

    heads = lambda t: t.reshape(t.shape[0], t.shape[1], N_HEADS_B, HEAD_DIM)
    return (y_p.reshape(bp, sp, D_MODEL), bt(y_s.reshape(ts, bs, D_MODEL)),
            conv_p, h_p.reshape(bp, D_A), heads(kwin_p), heads(vwin_p),
            bt(conv_s), h_s, kwin_s.transpose(0, 3, 1, 2), vwin_s.transpose(0, 3, 1, 2))
```

```python
import functools
import math

import jax
import jax.numpy as jnp
from jax import lax
from jax.experimental import pallas as pl
from jax.experimental.pallas import tpu as pltpu

F32 = jnp.float32
BF16 = jnp.bfloat16

D_MODEL = 1024
D_A = 512
N_BLK_A = 8
BLK_W_A = D_A // N_BLK_A
CONV_W = 4
RG_C = 8.0
D_B = 512
HEAD_DIM = 64
N_HEADS_B = D_B // HEAD_DIM
DILATED_CFG = ((128, 1), (512, 4), (2048, 16))
WIN_MAX = 2048
N_KEYS = 128
ROPE_THETA = 10000.0
PAST_LEN = 8192
N_GROUPS = 4
N_EXP_PER_GROUP = 4
N_EXPERTS = N_GROUPS * N_EXP_PER_GROUP
D_EXPERT = 512
DN_ALPHA = 2.0 ** 0.25
LN_EPS = 1e-5
NEG_INF = -1e30

N_PAIRS = 6
N_CLASSES = N_GROUPS * N_PAIRS
LANES = 128
SUBLANES = 8
HEAD_PAIR_W = 2 * HEAD_DIM
N_HEAD_PAIRS = N_HEADS_B // 2
Q_BLK = 128
ATTN_UNROLL = 8
DMA_UNROLL = 8
INPROJ_PARTS = 1
ROUTE_W = LANES
TOKEN_ROWS = D_MODEL // LANES
TOKEN_PITCH = TOKEN_ROWS + 1
N_BUF = 3
VMEM_LIMIT = 56 * 1024 * 1024


def _cparams(sem):
    return pltpu.CompilerParams(dimension_semantics=sem, vmem_limit_bytes=VMEM_LIMIT)


def _mod_kernel(c_ref, w_ref, b_ref, o_ref):
    c = c_ref[...]
    s = (c * jax.nn.sigmoid(c)).astype(BF16)
    o_ref[...] = jnp.dot(s, w_ref[...].astype(BF16), preferred_element_type=F32) + b_ref[...]


def _modulation(c_all, w_ada, b_ada):
    n = c_all.shape[0]
    tn = 1024
    return pl.pallas_call(
        _mod_kernel,
        grid=(6 * D_MODEL // tn,),
        in_specs=[pl.BlockSpec((n, D_MODEL), lambda j: (0, 0)),
                  pl.BlockSpec((D_MODEL, tn), lambda j: (0, j)),
                  pl.BlockSpec((1, tn), lambda j: (0, j))],
        out_specs=pl.BlockSpec((n, tn), lambda j: (0, j)),
        out_shape=jax.ShapeDtypeStruct((n, 6 * D_MODEL), F32),
        compiler_params=_cparams(("arbitrary",)),
        name="adaln_mod",
    )(c_all, w_ada, b_ada.reshape(1, -1))


def _rope_apply(t, cos, sin_signed):
    lane = lax.broadcasted_iota(jnp.int32, t.shape, t.ndim - 1)
    first_half = (lane & (HEAD_DIM - 1)) < HEAD_DIM // 2
    width = t.shape[-1]
    swapped = jnp.where(first_half,
                        pltpu.roll(t, width - HEAD_DIM // 2, t.ndim - 1),
                        pltpu.roll(t, HEAD_DIM // 2, t.ndim - 1))
    return t * cos + swapped * sin_signed


def _rglru_gates(xc, wg_ref, b_a, b_x, lam):
    half = D_A // 2
    xcb = xc.astype(BF16)
    g0 = jnp.dot(xcb[:, :half], wg_ref[0], preferred_element_type=F32)
    g1 = jnp.dot(xcb[:, half:], wg_ref[1], preferred_element_type=F32)
    r = jax.nn.sigmoid(jnp.concatenate([g0[:, :half], g1[:, :half]], axis=1) + b_a)
    i = jax.nn.sigmoid(jnp.concatenate([g0[:, half:], g1[:, half:]], axis=1) + b_x)
    z = -lam
    softplus = jnp.maximum(z, 0.0) + jnp.log1p(jnp.exp(-jnp.abs(z)))
    log_a = -RG_C * r * softplus
    a = jnp.exp(log_a)
    one_minus_a2 = -jnp.tanh(log_a) * (a * a + 1.0)
    u = jnp.sqrt(one_minus_a2) * (i * xc)
    return a, u


def _store_token_tiles(ref, x, pitch=TOKEN_ROWS):
    n = x.shape[0]
    for c in range(TOKEN_ROWS):
        ref[pl.ds(c, n, stride=pitch), :] = x[:, c * LANES:(c + 1) * LANES]


def _load_token_tiles(ref, n, pitch=TOKEN_ROWS):
    return jnp.concatenate(
        [ref[pl.ds(c, n, stride=pitch), :] for c in range(TOKEN_ROWS)], axis=1)


def _rms_norm(y, g):
    return y * lax.rsqrt(jnp.mean(y * y, axis=-1, keepdims=True) + LN_EPS) * g


def _layer_norm(x, g, b):
    mu = jnp.mean(x, axis=-1, keepdims=True)
    xc = x - mu
    var = jnp.mean(xc * xc, axis=-1, keepdims=True)
    return xc * lax.rsqrt(var + LN_EPS) * g + b


def _inproj_prompt_kernel(x_ref, mod_ref, w_in_ref, cos_ref, sin_ref, conv_w_ref, conv_b_ref,
                          wg_ref, b_a_ref, b_x_ref, lam_ref, gna_ref,
                          ya_ref, q_ref, k_ref, v_ref, conv_out_ref, h_out_ref,
                          xp_buf, h_carry):
    t = pl.program_id(1)
    rows = x_ref.shape[0]
    pad = SUBLANES

    @pl.when(t == 0)
    def _():
        xp_buf[0:pad, :] = jnp.zeros((pad, D_A), F32)
        h_carry[...] = jnp.zeros_like(h_carry)

    row = lax.broadcasted_iota(jnp.int32, (SUBLANES, D_A), 0)

    def part(r0, n, h_prev):
        sl = slice(r0, r0 + n)
        u = (x_ref[sl, :] * (1.0 + mod_ref[1:2, :]) + mod_ref[0:1, :]).astype(BF16)

        def proj(j):
            return jnp.dot(u, w_in_ref[:, j * D_A:(j + 1) * D_A], preferred_element_type=F32)

        cos = cos_ref[sl, :]
        sin = sin_ref[sl, :]
        q_ref[sl, :] = _rope_apply(proj(2), cos, sin)
        k_ref[sl, :] = _rope_apply(proj(3), cos, sin)
        v_ref[sl, :] = proj(4)

        xa = proj(0)
        xp_buf[pad + r0:pad + r0 + n, :] = xa
        xc = conv_b_ref[...] + xa * conv_w_ref[CONV_W - 1:CONV_W, :]
        for j in range(CONV_W - 1):
            off = pad - (CONV_W - 1) + j + r0
            xc = xc + xp_buf[off:off + n, :] * conv_w_ref[j:j + 1, :]

        a, u_in = _rglru_gates(xc, wg_ref, b_a_ref[...], b_x_ref[...], lam_ref[...])

        hs = []
        for g in range(n // SUBLANES):
            ag = a[g * SUBLANES:(g + 1) * SUBLANES]
            ug = u_in[g * SUBLANES:(g + 1) * SUBLANES]
            for sh in (1, 2, 4):
                keep = row >= sh
                a_sh = pltpu.roll(ag, sh, 0)
                u_sh = pltpu.roll(ug, sh, 0)
                ug = jnp.where(keep, ag * u_sh + ug, ug)
                ag = jnp.where(keep, ag * a_sh, ag)
            hg = ag * h_prev + ug
            hs.append(hg)
            h_prev = hg[SUBLANES - 1:SUBLANES, :]

        y = jnp.concatenate(hs, axis=0) * jax.nn.gelu(proj(1))
        ya_ref[sl, :] = _rms_norm(y, gna_ref[...]).astype(BF16)
        return h_prev

    h_last = h_carry[...]
    n_part = rows // INPROJ_PARTS
    for p in range(INPROJ_PARTS):
        h_last = part(p * n_part, n_part, h_last)
    h_carry[...] = h_last
    h_out_ref[...] = h_last

    tail = xp_buf[rows + pad - (CONV_W - 1):rows + pad, :]
    conv_out_ref[...] = tail
    xp_buf[pad - (CONV_W - 1):pad, :] = tail


def _inproj_prompt(x, mod3, w_in_bf, cos_t, sin_t, rg):
    b, s, _ = x.shape
    tile = 512
    nt = s // tile
    row_spec = lambda w: pl.BlockSpec((None, tile, w), lambda i, j: (i, j, 0))
    vec = lambda r, w: pl.BlockSpec((r, w), lambda i, j: (0, 0))
    outs = pl.pallas_call(
        _inproj_prompt_kernel,
        grid=(b, nt),
        in_specs=[row_spec(D_MODEL),
                  pl.BlockSpec((None, 6, D_MODEL), lambda i, j: (i, 0, 0)),
                  vec(D_MODEL, 5 * D_A),
                  pl.BlockSpec((tile, D_B), lambda i, j: (j, 0)),
                  pl.BlockSpec((tile, D_B), lambda i, j: (j, 0)),
                  vec(CONV_W, D_A), vec(1, D_A),
                  pl.BlockSpec((2, D_A // 2, D_A), lambda i, j: (0, 0, 0)),
                  vec(1, D_A), vec(1, D_A), vec(1, D_A), vec(1, D_A)],
        out_specs=[row_spec(D_A), row_spec(D_B), row_spec(D_B), row_spec(D_B),
                   pl.BlockSpec((None, CONV_W - 1, D_A), lambda i, j: (i, 0, 0)),
                   pl.BlockSpec((None, 1, D_A), lambda i, j: (i, 0, 0))],
        out_shape=[jax.ShapeDtypeStruct((b, s, D_A), BF16),
                   jax.ShapeDtypeStruct((b, s, D_B), F32),
                   jax.ShapeDtypeStruct((b, s, D_B), F32),
                   jax.ShapeDtypeStruct((b, s, D_B), F32),
                   jax.ShapeDtypeStruct((b, CONV_W - 1, D_A), F32),
                   jax.ShapeDtypeStruct((b, 1, D_A), F32)],
        scratch_shapes=[pltpu.VMEM((tile + SUBLANES, D_A), F32),
                        pltpu.VMEM((1, D_A), F32)],
        compiler_params=_cparams(("arbitrary", "arbitrary")),
        name="inproj_prompt",
    )(x, mod3, w_in_bf, cos_t, sin_t, rg["conv_w"], rg["conv_b"], rg["w_gate"],
      rg["b_a"], rg["b_x"], rg["lam"], rg["g_norm_a"])
    return outs


def _attn_prompt_kernel(q_ref, k_ref, v_ref, o_ref, kwin_ref, vwin_ref, acc_s, m_s, l_s, bias_s):
    s = q_ref.shape[0]
    keep = kwin_ref.shape[0]
    kwin_ref[...] = k_ref[s - keep:s, :]
    vwin_ref[...] = v_ref[s - keep:s, :]

    lane = lax.broadcasted_iota(jnp.int32, (Q_BLK, HEAD_PAIR_W), 1)
    head0 = lane < HEAD_DIM
    nk = 2 * Q_BLK

    qi = lax.broadcasted_iota(jnp.int32, (Q_BLK, nk), 0)
    ki = lax.broadcasted_iota(jnp.int32, (Q_BLK, nk), 1)
    for slot in range(2):
        dist = slot * Q_BLK + qi - ki
        bias_s[slot] = jnp.where((dist >= 0) & (dist <= N_KEYS), 0.0, NEG_INF)

    def rows(start, n, d):
        return pl.ds(start, n) if d == 1 else pl.ds(start, n, stride=d)

    def unit(bi, d, nb, u):
        r = u // nb
        j = u % nb
        jk = jnp.maximum(j - 1, 0)
        start_q = r + d * Q_BLK * j
        start_k = r + d * Q_BLK * jk
        bias = bias_s[j - jk]
        qb = q_ref[rows(start_q, Q_BLK, d), :] * (HEAD_DIM ** -0.5)
        kb = k_ref[rows(start_k, nk, d), :].astype(BF16)
        vb = v_ref[rows(start_k, nk, d), :].astype(BF16)
        q2 = jnp.concatenate([jnp.where(head0, qb, 0.0), jnp.where(head0, 0.0, qb)],
                             axis=0).astype(BF16)
        sc = lax.dot_general(q2, kb, (((1,), (1,)), ((), ())), preferred_element_type=F32)
        sc = sc + jnp.concatenate([bias, bias], axis=0)
        m = jnp.max(sc, axis=-1, keepdims=True)
        p = jnp.exp(sc - m)
        l = jnp.sum(p, axis=-1, keepdims=True)
        o = jnp.dot(p.astype(BF16), vb, preferred_element_type=F32)
        dst = rows(start_q, Q_BLK, d)
        acc_s[bi, dst, :] = jnp.where(head0, o[0:Q_BLK], o[Q_BLK:])
        m_s[bi, dst, :] = jnp.where(head0, m[0:Q_BLK], m[Q_BLK:])
        l_s[bi, dst, :] = jnp.where(head0, l[0:Q_BLK], l[Q_BLK:])

    for bi, (_, d) in enumerate(DILATED_CFG):
        nb = s // d // Q_BLK

        def body(u, carry, bi=bi, d=d, nb=nb):
            unit(bi, d, nb, u)
            return carry

        lax.fori_loop(0, d * nb, body, 0, unroll=ATTN_UNROLL)

    chunk = 512

    def merge(i, carry):
        r0 = pl.multiple_of(i * chunk, chunk)
        sl = pl.ds(r0, chunk)
        ms = [m_s[bi, sl, :] for bi in range(len(DILATED_CFG))]
        mx = functools.reduce(jnp.maximum, ms)
        num = jnp.zeros((chunk, HEAD_PAIR_W), F32)
        den = jnp.zeros((chunk, HEAD_PAIR_W), F32)
        for bi in range(len(DILATED_CFG)):
            w = jnp.exp(ms[bi] - mx)
            num = num + w * acc_s[bi, sl, :]
            den = den + w * l_s[bi, sl, :]
        o_ref[sl, :] = num / den
        return carry

    lax.fori_loop(0, s // chunk, merge, 0)


def _attn_prompt(q, k, v):
    b, s, _ = q.shape
    keep = min(WIN_MAX, s)
    nbr = len(DILATED_CFG)
    spec = pl.BlockSpec((None, s, HEAD_PAIR_W), lambda i, j: (i, 0, j))
    wspec = pl.BlockSpec((None, keep, HEAD_PAIR_W), lambda i, j: (i, 0, j))
    return pl.pallas_call(
        _attn_prompt_kernel,
        grid=(b, N_HEAD_PAIRS),
        in_specs=[spec, spec, spec],
        out_specs=[spec, wspec, wspec],
        out_shape=[jax.ShapeDtypeStruct((b, s, D_B), F32),
                   jax.ShapeDtypeStruct((b, keep, D_B), F32),
                   jax.ShapeDtypeStruct((b, keep, D_B), F32)],
        scratch_shapes=[pltpu.VMEM((nbr, s, HEAD_PAIR_W), F32),
                        pltpu.VMEM((nbr, s, HEAD_PAIR_W), F32),
                        pltpu.VMEM((nbr, s, HEAD_PAIR_W), F32),
                        pltpu.VMEM((2, Q_BLK, 2 * Q_BLK), F32)],
        compiler_params=_cparams(("arbitrary", "arbitrary")),
        name="attn_prompt",
    )(q, k, v)


def _inproj_sample_kernel(x_ref, mod_ref, w_in_ref, cos_ref, sin_ref, conv_state_ref, h0_ref,
                          conv_w_ref, conv_b_ref, wg_ref, b_a_ref, b_x_ref, lam_ref, gna_ref,
                          ya_ref, q_ref, k_ref, v_ref, conv_out_ref, h_out_ref):
    nt, nb, _ = x_ref.shape
    sh1 = mod_ref[:, 0:D_MODEL]
    sc1 = mod_ref[:, D_MODEL:2 * D_MODEL]
    u = (x_ref[...] * (1.0 + sc1)[None] + sh1[None]).astype(BF16).reshape(nt * nb, D_MODEL)

    def proj(j):
        return jnp.dot(u, w_in_ref[:, j * D_A:(j + 1) * D_A], preferred_element_type=F32)

    cos = cos_ref[...].reshape(nt * nb, D_B)
    sin = sin_ref[...].reshape(nt * nb, D_B)
    q_ref[...] = _rope_apply(proj(2), cos, sin).reshape(nt, nb, D_B)
    k_ref[...] = _rope_apply(proj(3), cos, sin).reshape(nt, nb, D_B)
    v_ref[...] = proj(4).reshape(nt, nb, D_B)

    xa = proj(0).reshape(nt, nb, D_A)
    xp = [conv_state_ref[j] for j in range(CONV_W - 1)] + [xa[t] for t in range(nt)]
    xc = jnp.concatenate(
        [conv_b_ref[...] + sum(xp[t + j] * conv_w_ref[j:j + 1, :] for j in range(CONV_W))
         for t in range(nt)], axis=0)
    for j in range(CONV_W - 1):
        conv_out_ref[j] = xp[nt + j]

    a, u_in = _rglru_gates(xc, wg_ref, b_a_ref[...], b_x_ref[...], lam_ref[...])
    h = h0_ref[...]
    hs = []
    for t in range(nt):
        h = a[t * nb:(t + 1) * nb] * h + u_in[t * nb:(t + 1) * nb]
        hs.append(h)
    h_out_ref[...] = h
    y = jnp.concatenate(hs, axis=0) * jax.nn.gelu(proj(1))
    ya_ref[...] = _rms_norm(y, gna_ref[...]).astype(BF16).reshape(nt, nb, D_A)


def _inproj_sample(x_tb, mod_s, w_in_bf, cos_t, sin_t, conv_state_tb, h0, rg):
    nt, nb, _ = x_tb.shape
    full = lambda shape: pl.BlockSpec(shape, lambda i: (0,) * len(shape))
    return pl.pallas_call(
        _inproj_sample_kernel,
        grid=(1,),
        in_specs=[full((nt, nb, D_MODEL)), full((nb, 6 * D_MODEL)), full((D_MODEL, 5 * D_A)),
                  full((nt, nb, D_B)), full((nt, nb, D_B)),
                  full((CONV_W - 1, nb, D_A)), full((nb, D_A)),
                  full((CONV_W, D_A)), full((1, D_A)), full((2, D_A // 2, D_A)),
                  full((1, D_A)), full((1, D_A)), full((1, D_A)), full((1, D_A))],
        out_specs=[full((nt, nb, D_A)), full((nt, nb, D_B)), full((nt, nb, D_B)),
                   full((nt, nb, D_B)), full((CONV_W - 1, nb, D_A)), full((nb, D_A))],
        out_shape=[jax.ShapeDtypeStruct((nt, nb, D_A), BF16),
                   jax.ShapeDtypeStruct((nt, nb, D_B), F32),
                   jax.ShapeDtypeStruct((nt, nb, D_B), F32),
                   jax.ShapeDtypeStruct((nt, nb, D_B), F32),
                   jax.ShapeDtypeStruct((CONV_W - 1, nb, D_A), F32),
                   jax.ShapeDtypeStruct((nb, D_A), F32)],
        compiler_params=_cparams(("arbitrary",)),
        name="inproj_sample",
    )(x_tb, mod_s, w_in_bf, cos_t, sin_t, conv_state_tb, h0, rg["conv_w"], rg["conv_b"],
      rg["w_gate"], rg["b_a"], rg["b_x"], rg["lam"], rg["g_norm_a"])


def _attn_sample_kernel(q_ref, kn_ref, vn_ref, ck_ref, cv_ref, o_ref, kwin_ref, vwin_ref,
                        kn_pad, vn_pad):
    nt = q_ref.shape[0]
    n_buf = ck_ref.shape[-1]
    n_rows = N_HEADS_B * nt

    @pl.when(pl.program_id(0) == 0)
    def _():
        kn_pad[...] = jnp.zeros_like(kn_pad)
        vn_pad[...] = jnp.zeros_like(vn_pad)

    kn_pad[0:nt, :] = kn_ref[...]
    vn_pad[0:nt, :] = vn_ref[...]
    ck_t = ck_ref[...].reshape(D_B, n_buf)
    cv_t = cv_ref[...].reshape(D_B, n_buf)

    tail_lane = lax.broadcasted_iota(jnp.int32, (D_B, LANES), 1)

    def shift_in(old_t, new_pad, out_ref):
        rolled = pltpu.roll(old_t, n_buf - nt, 1)
        new_t = pltpu.roll(new_pad.T, LANES - nt, 1)
        last = jnp.where(tail_lane < LANES - nt, rolled[:, n_buf - LANES:n_buf], new_t)
        out_ref[:, :, 0:n_buf - LANES] = rolled[:, 0:n_buf - LANES].reshape(
            N_HEADS_B, HEAD_DIM, n_buf - LANES)
        out_ref[:, :, n_buf - LANES:n_buf] = last.reshape(N_HEADS_B, HEAD_DIM, LANES)

    shift_in(ck_t, kn_pad[...], kwin_ref)
    shift_in(cv_t, vn_pad[...], vwin_ref)

    ri = lax.broadcasted_iota(jnp.int32, (n_rows, nt), 0)
    ci = lax.broadcasted_iota(jnp.int32, (n_rows, nt), 1)
    pick = (ri % nt == ci).astype(BF16)
    qs = (q_ref[...] * (HEAD_DIM ** -0.5)).astype(BF16)
    q_rep = jnp.dot(pick, qs, preferred_element_type=F32)
    row_h = lax.broadcasted_iota(jnp.int32, (n_rows, D_B), 0) // nt
    lane_h = lax.broadcasted_iota(jnp.int32, (n_rows, D_B), 1) // HEAD_DIM
    own = row_h == lane_h
    qbd = jnp.where(own, q_rep, 0.0).astype(BF16)

    nt_dims = (((1,), (1,)), ((), ()))
    sc_c = jnp.dot(qbd, ck_t.astype(BF16), preferred_element_type=F32)
    sc_n = lax.dot_general(qbd, kn_pad[...].astype(BF16), nt_dims, preferred_element_type=F32)

    def mult(dist, limit_ok):
        c = jnp.zeros(dist.shape, F32)
        for win, d in DILATED_CFG:
            hit = (dist >= 0) & (dist <= win) & (dist % d == 0) & limit_ok
            c = c + hit.astype(F32)
        return c

    t_c = lax.broadcasted_iota(jnp.int32, sc_c.shape, 0) % nt
    dist_c = n_buf + t_c - lax.broadcasted_iota(jnp.int32, sc_c.shape, 1)
    mult_c = mult(dist_c, dist_c >= 0)
    t_n = lax.broadcasted_iota(jnp.int32, sc_n.shape, 0) % nt
    col_n = lax.broadcasted_iota(jnp.int32, sc_n.shape, 1)
    mult_n = mult(t_n - col_n, col_n < nt)

    sc_c = jnp.where(mult_c > 0, sc_c, NEG_INF)
    sc_n = jnp.where(mult_n > 0, sc_n, NEG_INF)
    m = jnp.maximum(jnp.max(sc_c, axis=-1, keepdims=True), jnp.max(sc_n, axis=-1, keepdims=True))
    p_c = mult_c * jnp.exp(sc_c - m)
    p_n = mult_n * jnp.exp(sc_n - m)
    l = jnp.sum(p_c, axis=-1, keepdims=True) + jnp.sum(p_n, axis=-1, keepdims=True)
    acc = (lax.dot_general(p_c.astype(BF16), cv_t.astype(BF16), nt_dims,
                           preferred_element_type=F32)
           + jnp.dot(p_n.astype(BF16), vn_pad[...].astype(BF16), preferred_element_type=F32))
    o_full = jnp.where(own, acc / l, 0.0)
    out = o_full[0:nt, :]
    for h in range(1, N_HEADS_B):
        out = out + o_full[h * nt:(h + 1) * nt, :]
    o_ref[...] = out


def _attn_sample(q, k_new, v_new, cache_k_t, cache_v_t):
    b, nt, _ = q.shape
    n_buf = cache_k_t.shape[-1]
    small = pl.BlockSpec((None, nt, D_B), lambda i: (i, 0, 0))
    big = pl.BlockSpec((None, N_HEADS_B, HEAD_DIM, n_buf), lambda i: (i, 0, 0, 0))
    win = jax.ShapeDtypeStruct((b, N_HEADS_B, HEAD_DIM, n_buf), F32)
    return pl.pallas_call(
        _attn_sample_kernel,
        grid=(b,),
        in_specs=[small, small, small, big, big],
        out_specs=[small, big, big],
        out_shape=[jax.ShapeDtypeStruct((b, nt, D_B), F32), win, win],
        scratch_shapes=[pltpu.VMEM((LANES, D_B), F32), pltpu.VMEM((LANES, D_B), F32)],
        compiler_params=_cparams(("arbitrary",)),
        name="attn_sample",
    )(q, k_new, v_new, cache_k_t, cache_v_t)


def _split_bf16(x):
    hi = x.astype(BF16)
    lo = (x - hi.astype(F32)).astype(BF16)
    return hi, lo


def _first_argmax(vals):
    mx = functools.reduce(jnp.maximum, vals)
    idx = jnp.full(mx.shape, float(len(vals) - 1), F32)
    for j in range(len(vals) - 2, -1, -1):
        idx = jnp.where(vals[j] == mx, float(j), idx)
    return mx, idx


def _outproj_kernel(ya_ref, yb_ref, x_ref, gt1_ref, sh2_ref, sc2_ref, w_out_ref, gnb_ref,
                    ln_g_ref, ln_b_ref, w_r_ref, b_r_ref, x1_ref, u2_ref, route_ref):
    yb = _rms_norm(yb_ref[...], gnb_ref[...]).astype(BF16)
    mixed = (jnp.dot(ya_ref[...], w_out_ref[0:D_A, :], preferred_element_type=F32)
             + jnp.dot(yb, w_out_ref[D_A:D_A + D_B, :], preferred_element_type=F32))
    x1 = _layer_norm(DN_ALPHA * x_ref[...] + gt1_ref[...] * mixed, ln_g_ref[...], ln_b_ref[...])
    x1_ref[...] = x1
    u2 = x1 * (1.0 + sc2_ref[...]) + sh2_ref[...]
    _store_token_tiles(u2_ref, u2, TOKEN_PITCH)

    u_hi, u_lo = _split_bf16(u2)
    w_hi, w_lo = _split_bf16(w_r_ref[...])
    logits = (jnp.dot(u_hi, w_hi, preferred_element_type=F32)
              + jnp.dot(u_lo, w_hi, preferred_element_type=F32)
              + jnp.dot(u_hi, w_lo, preferred_element_type=F32)) + b_r_ref[...]
    lt = logits.T

    g_rows = [lt[j:j + 1, :] for j in range(N_GROUPS)]
    g_max, g_idx = _first_argmax(g_rows)
    p_group = 1.0 / sum(jnp.exp(g - g_max) for g in g_rows)
    e_rows = []
    for e in range(N_EXP_PER_GROUP):
        acc = jnp.zeros_like(g_max)
        for g in range(N_GROUPS):
            r = N_GROUPS + g * N_EXP_PER_GROUP + e
            acc = jnp.where(g_idx == float(g), lt[r:r + 1, :], acc)
        e_rows.append(acc)
    v1, i1 = _first_argmax(e_rows)
    rest = [jnp.where(i1 == float(e), -jnp.inf, e_rows[e]) for e in range(N_EXP_PER_GROUP)]
    v2, i2 = _first_argmax(rest)
    ex = jnp.exp(v2 - v1)
    w1 = p_group / (1.0 + ex)
    w2 = p_group * ex / (1.0 + ex)
    lo = jnp.minimum(i1, i2)
    hi = jnp.maximum(i1, i2)
    pair = jnp.where(lo == 0.0, hi - 1.0, jnp.where(lo == 1.0, hi + 1.0, 5.0))
    cls = g_idx * float(N_PAIRS) + pair
    w_of_lo = jnp.where(i1 < i2, w1, w2)
    w_of_hi = jnp.where(i1 < i2, w2, w1)
    e_lo = g_idx * float(N_EXP_PER_GROUP) + lo
    e_hi = g_idx * float(N_EXP_PER_GROUP) + hi
    n_tok = cls.shape[1]
    route = jnp.concatenate(
        [cls, w_of_lo, w_of_hi, e_lo, e_hi, jnp.zeros((ROUTE_W - 5, n_tok), F32)], axis=0)
    route_ref[...] = route[0:SUBLANES, :]
    u2_ref[pl.ds(TOKEN_ROWS, n_tok, stride=TOKEN_PITCH), :] = route.T


def _outproj(ya, yb, x, mods, w_out_bf, gnb, ln_g, ln_b, w_r, b_r, tile):
    n = x.shape[0]
    nt = n // tile
    mod_arr, gt1_spec, sh2_spec, sc2_spec = mods
    row = lambda w: pl.BlockSpec((tile, w), lambda i: (i, 0))
    vec = lambda r, w: pl.BlockSpec((r, w), lambda i: (0, 0))
    return pl.pallas_call(
        _outproj_kernel,
        grid=(nt,),
        in_specs=[row(D_A), row(D_B), row(D_MODEL), gt1_spec, sh2_spec, sc2_spec,
                  vec(D_MODEL, D_MODEL), vec(1, D_B), vec(1, D_MODEL), vec(1, D_MODEL),
                  vec(D_MODEL, ROUTE_W), vec(1, ROUTE_W)],
        out_specs=[row(D_MODEL),
                   pl.BlockSpec((tile * TOKEN_PITCH, LANES), lambda i: (i, 0)),
                   pl.BlockSpec((None, SUBLANES, tile), lambda i: (i, 0, 0))],
        out_shape=[jax.ShapeDtypeStruct((n, D_MODEL), F32),
                   jax.ShapeDtypeStruct((n * TOKEN_PITCH, LANES), F32),
                   jax.ShapeDtypeStruct((nt, SUBLANES, tile), F32)],
        compiler_params=_cparams(("arbitrary",)),
        name="outproj_router",
    )(ya, yb, x, mod_arr, mod_arr, mod_arr, w_out_bf, gnb, ln_g, ln_b, w_r, b_r)


def _moe_kernel(e_lo_ref, e_hi_ref, n_used_ref, src_ref, dst_ref,
                x_hbm, wg_lo, wg_hi, wu_lo, wu_hi, wd_lo, wd_hi, o_hbm,
                xg, og, gsem, ssem, *, n_tokens, tile):
    i = pl.program_id(0)
    n_steps = pl.num_programs(0)
    n_used = n_used_ref[0]
    cur = i % N_BUF
    ahead = (i + 2) % N_BUF
    rows = tile * TOKEN_ROWS

    def gather_token(base, r, buf):
        pltpu.make_async_copy(x_hbm.at[pl.ds(src_ref[base + r], TOKEN_PITCH)],
                              xg.at[buf, pl.ds(r * TOKEN_PITCH, TOKEN_PITCH)],
                              gsem.at[buf]).start()

    def gather_wait(buf):
        pltpu.make_async_copy(x_hbm.at[pl.ds(0, tile * TOKEN_PITCH)], xg.at[buf],
                              gsem.at[buf]).wait()

    def scatter_wait(buf):
        pltpu.make_async_copy(og.at[buf], o_hbm.at[pl.ds(0, rows)], ssem.at[buf]).wait()

    @pl.when(i == 0)
    def _():
        def first(r, c):
            gather_token(0, r, 0)
            gather_token(jnp.minimum(1, n_steps - 1) * tile, r, 1)
            return c
        lax.fori_loop(0, tile, first, 0, unroll=DMA_UNROLL)
        og[...] = jnp.zeros_like(og)
        for buf in range(N_BUF):
            pad_rows = pltpu.make_async_copy(
                og.at[buf], o_hbm.at[pl.ds((n_tokens + buf * tile) * TOKEN_ROWS, rows)],
                ssem.at[buf])
            pad_rows.start()
            pad_rows.wait()

    @pl.when(i <= n_used)
    def _():
        gather_wait(cur)

        @pl.when(i >= 2)
        def _():
            scatter_wait(cur)

        nxt = jnp.minimum(i + 2, n_steps - 1) * tile
        for r in range(tile):
            gather_token(nxt, r, ahead)
        prev = i * tile
        for r in range(tile):
            row = pl.multiple_of(dst_ref[prev + r], TOKEN_ROWS)
            pltpu.make_async_copy(og.at[ahead, pl.ds(r * TOKEN_ROWS, TOKEN_ROWS)],
                                  o_hbm.at[pl.ds(row, TOKEN_ROWS)], ssem.at[ahead]).start()

        xb = _load_token_tiles(xg.at[cur], tile, TOKEN_PITCH).astype(BF16)
        w2 = xg[cur, pl.ds(TOKEN_ROWS, tile, stride=TOKEN_PITCH), :]
        out = jnp.zeros((tile, D_MODEL), F32)
        for col, wg, wu, wd in ((1, wg_lo, wu_lo, wd_lo), (2, wg_hi, wu_hi, wd_hi)):
            hg = jnp.dot(xb, wg[...], preferred_element_type=F32)
            hu = jnp.dot(xb, wu[...], preferred_element_type=F32)
            act = (hg * jax.nn.sigmoid(hg)) * hu * w2[:, col:col + 1]
            out = out + jnp.dot(act.astype(BF16), wd[...], preferred_element_type=F32)
        _store_token_tiles(og.at[cur], out)

        @pl.when(i == n_used)
        def _():
            for buf in range(N_BUF):
                @pl.when(buf != cur)
                def _():
                    gather_wait(buf)
                    scatter_wait(buf)


def _moe(u2t, cls, w_gate_bf, w_up_bf, w_down_bf, tile):
    n = cls.shape[0]
    n_steps = n // tile + N_CLASSES
    cls = cls.astype(jnp.int32)
    order = jnp.argsort(cls, stable=True).astype(jnp.int32)
    class_ids = jnp.arange(N_CLASSES, dtype=jnp.int32)
    counts = jnp.sum((cls[:, None] == class_ids[None, :]).astype(jnp.int32), axis=0)
    tiles_per = (counts + tile - 1) // tile
    tile_end = jnp.cumsum(tiles_per)
    tile_off = tile_end - tiles_per
    n_used = tile_end[-1]
    class_start = jnp.cumsum(counts) - counts
    step = jnp.arange(n_steps, dtype=jnp.int32)
    step_c = jnp.minimum(step, n_used - 1)
    cls_of = jnp.sum((step_c[:, None] >= tile_end[None, :]).astype(jnp.int32), axis=1)
    onehot = (cls_of[:, None] == class_ids[None, :]).astype(jnp.int32)
    pick = lambda table: jnp.sum(onehot * table[None, :], axis=1)
    local = step - pick(tile_off)
    nvalid = jnp.where(step < n_used, jnp.clip(pick(counts) - local * tile, 0, tile), 0)
    r = jnp.arange(tile, dtype=jnp.int32)
    pos = pick(class_start)[:, None] + local[:, None] * tile + r[None, :]
    valid = r[None, :] < nvalid[:, None]
    tok = order[jnp.clip(pos, 0, n - 1)]
    src = (jnp.where(valid, tok, 0) * TOKEN_PITCH).astype(jnp.int32).reshape(-1)
    spare = n + (step[:, None] % N_BUF) * tile + r[None, :]
    dst = jnp.where(valid, tok, spare)
    dst = jnp.concatenate([(n + (N_BUF - 1) * tile + r)[None, :], dst], axis=0)
    dst = (dst * TOKEN_ROWS).astype(jnp.int32).reshape(-1)
    grp = cls_of // N_PAIRS
    pair = cls_of % N_PAIRS
    pair_lo = (pair >= 3).astype(jnp.int32) + (pair >= 5).astype(jnp.int32)
    pair_hi = pair + 1 - 2 * (pair >= 3).astype(jnp.int32) - (pair >= 5).astype(jnp.int32)
    e_lo = (grp * N_EXP_PER_GROUP + pair_lo).astype(jnp.int32)
    e_hi = (grp * N_EXP_PER_GROUP + pair_hi).astype(jnp.int32)

    w_in_spec = lambda which: pl.BlockSpec(
        (None, D_MODEL, D_EXPERT), lambda i, elo, ehi, nu, s, d: ((elo, ehi)[which][i], 0, 0))
    w_dn_spec = lambda which: pl.BlockSpec(
        (None, D_EXPERT, D_MODEL), lambda i, elo, ehi, nu, s, d: ((elo, ehi)[which][i], 0, 0))
    grid_spec = pltpu.PrefetchScalarGridSpec(
        num_scalar_prefetch=5,
        grid=(n_steps,),
        in_specs=[pl.BlockSpec(memory_space=pl.ANY),
                  w_in_spec(0), w_in_spec(1), w_in_spec(0), w_in_spec(1),
                  w_dn_spec(0), w_dn_spec(1)],
        out_specs=pl.BlockSpec(memory_space=pl.ANY),
        scratch_shapes=[pltpu.VMEM((N_BUF, tile * TOKEN_PITCH, LANES), F32),
                        pltpu.VMEM((N_BUF, tile * TOKEN_ROWS, LANES), F32),
                        pltpu.SemaphoreType.DMA((N_BUF,)),
                        pltpu.SemaphoreType.DMA((N_BUF,))],
    )
    return pl.pallas_call(
        functools.partial(_moe_kernel, n_tokens=n, tile=tile),
        grid_spec=grid_spec,
        out_shape=jax.ShapeDtypeStruct(((n + N_BUF * tile) * TOKEN_ROWS, LANES), F32),
        compiler_params=_cparams(("arbitrary",)),
        name="moe_sparse",
    )(e_lo, e_hi, n_used.reshape(1).astype(jnp.int32), src, dst,
      u2t, w_gate_bf, w_gate_bf, w_up_bf, w_up_bf, w_down_bf, w_down_bf)


def _moe_dense_kernel(x_ref, wg_ref, wu_ref, wd_ref, o_ref):
    e = pl.program_id(0)
    n = o_ref.shape[0]

    @pl.when(e == 0)
    def _():
        o_ref[...] = jnp.zeros_like(o_ref)

    xb = _load_token_tiles(x_ref, n, TOKEN_PITCH).astype(BF16)
    ef = e.astype(F32)
    route = x_ref[pl.ds(TOKEN_ROWS, n, stride=TOKEN_PITCH), :]
    col = lambda c: route[:, c:c + 1]
    comb = jnp.where(col(3) == ef, col(1), 0.0) + jnp.where(col(4) == ef, col(2), 0.0)
    hg = jnp.dot(xb, wg_ref[...], preferred_element_type=F32)
    hu = jnp.dot(xb, wu_ref[...], preferred_element_type=F32)
    act = (hg * jax.nn.sigmoid(hg)) * hu * comb
    o_ref[...] += jnp.dot(act.astype(BF16), wd_ref[...], preferred_element_type=F32)


def _moe_dense(u2t, w_gate_bf, w_up_bf, w_down_bf):
    n = u2t.shape[0] // TOKEN_PITCH
    return pl.pallas_call(
        _moe_dense_kernel,
        grid=(N_EXPERTS,),
        in_specs=[pl.BlockSpec((n * TOKEN_PITCH, LANES), lambda e: (0, 0)),
                  pl.BlockSpec((None, D_MODEL, D_EXPERT), lambda e: (e, 0, 0)),
                  pl.BlockSpec((None, D_MODEL, D_EXPERT), lambda e: (e, 0, 0)),
                  pl.BlockSpec((None, D_EXPERT, D_MODEL), lambda e: (e, 0, 0))],
        out_specs=pl.BlockSpec((n, D_MODEL), lambda e: (0, 0)),
        out_shape=jax.ShapeDtypeStruct((n, D_MODEL), F32),
        compiler_params=_cparams(("arbitrary",)),
        name="moe_dense",
    )(u2t, w_gate_bf, w_up_bf, w_down_bf)


def _final_kernel(x1_ref, ffn_ref, gt2_ref, g_ref, b_ref, o_ref, *, token_tiled):
    rows = x1_ref.shape[0]
    ffn = _load_token_tiles(ffn_ref, rows) if token_tiled else ffn_ref[...]
    o_ref[...] = _layer_norm(DN_ALPHA * x1_ref[...] + gt2_ref[...] * ffn, g_ref[...], b_ref[...])


def _final_norm(x1, ffn, mod_arr, gt2_spec, ln_g, ln_b, tile):
    n = x1.shape[0]
    token_tiled = ffn.shape[-1] == LANES
    row = pl.BlockSpec((tile, D_MODEL), lambda i: (i, 0))
    ffn_spec = pl.BlockSpec((tile * TOKEN_ROWS, LANES), lambda i: (i, 0)) if token_tiled else row
    vec = pl.BlockSpec((1, D_MODEL), lambda i: (0, 0))
    return pl.pallas_call(
        functools.partial(_final_kernel, token_tiled=token_tiled),
        grid=(n // tile,),
        in_specs=[row, ffn_spec, gt2_spec, vec, vec],
        out_specs=row,
        out_shape=jax.ShapeDtypeStruct((n, D_MODEL), F32),
        compiler_params=_cparams(("arbitrary",)),
        name="final_norm",
    )(x1, ffn, mod_arr, ln_g, ln_b)


def _rope_tables(pos):
    half = HEAD_DIM // 2
    inv = ROPE_THETA ** (-jnp.arange(half, dtype=F32) * 2.0 / HEAD_DIM)
    ang = pos.astype(F32)[:, None] * inv[None, :]
    cos = jnp.cos(ang)
    sin = jnp.sin(ang)
    cos_t = jnp.tile(jnp.concatenate([cos, cos], axis=-1), (1, N_HEADS_B))
    sin_t = jnp.tile(jnp.concatenate([-sin, sin], axis=-1), (1, N_HEADS_B))
    return cos_t, sin_t


def _block_diag(w):
    n, a, b = w.shape
    eye = jnp.eye(n, dtype=w.dtype)
    return (eye[:, None, :, None] * w[:, :, None, :]).reshape(n * a, n * b)


def _prepare_weights(w_in, conv_w, conv_b, w_rg_a, b_rg_a, w_rg_x, b_rg_x, rg_lambda, g_norm_a,
                     w_router_group, b_router_group, w_router_expert, b_router_expert):
    half_blocks = N_BLK_A // 2
    w_gate = jnp.stack([
        jnp.concatenate([_block_diag(w_rg_a[h * half_blocks:(h + 1) * half_blocks]),
                         _block_diag(w_rg_x[h * half_blocks:(h + 1) * half_blocks])], axis=1)
        for h in range(2)]).astype(BF16)
    rg = dict(conv_w=conv_w, conv_b=conv_b.reshape(1, D_A), w_gate=w_gate,
              b_a=b_rg_a.reshape(1, D_A), b_x=b_rg_x.reshape(1, D_A),
              lam=rg_lambda.reshape(1, D_A), g_norm_a=g_norm_a.reshape(1, D_A))
    n_logits = N_GROUPS + N_EXPERTS
    w_r = jnp.concatenate(
        [w_router_group,
         w_router_expert.transpose(1, 0, 2).reshape(D_MODEL, N_EXPERTS),
         jnp.zeros((D_MODEL, ROUTE_W - n_logits), F32)], axis=1)
    b_r = jnp.concatenate([b_router_group, b_router_expert.reshape(-1),
                           jnp.zeros((ROUTE_W - n_logits,), F32)]).reshape(1, ROUTE_W)
    return rg, w_r, b_r


def _channel_mixing(ya, yb, x, mods, gt2_spec, weights, tile, moe_tile):
    (w_out_bf, gnb, ln1_g, ln1_b, w_r, b_r, wg_bf, wu_bf, wd_bf, ln2_g, ln2_b) = weights
    x1, u2t, route = _outproj(ya, yb, x, mods, w_out_bf, gnb, ln1_g, ln1_b, w_r, b_r, tile)
    if moe_tile is None:
        ffn = _moe_dense(u2t, wg_bf, wu_bf, wd_bf)
    else:
        ffn = _moe(u2t, route[:, 0, :].reshape(-1), wg_bf, wu_bf, wd_bf, moe_tile)
    return _final_norm(x1, ffn, mods[0], gt2_spec, ln2_g, ln2_b, tile)


def kernel(x_prompt, x_sample, state_conv, state_rglru, cache_win_k, cache_win_v, c_prompt, c_sample, w_ada, b_ada, w_in, conv_w, conv_b, w_rg_a, b_rg_a, w_rg_x, b_rg_x, rg_lambda, g_norm_a, g_norm_b, w_out, ln1_g, ln1_b, w_router_group, b_router_group, w_router_expert, b_router_expert, w_exp_gate, w_exp_up, w_exp_down, ln2_g, ln2_b):
    bp, sp, _ = x_prompt.shape
    bs, ts, _ = x_sample.shape
    n_buf = cache_win_k.shape[1]

    rg, w_r, b_r = _prepare_weights(w_in, conv_w, conv_b, w_rg_a, b_rg_a, w_rg_x, b_rg_x,
                                    rg_lambda, g_norm_a, w_router_group, b_router_group,
                                    w_router_expert, b_router_expert)
    w_in_bf = w_in.astype(BF16)
    mix_weights = (w_out.astype(BF16), g_norm_b.reshape(1, D_B), ln1_g.reshape(1, D_MODEL),
                   ln1_b.reshape(1, D_MODEL), w_r, b_r, w_exp_gate.astype(BF16),
                   w_exp_up.astype(BF16), w_exp_down.astype(BF16),
                   ln2_g.reshape(1, D_MODEL), ln2_b.reshape(1, D_MODEL))

    mod = _modulation(jnp.concatenate([c_prompt, c_sample], axis=0), w_ada, b_ada)
    mod_p, mod_s = mod[:bp], mod[bp:]

    tile_p = 512
    cos_p, sin_p = _rope_tables(jnp.arange(sp))
    ya_p, q_p, k_p, v_p, conv_p, h_p = _inproj_prompt(
        x_prompt, mod_p.reshape(bp, 6, D_MODEL), w_in_bf, cos_p, sin_p, rg)
    yb_p, kwin_p, vwin_p = _attn_prompt(q_p, k_p, v_p)
    tiles_per_seq = sp // tile_p
    mod_p3 = mod_p.reshape(bp * 6, 1, D_MODEL)
    mod_spec_p = lambda j: pl.BlockSpec((None, 1, D_MODEL),
                                        lambda i: ((i // tiles_per_seq) * 6 + j, 0, 0))
    y_p = _channel_mixing(
        ya_p.reshape(bp * sp, D_A), yb_p.reshape(bp * sp, D_B), x_prompt.reshape(bp * sp, D_MODEL),
        (mod_p3, mod_spec_p(2), mod_spec_p(3), mod_spec_p(4)), mod_spec_p(5),
        mix_weights, tile_p, 256)

    cos_s, sin_s = _rope_tables(PAST_LEN + jnp.arange(ts))
    tb = lambda t: jnp.broadcast_to(t[:, None, :], (ts, bs, D_B))
    ya_s, q_s, k_s, v_s, conv_s, h_s = _inproj_sample(
        x_sample.transpose(1, 0, 2), mod_s, w_in_bf, tb(cos_s), tb(sin_s),
        state_conv.transpose(1, 0, 2), state_rglru, rg)
    bt = lambda t: t.transpose(1, 0, 2)
    yb_s, kwin_s, vwin_s = _attn_sample(
        bt(q_s), bt(k_s), bt(v_s), cache_win_k.transpose(0, 2, 3, 1),
        cache_win_v.transpose(0, 2, 3, 1))
    mod_spec_s = lambda j: pl.BlockSpec((bs, D_MODEL), lambda i: (0, j))
    y_s = _channel_mixing(
        ya_s.reshape(ts * bs, D_A), bt(yb_s).reshape(ts * bs, D_B),
        x_sample.transpose(1, 0, 2).reshape(ts * bs, D_MODEL),
        (mod_s, mod_spec_s(2), mod_spec_s(3), mod_spec_s(4)), mod_spec_s(5),
        mix_weights, bs, None)

    heads = lambda t: t.reshape(t.shape[0], t.shape[1], N_HEADS_B, HEAD_DIM)
    return (y_p.reshape(bp, sp, D_MODEL), bt(y_s.reshape(ts, bs, D_MODEL)),
            conv_p, h_p.reshape(bp, D_A), heads(kwin_p), heads(vwin_p),
            bt(conv_s), h_s, kwin_s.transpose(0, 3, 1, 2), vwin_s.transpose(0, 3, 1, 2))
```

```python
import functools
import math

import jax
import jax.numpy as jnp
from jax import lax
from jax.experimental import pallas as pl
from jax.experimental.pallas import tpu as pltpu

F32 = jnp.float32
BF16 = jnp.bfloat16

D_MODEL = 1024
D_A = 512
N_BLK_A = 8
BLK_W_A = D_A // N_BLK_A
CONV_W = 4
RG_C = 8.0
D_B = 512
HEAD_DIM = 64
N_HEADS_B = D_B // HEAD_DIM
DILATED_CFG = ((128, 1), (512, 4), (2048, 16))
WIN_MAX = 2048
N_KEYS = 128
ROPE_THETA = 10000.0
PAST_LEN = 8192
N_GROUPS = 4
N_EXP_PER_GROUP = 4
N_EXPERTS = N_GROUPS * N_EXP_PER_GROUP
D_EXPERT = 512
DN_ALPHA = 2.0 ** 0.25
LN_EPS = 1e-5
NEG_INF = -1e30

N_PAIRS = 6
N_CLASSES = N_GROUPS * N_PAIRS
LANES = 128
SUBLANES = 8
HEAD_PAIR_W = 2 * HEAD_DIM
N_HEAD_PAIRS = N_HEADS_B // 2
Q_BLK = 128
ATTN_UNROLL = 32
DMA_UNROLL = 8
INPROJ_PARTS = 1
ROUTE_W = LANES
TOKEN_ROWS = D_MODEL // LANES
TOKEN_PITCH = TOKEN_ROWS + 1
N_BUF = 3
VMEM_LIMIT = 56 * 1024 * 1024


def _cparams(sem):
    return pltpu.CompilerParams(dimension_semantics=sem, vmem_limit_bytes=VMEM_LIMIT)


def _mod_kernel(c_ref, w_ref, b_ref, o_ref):
    c = c_ref[...]
    s = (c * jax.nn.sigmoid(c)).astype(BF16)
    o_ref[...] = jnp.dot(s, w_ref[...].astype(BF16), preferred_element_type=F32) + b_ref[...]


def _modulation(c_all, w_ada, b_ada):
    n = c_all.shape[0]
    tn = 1024
    return pl.pallas_call(
        _mod_kernel,
        grid=(6 * D_MODEL // tn,),
        in_specs=[pl.BlockSpec((n, D_MODEL), lambda j: (0, 0)),
                  pl.BlockSpec((D_MODEL, tn), lambda j: (0, j)),
                  pl.BlockSpec((1, tn), lambda j: (0, j))],
        out_specs=pl.BlockSpec((n, tn), lambda j: (0, j)),
        out_shape=jax.ShapeDtypeStruct((n, 6 * D_MODEL), F32),
        compiler_params=_cparams(("arbitrary",)),
        name="adaln_mod",
    )(c_all, w_ada, b_ada.reshape(1, -1))


def _rope_apply(t, cos, sin_signed):
    lane = lax.broadcasted_iota(jnp.int32, t.shape, t.ndim - 1)
    first_half = (lane & (HEAD_DIM - 1)) < HEAD_DIM // 2
    width = t.shape[-1]
    swapped = jnp.where(first_half,
                        pltpu.roll(t, width - HEAD_DIM // 2, t.ndim - 1),
                        pltpu.roll(t, HEAD_DIM // 2, t.ndim - 1))
    return t * cos + swapped * sin_signed


def _rglru_gates(xc, wg_ref, b_a, b_x, lam):
    half = D_A // 2
    xcb = xc.astype(BF16)
    g0 = jnp.dot(xcb[:, :half], wg_ref[0], preferred_element_type=F32)
    g1 = jnp.dot(xcb[:, half:], wg_ref[1], preferred_element_type=F32)
    r = jax.nn.sigmoid(jnp.concatenate([g0[:, :half], g1[:, :half]], axis=1) + b_a)
    i = jax.nn.sigmoid(jnp.concatenate([g0[:, half:], g1[:, half:]], axis=1) + b_x)
    z = -lam
    softplus = jnp.maximum(z, 0.0) + jnp.log1p(jnp.exp(-jnp.abs(z)))
    log_a = -RG_C * r * softplus
    a = jnp.exp(log_a)
    one_minus_a2 = -jnp.tanh(log_a) * (a * a + 1.0)
    u = jnp.sqrt(one_minus_a2) * (i * xc)
    return a, u


def _store_token_tiles(ref, x, pitch=TOKEN_ROWS):
    n = x.shape[0]
    for c in range(TOKEN_ROWS):
        ref[pl.ds(c, n, stride=pitch), :] = x[:, c * LANES:(c + 1) * LANES]


def _load_token_tiles(ref, n, pitch=TOKEN_ROWS):
    return jnp.concatenate(
        [ref[pl.ds(c, n, stride=pitch), :] for c in range(TOKEN_ROWS)], axis=1)


def _rms_norm(y, g):
    return y * lax.rsqrt(jnp.mean(y * y, axis=-1, keepdims=True) + LN_EPS) * g


def _layer_norm(x, g, b):
    mu = jnp.mean(x, axis=-1, keepdims=True)
    xc = x - mu
    var = jnp.mean(xc * xc, axis=-1, keepdims=True)
    return xc * lax.rsqrt(var + LN_EPS) * g + b


def _inproj_prompt_kernel(x_ref, mod_ref, w_in_ref, cos_ref, sin_ref, conv_w_ref, conv_b_ref,
                          wg_ref, b_a_ref, b_x_ref, lam_ref, gna_ref,
                          ya_ref, q_ref, k_ref, v_ref, conv_out_ref, h_out_ref,
                          xp_buf, h_carry):
    t = pl.program_id(1)
    rows = x_ref.shape[0]
    pad = SUBLANES

    @pl.when(t == 0)
    def _():
        xp_buf[0:pad, :] = jnp.zeros((pad, D_A), F32)
        h_carry[...] = jnp.zeros_like(h_carry)

    row = lax.broadcasted_iota(jnp.int32, (SUBLANES, D_A), 0)

    def part(r0, n, h_prev):
        sl = slice(r0, r0 + n)
        u = (x_ref[sl, :] * (1.0 + mod_ref[1:2, :]) + mod_ref[0:1, :]).astype(BF16)

        def proj(j):
            return jnp.dot(u, w_in_ref[:, j * D_A:(j + 1) * D_A], preferred_element_type=F32)

        cos = jnp.concatenate([cos_ref[sl, :]] * N_HEAD_PAIRS, axis=1)
        sin = jnp.concatenate([sin_ref[sl, :]] * N_HEAD_PAIRS, axis=1)
        q_ref[sl, :] = _rope_apply(proj(2), cos, sin)
        k_ref[sl, :] = _rope_apply(proj(3), cos, sin)
        v_ref[sl, :] = proj(4)

        xa = proj(0)
        xp_buf[pad + r0:pad + r0 + n, :] = xa
        xc = conv_b_ref[...] + xa * conv_w_ref[CONV_W - 1:CONV_W, :]
        for j in range(CONV_W - 1):
            off = pad - (CONV_W - 1) + j + r0
            xc = xc + xp_buf[off:off + n, :] * conv_w_ref[j:j + 1, :]

        a, u_in = _rglru_gates(xc, wg_ref, b_a_ref[...], b_x_ref[...], lam_ref[...])

        hs = []
        for g in range(n // SUBLANES):
            ag = a[g * SUBLANES:(g + 1) * SUBLANES]
            ug = u_in[g * SUBLANES:(g + 1) * SUBLANES]
            for sh in (1, 2, 4):
                keep = row >= sh
                a_sh = pltpu.roll(ag, sh, 0)
                u_sh = pltpu.roll(ug, sh, 0)
                ug = jnp.where(keep, ag * u_sh + ug, ug)
                ag = jnp.where(keep, ag * a_sh, ag)
            hg = ag * h_prev + ug
            hs.append(hg)
            h_prev = hg[SUBLANES - 1:SUBLANES, :]

        y = jnp.concatenate(hs, axis=0) * jax.nn.gelu(proj(1))
        ya_ref[sl, :] = _rms_norm(y, gna_ref[...]).astype(BF16)
        return h_prev

    h_last = h_carry[...]
    n_part = rows // INPROJ_PARTS
    for p in range(INPROJ_PARTS):
        h_last = part(p * n_part, n_part, h_last)
    h_carry[...] = h_last
    h_out_ref[...] = h_last

    tail = xp_buf[rows + pad - (CONV_W - 1):rows + pad, :]
    conv_out_ref[...] = tail
    xp_buf[pad - (CONV_W - 1):pad, :] = tail


def _inproj_prompt(x, mod3, w_in_bf, cos_t, sin_t, rg):
    b, s, _ = x.shape
    tile = 512
    nt = s // tile
    row_spec = lambda w: pl.BlockSpec((None, tile, w), lambda i, j: (i, j, 0))
    vec = lambda r, w: pl.BlockSpec((r, w), lambda i, j: (0, 0))
    outs = pl.pallas_call(
        _inproj_prompt_kernel,
        grid=(b, nt),
        in_specs=[row_spec(D_MODEL),
                  pl.BlockSpec((None, 6, D_MODEL), lambda i, j: (i, 0, 0)),
                  vec(D_MODEL, 5 * D_A),
                  pl.BlockSpec((tile, HEAD_PAIR_W), lambda i, j: (j, 0)),
                  pl.BlockSpec((tile, HEAD_PAIR_W), lambda i, j: (j, 0)),
                  vec(CONV_W, D_A), vec(1, D_A),
                  pl.BlockSpec((2, D_A // 2, D_A), lambda i, j: (0, 0, 0)),
                  vec(1, D_A), vec(1, D_A), vec(1, D_A), vec(1, D_A)],
        out_specs=[row_spec(D_A), row_spec(D_B), row_spec(D_B), row_spec(D_B),
                   pl.BlockSpec((None, CONV_W - 1, D_A), lambda i, j: (i, 0, 0)),
                   pl.BlockSpec((None, 1, D_A), lambda i, j: (i, 0, 0))],
        out_shape=[jax.ShapeDtypeStruct((b, s, D_A), BF16),
                   jax.ShapeDtypeStruct((b, s, D_B), F32),
                   jax.ShapeDtypeStruct((b, s, D_B), F32),
                   jax.ShapeDtypeStruct((b, s, D_B), F32),
                   jax.ShapeDtypeStruct((b, CONV_W - 1, D_A), F32),
                   jax.ShapeDtypeStruct((b, 1, D_A), F32)],
        scratch_shapes=[pltpu.VMEM((tile + SUBLANES, D_A), F32),
                        pltpu.VMEM((1, D_A), F32)],
        compiler_params=_cparams(("arbitrary", "arbitrary")),
        name="inproj_prompt",
    )(x, mod3, w_in_bf, cos_t, sin_t, rg["conv_w"], rg["conv_b"], rg["w_gate"],
      rg["b_a"], rg["b_x"], rg["lam"], rg["g_norm_a"])
    return outs


def _attn_prompt_kernel(q_ref, k_ref, v_ref, o_ref, kwin_ref, vwin_ref, acc_s, m_s, l_s, bias_s):
    s = q_ref.shape[0]
    keep = kwin_ref.shape[0]
    kwin_ref[...] = k_ref[s - keep:s, :]
    vwin_ref[...] = v_ref[s - keep:s, :]

    lane = lax.broadcasted_iota(jnp.int32, (Q_BLK, HEAD_PAIR_W), 1)
    head0 = lane < HEAD_DIM
    nk = 2 * Q_BLK

    qi = lax.broadcasted_iota(jnp.int32, (Q_BLK, nk), 0)
    ki = lax.broadcasted_iota(jnp.int32, (Q_BLK, nk), 1)
    for slot in range(2):
        dist = slot * Q_BLK + qi - ki
        bias_s[slot] = jnp.where((dist >= 0) & (dist <= N_KEYS), 0.0, NEG_INF)

    def rows(start, n, d):
        return pl.ds(start, n) if d == 1 else pl.ds(start, n, stride=d)

    def unit(bi, d, nb, u):
        r = u // nb
        j = u % nb
        jk = jnp.maximum(j - 1, 0)
        start_q = r + d * Q_BLK * j
        start_k = r + d * Q_BLK * jk
        bias = bias_s[j - jk]
        qb = q_ref[rows(start_q, Q_BLK, d), :] * (HEAD_DIM ** -0.5)
        kb = k_ref[rows(start_k, nk, d), :].astype(BF16)
        vb = v_ref[rows(start_k, nk, d), :].astype(BF16)
        q2 = jnp.concatenate([jnp.where(head0, qb, 0.0), jnp.where(head0, 0.0, qb)],
                             axis=0).astype(BF16)
        sc = lax.dot_general(q2, kb, (((1,), (1,)), ((), ())), preferred_element_type=F32)
        sc = sc + jnp.concatenate([bias, bias], axis=0)
        m = jnp.max(sc, axis=-1, keepdims=True)
        p = jnp.exp(sc - m)
        l = jnp.sum(p, axis=-1, keepdims=True)
        o = jnp.dot(p.astype(BF16), vb, preferred_element_type=F32)
        dst = rows(start_q, Q_BLK, d)
        acc_s[bi, dst, :] = jnp.where(head0, o[0:Q_BLK], o[Q_BLK:])
        m_s[bi, dst, :] = jnp.where(head0, m[0:Q_BLK], m[Q_BLK:])
        l_s[bi, dst, :] = jnp.where(head0, l[0:Q_BLK], l[Q_BLK:])

    for bi, (_, d) in enumerate(DILATED_CFG):
        nb = s // d // Q_BLK

        def body(u, carry, bi=bi, d=d, nb=nb):
            unit(bi, d, nb, u)
            return carry

        lax.fori_loop(0, d * nb, body, 0, unroll=ATTN_UNROLL)

    chunk = 512

    def merge(i, carry):
        r0 = pl.multiple_of(i * chunk, chunk)
        sl = pl.ds(r0, chunk)
        ms = [m_s[bi, sl, :] for bi in range(len(DILATED_CFG))]
        mx = functools.reduce(jnp.maximum, ms)
        num = jnp.zeros((chunk, HEAD_PAIR_W), F32)
        den = jnp.zeros((chunk, HEAD_PAIR_W), F32)
        for bi in range(len(DILATED_CFG)):
            w = jnp.exp(ms[bi] - mx)
            num = num + w * acc_s[bi, sl, :]
            den = den + w * l_s[bi, sl, :]
        o_ref[sl, :] = num / den
        return carry

    lax.fori_loop(0, s // chunk, merge, 0)


def _attn_prompt(q, k, v):
    b, s, _ = q.shape
    keep = min(WIN_MAX, s)
    nbr = len(DILATED_CFG)
    spec = pl.BlockSpec((None, s, HEAD_PAIR_W), lambda i, j: (i, 0, j))
    wspec = pl.BlockSpec((None, keep, HEAD_PAIR_W), lambda i, j: (i, 0, j))
    return pl.pallas_call(
        _attn_prompt_kernel,
        grid=(b, N_HEAD_PAIRS),
        in_specs=[spec, spec, spec],
        out_specs=[spec, wspec, wspec],
        out_shape=[jax.ShapeDtypeStruct((b, s, D_B), F32),
                   jax.ShapeDtypeStruct((b, keep, D_B), F32),
                   jax.ShapeDtypeStruct((b, keep, D_B), F32)],
        scratch_shapes=[pltpu.VMEM((nbr, s, HEAD_PAIR_W), F32),
                        pltpu.VMEM((nbr, s, HEAD_PAIR_W), F32),
                        pltpu.VMEM((nbr, s, HEAD_PAIR_W), F32),
                        pltpu.VMEM((2, Q_BLK, 2 * Q_BLK), F32)],
        compiler_params=_cparams(("arbitrary", "arbitrary")),
        name="attn_prompt",
    )(q, k, v)


def _inproj_sample_kernel(x_ref, mod_ref, w_in_ref, cos_ref, sin_ref, conv_state_ref, h0_ref,
                          conv_w_ref, conv_b_ref, wg_ref, b_a_ref, b_x_ref, lam_ref, gna_ref,
                          ya_ref, q_ref, k_ref, v_ref, conv_out_ref, h_out_ref):
    nt, nb, _ = x_ref.shape
    sh1 = mod_ref[:, 0:D_MODEL]
    sc1 = mod_ref[:, D_MODEL:2 * D_MODEL]
    u = (x_ref[...] * (1.0 + sc1)[None] + sh1[None]).astype(BF16).reshape(nt * nb, D_MODEL)

    def proj(j):
        return jnp.dot(u, w_in_ref[:, j * D_A:(j + 1) * D_A], preferred_element_type=F32)

    cos = cos_ref[...].reshape(nt * nb, D_B)
    sin = sin_ref[...].reshape(nt * nb, D_B)
    q_ref[...] = _rope_apply(proj(2), cos, sin).reshape(nt, nb, D_B)
    k_ref[...] = _rope_apply(proj(3), cos, sin).reshape(nt, nb, D_B)
    v_ref[...] = proj(4).reshape(nt, nb, D_B)

    xa = proj(0).reshape(nt, nb, D_A)
    xp = [conv_state_ref[j] for j in range(CONV_W - 1)] + [xa[t] for t in range(nt)]
    xc = jnp.concatenate(
        [conv_b_ref[...] + sum(xp[t + j] * conv_w_ref[j:j + 1, :] for j in range(CONV_W))
         for t in range(nt)], axis=0)
    for j in range(CONV_W - 1):
        conv_out_ref[j] = xp[nt + j]

    a, u_in = _rglru_gates(xc, wg_ref, b_a_ref[...], b_x_ref[...], lam_ref[...])
    h = h0_ref[...]
    hs = []
    for t in range(nt):
        h = a[t * nb:(t + 1) * nb] * h + u_in[t * nb:(t + 1) * nb]
        hs.append(h)
    h_out_ref[...] = h
    y = jnp.concatenate(hs, axis=0) * jax.nn.gelu(proj(1))
    ya_ref[...] = _rms_norm(y, gna_ref[...]).astype(BF16).reshape(nt, nb, D_A)


def _inproj_sample(x_tb, mod_s, w_in_bf, cos_t, sin_t, conv_state_tb, h0, rg):
    nt, nb, _ = x_tb.shape
    full = lambda shape: pl.BlockSpec(shape, lambda i: (0,) * len(shape))
    return pl.pallas_call(
        _inproj_sample_kernel,
        grid=(1,),
        in_specs=[full((nt, nb, D_MODEL)), full((nb, 6 * D_MODEL)), full((D_MODEL, 5 * D_A)),
                  full((nt, nb, D_B)), full((nt, nb, D_B)),
                  full((CONV_W - 1, nb, D_A)), full((nb, D_A)),
                  full((CONV_W, D_A)), full((1, D_A)), full((2, D_A // 2, D_A)),
                  full((1, D_A)), full((1, D_A)), full((1, D_A)), full((1, D_A))],
        out_specs=[full((nt, nb, D_A)), full((nt, nb, D_B)), full((nt, nb, D_B)),
                   full((nt, nb, D_B)), full((CONV_W - 1, nb, D_A)), full((nb, D_A))],
        out_shape=[jax.ShapeDtypeStruct((nt, nb, D_A), BF16),
                   jax.ShapeDtypeStruct((nt, nb, D_B), F32),
                   jax.ShapeDtypeStruct((nt, nb, D_B), F32),
                   jax.ShapeDtypeStruct((nt, nb, D_B), F32),
                   jax.ShapeDtypeStruct((CONV_W - 1, nb, D_A), F32),
                   jax.ShapeDtypeStruct((nb, D_A), F32)],
        compiler_params=_cparams(("arbitrary",)),
        name="inproj_sample",
    )(x_tb, mod_s, w_in_bf, cos_t, sin_t, conv_state_tb, h0, rg["conv_w"], rg["conv_b"],
      rg["w_gate"], rg["b_a"], rg["b_x"], rg["lam"], rg["g_norm_a"])


def _attn_sample_kernel(q_ref, kn_ref, vn_ref, ck_ref, cv_ref, o_ref, kwin_ref, vwin_ref,
                        kn_pad, vn_pad):
    nt = q_ref.shape[0]
    n_buf = ck_ref.shape[-1]
    n_rows = N_HEADS_B * nt

    @pl.when(pl.program_id(0) == 0)
    def _():
        kn_pad[...] = jnp.zeros_like(kn_pad)
        vn_pad[...] = jnp.zeros_like(vn_pad)

    kn_pad[0:nt, :] = kn_ref[...]
    vn_pad[0:nt, :] = vn_ref[...]
    ck_t = ck_ref[...].reshape(D_B, n_buf)
    cv_t = cv_ref[...].reshape(D_B, n_buf)

    tail_lane = lax.broadcasted_iota(jnp.int32, (D_B, LANES), 1)

    def shift_in(old_t, new_pad, out_ref):
        rolled = pltpu.roll(old_t, n_buf - nt, 1)
        new_t = pltpu.roll(new_pad.T, LANES - nt, 1)
        last = jnp.where(tail_lane < LANES - nt, rolled[:, n_buf - LANES:n_buf], new_t)
        out_ref[:, :, 0:n_buf - LANES] = rolled[:, 0:n_buf - LANES].reshape(
            N_HEADS_B, HEAD_DIM, n_buf - LANES)
        out_ref[:, :, n_buf - LANES:n_buf] = last.reshape(N_HEADS_B, HEAD_DIM, LANES)

    shift_in(ck_t, kn_pad[...], kwin_ref)
    shift_in(cv_t, vn_pad[...], vwin_ref)

    ri = lax.broadcasted_iota(jnp.int32, (n_rows, nt), 0)
    ci = lax.broadcasted_iota(jnp.int32, (n_rows, nt), 1)
    pick = (ri % nt == ci).astype(BF16)
    qs = (q_ref[...] * (HEAD_DIM ** -0.5)).astype(BF16)
    q_rep = jnp.dot(pick, qs, preferred_element_type=F32)
    row_h = lax.broadcasted_iota(jnp.int32, (n_rows, D_B), 0) // nt
    lane_h = lax.broadcasted_iota(jnp.int32, (n_rows, D_B), 1) // HEAD_DIM
    own = row_h == lane_h
    qbd = jnp.where(own, q_rep, 0.0).astype(BF16)

    nt_dims = (((1,), (1,)), ((), ()))
    sc_c = jnp.dot(qbd, ck_t.astype(BF16), preferred_element_type=F32)
    sc_n = lax.dot_general(qbd, kn_pad[...].astype(BF16), nt_dims, preferred_element_type=F32)

    def mult(dist, limit_ok):
        c = jnp.zeros(dist.shape, F32)
        for win, d in DILATED_CFG:
            hit = (dist >= 0) & (dist <= win) & (dist % d == 0) & limit_ok
            c = c + hit.astype(F32)
        return c

    t_c = lax.broadcasted_iota(jnp.int32, sc_c.shape, 0) % nt
    dist_c = n_buf + t_c - lax.broadcasted_iota(jnp.int32, sc_c.shape, 1)
    mult_c = mult(dist_c, dist_c >= 0)
    t_n = lax.broadcasted_iota(jnp.int32, sc_n.shape, 0) % nt
    col_n = lax.broadcasted_iota(jnp.int32, sc_n.shape, 1)
    mult_n = mult(t_n - col_n, col_n < nt)

    sc_c = jnp.where(mult_c > 0, sc_c, NEG_INF)
    sc_n = jnp.where(mult_n > 0, sc_n, NEG_INF)
    m = jnp.maximum(jnp.max(sc_c, axis=-1, keepdims=True), jnp.max(sc_n, axis=-1, keepdims=True))
    p_c = mult_c * jnp.exp(sc_c - m)
    p_n = mult_n * jnp.exp(sc_n - m)
    l = jnp.sum(p_c, axis=-1, keepdims=True) + jnp.sum(p_n, axis=-1, keepdims=True)
    acc = (lax.dot_general(p_c.astype(BF16), cv_t.astype(BF16), nt_dims,
                           preferred_element_type=F32)
           + jnp.dot(p_n.astype(BF16), vn_pad[...].astype(BF16), preferred_element_type=F32))
    o_full = jnp.where(own, acc / l, 0.0)
    out = o_full[0:nt, :]
    for h in range(1, N_HEADS_B):
        out = out + o_full[h * nt:(h + 1) * nt, :]
    o_ref[...] = out


def _attn_sample(q, k_new, v_new, cache_k_t, cache_v_t):
    b, nt, _ = q.shape
    n_buf = cache_k_t.shape[-1]
    small = pl.BlockSpec((None, nt, D_B), lambda i: (i, 0, 0))
    big = pl.BlockSpec((None, N_HEADS_B, HEAD_DIM, n_buf), lambda i: (i, 0, 0, 0))
    win = jax.ShapeDtypeStruct((b, N_HEADS_B, HEAD_DIM, n_buf), F32)
    return pl.pallas_call(
        _attn_sample_kernel,
        grid=(b,),
        in_specs=[small, small, small, big, big],
        out_specs=[small, big, big],
        out_shape=[jax.ShapeDtypeStruct((b, nt, D_B), F32), win, win],
        scratch_shapes=[pltpu.VMEM((LANES, D_B), F32), pltpu.VMEM((LANES, D_B), F32)],
        compiler_params=_cparams(("arbitrary",)),
        name="attn_sample",
    )(q, k_new, v_new, cache_k_t, cache_v_t)


def _split_bf16(x):
    hi = x.astype(BF16)
    lo = (x - hi.astype(F32)).astype(BF16)
    return hi, lo


def _first_argmax(vals):
    mx = functools.reduce(jnp.maximum, vals)
    idx = jnp.full(mx.shape, float(len(vals) - 1), F32)
    for j in range(len(vals) - 2, -1, -1):
        idx = jnp.where(vals[j] == mx, float(j), idx)
    return mx, idx


def _outproj_kernel(ya_ref, yb_ref, x_ref, gt1_ref, sh2_ref, sc2_ref, w_out_ref, gnb_ref,
                    ln_g_ref, ln_b_ref, w_r_ref, b_r_ref, x1_ref, u2_ref, route_ref):
    yb = _rms_norm(yb_ref[...], gnb_ref[...]).astype(BF16)
    mixed = (jnp.dot(ya_ref[...], w_out_ref[0:D_A, :], preferred_element_type=F32)
             + jnp.dot(yb, w_out_ref[D_A:D_A + D_B, :], preferred_element_type=F32))
    x1 = _layer_norm(DN_ALPHA * x_ref[...] + gt1_ref[...] * mixed, ln_g_ref[...], ln_b_ref[...])
    x1_ref[...] = x1
    u2 = x1 * (1.0 + sc2_ref[...]) + sh2_ref[...]
    _store_token_tiles(u2_ref, u2, TOKEN_PITCH)

    u_hi, u_lo = _split_bf16(u2)
    w_hi, w_lo = _split_bf16(w_r_ref[...])
    both = jnp.dot(u_hi, jnp.concatenate([w_hi, w_lo], axis=1), preferred_element_type=F32)
    logits = (both[:, 0:ROUTE_W] + jnp.dot(u_lo, w_hi, preferred_element_type=F32)
              + both[:, ROUTE_W:2 * ROUTE_W]) + b_r_ref[...]
    lt = logits.T

    g_rows = [lt[j:j + 1, :] for j in range(N_GROUPS)]
    g_max, g_idx = _first_argmax(g_rows)
    p_group = 1.0 / sum(jnp.exp(g - g_max) for g in g_rows)
    e_rows = []
    for e in range(N_EXP_PER_GROUP):
        acc = jnp.zeros_like(g_max)
        for g in range(N_GROUPS):
            r = N_GROUPS + g * N_EXP_PER_GROUP + e
            acc = jnp.where(g_idx == float(g), lt[r:r + 1, :], acc)
        e_rows.append(acc)
    v1, i1 = _first_argmax(e_rows)
    rest = [jnp.where(i1 == float(e), -jnp.inf, e_rows[e]) for e in range(N_EXP_PER_GROUP)]
    v2, i2 = _first_argmax(rest)
    ex = jnp.exp(v2 - v1)
    w1 = p_group / (1.0 + ex)
    w2 = p_group * ex / (1.0 + ex)
    lo = jnp.minimum(i1, i2)
    hi = jnp.maximum(i1, i2)
    pair = jnp.where(lo == 0.0, hi - 1.0, jnp.where(lo == 1.0, hi + 1.0, 5.0))
    cls = g_idx * float(N_PAIRS) + pair
    w_of_lo = jnp.where(i1 < i2, w1, w2)
    w_of_hi = jnp.where(i1 < i2, w2, w1)
    e_lo = g_idx * float(N_EXP_PER_GROUP) + lo
    e_hi = g_idx * float(N_EXP_PER_GROUP) + hi
    n_tok = cls.shape[1]
    route = jnp.concatenate(
        [cls, w_of_lo, w_of_hi, e_lo, e_hi, jnp.zeros((ROUTE_W - 5, n_tok), F32)], axis=0)
    route_ref[...] = route[0:SUBLANES, :]
    u2_ref[pl.ds(TOKEN_ROWS, n_tok, stride=TOKEN_PITCH), :] = route.T


def _outproj(ya, yb, x, mods, w_out_bf, gnb, ln_g, ln_b, w_r, b_r, tile):
    n = x.shape[0]
    nt = n // tile
    mod_arr, gt1_spec, sh2_spec, sc2_spec = mods
    row = lambda w: pl.BlockSpec((tile, w), lambda i: (i, 0))
    vec = lambda r, w: pl.BlockSpec((r, w), lambda i: (0, 0))
    return pl.pallas_call(
        _outproj_kernel,
        grid=(nt,),
        in_specs=[row(D_A), row(D_B), row(D_MODEL), gt1_spec, sh2_spec, sc2_spec,
                  vec(D_MODEL, D_MODEL), vec(1, D_B), vec(1, D_MODEL), vec(1, D_MODEL),
                  vec(D_MODEL, ROUTE_W), vec(1, ROUTE_W)],
        out_specs=[row(D_MODEL),
                   pl.BlockSpec((tile * TOKEN_PITCH, LANES), lambda i: (i, 0)),
                   pl.BlockSpec((None, SUBLANES, tile), lambda i: (i, 0, 0))],
        out_shape=[jax.ShapeDtypeStruct((n, D_MODEL), F32),
                   jax.ShapeDtypeStruct((n * TOKEN_PITCH, LANES), F32),
                   jax.ShapeDtypeStruct((nt, SUBLANES, tile), F32)],
        compiler_params=_cparams(("arbitrary",)),
        name="outproj_router",
    )(ya, yb, x, mod_arr, mod_arr, mod_arr, w_out_bf, gnb, ln_g, ln_b, w_r, b_r)


def _moe_kernel(e_lo_ref, e_hi_ref, n_used_ref, src_ref, dst_ref,
                x_hbm, wg_lo, wg_hi, wu_lo, wu_hi, wd_lo, wd_hi, o_hbm,
                xg, og, gsem, ssem, *, n_tokens, tile):
    i = pl.program_id(0)
    n_steps = pl.num_programs(0)
    n_used = n_used_ref[0]
    cur = i % N_BUF
    ahead = (i + 2) % N_BUF
    rows = tile * TOKEN_ROWS

    def gather_token(base, r, buf):
        pltpu.make_async_copy(x_hbm.at[pl.ds(src_ref[base + r], TOKEN_PITCH)],
                              xg.at[buf, pl.ds(r * TOKEN_PITCH, TOKEN_PITCH)],
                              gsem.at[buf]).start()

    def gather_wait(buf):
        pltpu.make_async_copy(x_hbm.at[pl.ds(0, tile * TOKEN_PITCH)], xg.at[buf],
                              gsem.at[buf]).wait()

    def scatter_wait(buf):
        pltpu.make_async_copy(og.at[buf], o_hbm.at[pl.ds(0, rows)], ssem.at[buf]).wait()

    @pl.when(i == 0)
    def _():
        def first(r, c):
            gather_token(0, r, 0)
            gather_token(jnp.minimum(1, n_steps - 1) * tile, r, 1)
            return c
        lax.fori_loop(0, tile, first, 0, unroll=DMA_UNROLL)
        og[...] = jnp.zeros_like(og)
        for buf in range(N_BUF):
            pad_rows = pltpu.make_async_copy(
                og.at[buf], o_hbm.at[pl.ds((n_tokens + buf * tile) * TOKEN_ROWS, rows)],
                ssem.at[buf])
            pad_rows.start()
            pad_rows.wait()

    @pl.when(i <= n_used)
    def _():
        gather_wait(cur)

        @pl.when(i >= 2)
        def _():
            scatter_wait(cur)

        nxt = jnp.minimum(i + 2, n_steps - 1) * tile
        for r in range(tile):
            gather_token(nxt, r, ahead)
        prev = i * tile
        for r in range(tile):
            row = pl.multiple_of(dst_ref[prev + r], TOKEN_ROWS)
            pltpu.make_async_copy(og.at[ahead, pl.ds(r * TOKEN_ROWS, TOKEN_ROWS)],
                                  o_hbm.at[pl.ds(row, TOKEN_ROWS)], ssem.at[ahead]).start()

        xb = _load_token_tiles(xg.at[cur], tile, TOKEN_PITCH).astype(BF16)
        w2 = xg[cur, pl.ds(TOKEN_ROWS, tile, stride=TOKEN_PITCH), :]
        out = jnp.zeros((tile, D_MODEL), F32)
        for col, wg, wu, wd in ((1, wg_lo, wu_lo, wd_lo), (2, wg_hi, wu_hi, wd_hi)):
            hg = jnp.dot(xb, wg[...], preferred_element_type=F32)
            hu = jnp.dot(xb, wu[...], preferred_element_type=F32)
            act = (hg * jax.nn.sigmoid(hg)) * hu * w2[:, col:col + 1]
            out = out + jnp.dot(act.astype(BF16), wd[...], preferred_element_type=F32)
        _store_token_tiles(og.at[cur], out)

        @pl.when(i == n_used)
        def _():
            for buf in range(N_BUF):
                @pl.when(buf != cur)
                def _():
                    gather_wait(buf)
                    scatter_wait(buf)


def _moe(u2t, cls, w_gate_bf, w_up_bf, w_down_bf, tile):
    n = cls.shape[0]
    n_steps = n // tile + N_CLASSES
    cls = cls.astype(jnp.int32)
    order = jnp.argsort(cls, stable=True).astype(jnp.int32)
    class_ids = jnp.arange(N_CLASSES, dtype=jnp.int32)
    counts = jnp.sum((cls[:, None] == class_ids[None, :]).astype(jnp.int32), axis=0)
    tiles_per = (counts + tile - 1) // tile
    tile_end = jnp.cumsum(tiles_per)
    tile_off = tile_end - tiles_per
    n_used = tile_end[-1]
    class_start = jnp.cumsum(counts) - counts
    step = jnp.arange(n_steps, dtype=jnp.int32)
    step_c = jnp.minimum(step, n_used - 1)
    cls_of = jnp.sum((step_c[:, None] >= tile_end[None, :]).astype(jnp.int32), axis=1)
    onehot = (cls_of[:, None] == class_ids[None, :]).astype(jnp.int32)
    pick = lambda table: jnp.sum(onehot * table[None, :], axis=1)
    local = step - pick(tile_off)
    nvalid = jnp.where(step < n_used, jnp.clip(pick(counts) - local * tile, 0, tile), 0)
    r = jnp.arange(tile, dtype=jnp.int32)
    pos = pick(class_start)[:, None] + local[:, None] * tile + r[None, :]
    valid = r[None, :] < nvalid[:, None]
    tok = order[jnp.clip(pos, 0, n - 1)]
    src = (jnp.where(valid, tok, 0) * TOKEN_PITCH).astype(jnp.int32).reshape(-1)
    spare = n + (step[:, None] % N_BUF) * tile + r[None, :]
    dst = jnp.where(valid, tok, spare)
    dst = jnp.concatenate([(n + (N_BUF - 1) * tile + r)[None, :], dst], axis=0)
    dst = (dst * TOKEN_ROWS).astype(jnp.int32).reshape(-1)
    grp = cls_of // N_PAIRS
    pair = cls_of % N_PAIRS
    pair_lo = (pair >= 3).astype(jnp.int32) + (pair >= 5).astype(jnp.int32)
    pair_hi = pair + 1 - 2 * (pair >= 3).astype(jnp.int32) - (pair >= 5).astype(jnp.int32)
    e_lo = (grp * N_EXP_PER_GROUP + pair_lo).astype(jnp.int32)
    e_hi = (grp * N_EXP_PER_GROUP + pair_hi).astype(jnp.int32)

    w_in_spec = lambda which: pl.BlockSpec(
        (None, D_MODEL, D_EXPERT), lambda i, elo, ehi, nu, s, d: ((elo, ehi)[which][i], 0, 0))
    w_dn_spec = lambda which: pl.BlockSpec(
        (None, D_EXPERT, D_MODEL), lambda i, elo, ehi, nu, s, d: ((elo, ehi)[which][i], 0, 0))
    grid_spec = pltpu.PrefetchScalarGridSpec(
        num_scalar_prefetch=5,
        grid=(n_steps,),
        in_specs=[pl.BlockSpec(memory_space=pl.ANY),
                  w_in_spec(0), w_in_spec(1), w_in_spec(0), w_in_spec(1),
                  w_dn_spec(0), w_dn_spec(1)],
        out_specs=pl.BlockSpec(memory_space=pl.ANY),
        scratch_shapes=[pltpu.VMEM((N_BUF, tile * TOKEN_PITCH, LANES), F32),
                        pltpu.VMEM((N_BUF, tile * TOKEN_ROWS, LANES), F32),
                        pltpu.SemaphoreType.DMA((N_BUF,)),
                        pltpu.SemaphoreType.DMA((N_BUF,))],
    )
    return pl.pallas_call(
        functools.partial(_moe_kernel, n_tokens=n, tile=tile),
        grid_spec=grid_spec,
        out_shape=jax.ShapeDtypeStruct(((n + N_BUF * tile) * TOKEN_ROWS, LANES), F32),
        compiler_params=_cparams(("arbitrary",)),
        name="moe_sparse",
    )(e_lo, e_hi, n_used.reshape(1).astype(jnp.int32), src, dst,
      u2t, w_gate_bf, w_gate_bf, w_up_bf, w_up_bf, w_down_bf, w_down_bf)


def _moe_dense_kernel(x_ref, wg_ref, wu_ref, wd_ref, o_ref):
    e = pl.program_id(0)
    n = o_ref.shape[0]

    @pl.when(e == 0)
    def _():
        o_ref[...] = jnp.zeros_like(o_ref)

    xb = _load_token_tiles(x_ref, n, TOKEN_PITCH).astype(BF16)
    ef = e.astype(F32)
    route = x_ref[pl.ds(TOKEN_ROWS, n, stride=TOKEN_PITCH), :]
    col = lambda c: route[:, c:c + 1]
    comb = jnp.where(col(3) == ef, col(1), 0.0) + jnp.where(col(4) == ef, col(2), 0.0)
    hg = jnp.dot(xb, wg_ref[...], preferred_element_type=F32)
    hu = jnp.dot(xb, wu_ref[...], preferred_element_type=F32)
    act = (hg * jax.nn.sigmoid(hg)) * hu * comb
    o_ref[...] += jnp.dot(act.astype(BF16), wd_ref[...], preferred_element_type=F32)


def _moe_dense(u2t, w_gate_bf, w_up_bf, w_down_bf):
    n = u2t.shape[0] // TOKEN_PITCH
    return pl.pallas_call(
        _moe_dense_kernel,
        grid=(N_EXPERTS,),
        in_specs=[pl.BlockSpec((n * TOKEN_PITCH, LANES), lambda e: (0, 0)),
                  pl.BlockSpec((None, D_MODEL, D_EXPERT), lambda e: (e, 0, 0)),
                  pl.BlockSpec((None, D_MODEL, D_EXPERT), lambda e: (e, 0, 0)),
                  pl.BlockSpec((None, D_EXPERT, D_MODEL), lambda e: (e, 0, 0))],
        out_specs=pl.BlockSpec((n, D_MODEL), lambda e: (0, 0)),
        out_shape=jax.ShapeDtypeStruct((n, D_MODEL), F32),
        compiler_params=_cparams(("arbitrary",)),
        name="moe_dense",
    )(u2t, w_gate_bf, w_up_bf, w_down_bf)


def _final_kernel(x1_ref, ffn_ref, gt2_ref, g_ref, b_ref, o_ref, *, token_tiled):
    rows = x1_ref.shape[0]
    ffn = _load_token_tiles(ffn_ref, rows) if token_tiled else ffn_ref[...]
    o_ref[...] = _layer_norm(DN_ALPHA * x1_ref[...] + gt2_ref[...] * ffn, g_ref[...], b_ref[...])


def _final_norm(x1, ffn, mod_arr, gt2_spec, ln_g, ln_b, tile):
    n = x1.shape[0]
    token_tiled = ffn.shape[-1] == LANES
    row = pl.BlockSpec((tile, D_MODEL), lambda i: (i, 0))
    ffn_spec = pl.BlockSpec((tile * TOKEN_ROWS, LANES), lambda i: (i, 0)) if token_tiled else row
    vec = pl.BlockSpec((1, D_MODEL), lambda i: (0, 0))
    return pl.pallas_call(
        functools.partial(_final_kernel, token_tiled=token_tiled),
        grid=(n // tile,),
        in_specs=[row, ffn_spec, gt2_spec, vec, vec],
        out_specs=row,
        out_shape=jax.ShapeDtypeStruct((n, D_MODEL), F32),
        compiler_params=_cparams(("arbitrary",)),
        name="final_norm",
    )(x1, ffn, mod_arr, ln_g, ln_b)


def _rope_tables(pos):
    half = HEAD_DIM // 2
    inv = ROPE_THETA ** (-jnp.arange(half, dtype=F32) * 2.0 / HEAD_DIM)
    ang = pos.astype(F32)[:, None] * inv[None, :]
    cos = jnp.cos(ang)
    sin = jnp.sin(ang)
    cos_t = jnp.tile(jnp.concatenate([cos, cos], axis=-1), (1, 2))
    sin_t = jnp.tile(jnp.concatenate([-sin, sin], axis=-1), (1, 2))
    return cos_t, sin_t


def _block_diag(w):
    n, a, b = w.shape
    eye = jnp.eye(n, dtype=w.dtype)
    return (eye[:, None, :, None] * w[:, :, None, :]).reshape(n * a, n * b)


def _prepare_weights(w_in, conv_w, conv_b, w_rg_a, b_rg_a, w_rg_x, b_rg_x, rg_lambda, g_norm_a,
                     w_router_group, b_router_group, w_router_expert, b_router_expert):
    half_blocks = N_BLK_A // 2
    w_gate = jnp.stack([
        jnp.concatenate([_block_diag(w_rg_a[h * half_blocks:(h + 1) * half_blocks]),
                         _block_diag(w_rg_x[h * half_blocks:(h + 1) * half_blocks])], axis=1)
        for h in range(2)]).astype(BF16)
    rg = dict(conv_w=conv_w, conv_b=conv_b.reshape(1, D_A), w_gate=w_gate,
              b_a=b_rg_a.reshape(1, D_A), b_x=b_rg_x.reshape(1, D_A),
              lam=rg_lambda.reshape(1, D_A), g_norm_a=g_norm_a.reshape(1, D_A))
    n_logits = N_GROUPS + N_EXPERTS
    w_r = jnp.concatenate(
        [w_router_group,
         w_router_expert.transpose(1, 0, 2).reshape(D_MODEL, N_EXPERTS),
         jnp.zeros((D_MODEL, ROUTE_W - n_logits), F32)], axis=1)
    b_r = jnp.concatenate([b_router_group, b_router_expert.reshape(-1),
                           jnp.zeros((ROUTE_W - n_logits,), F32)]).reshape(1, ROUTE_W)
    return rg, w_r, b_r


def _channel_mixing(ya, yb, x, mods, gt2_spec, weights, tile, moe_tile):
    (w_out_bf, gnb, ln1_g, ln1_b, w_r, b_r, wg_bf, wu_bf, wd_bf, ln2_g, ln2_b) = weights
    x1, u2t, route = _outproj(ya, yb, x, mods, w_out_bf, gnb, ln1_g, ln1_b, w_r, b_r, tile)
    if moe_tile is None:
        ffn = _moe_dense(u2t, wg_bf, wu_bf, wd_bf)
    else:
        ffn = _moe(u2t, route[:, 0, :].reshape(-1), wg_bf, wu_bf, wd_bf, moe_tile)
    return _final_norm(x1, ffn, mods[0], gt2_spec, ln2_g, ln2_b, tile)


def kernel(x_prompt, x_sample, state_conv, state_rglru, cache_win_k, cache_win_v, c_prompt, c_sample, w_ada, b_ada, w_in, conv_w, conv_b, w_rg_a, b_rg_a, w_rg_x, b_rg_x, rg_lambda, g_norm_a, g_norm_b, w_out, ln1_g, ln1_b, w_router_group, b_router_group, w_router_expert, b_router_expert, w_exp_gate, w_exp_up, w_exp_down, ln2_g, ln2_b):
    bp, sp, _ = x_prompt.shape
    bs, ts, _ = x_sample.shape
    n_buf = cache_win_k.shape[1]

    rg, w_r, b_r = _prepare_weights(w_in, conv_w, conv_b, w_rg_a, b_rg_a, w_rg_x, b_rg_x,
                                    rg_lambda, g_norm_a, w_router_group, b_router_group,
                                    w_router_expert, b_router_expert)
    w_in_bf = w_in.astype(BF16)
    mix_weights = (w_out.astype(BF16), g_norm_b.reshape(1, D_B), ln1_g.reshape(1, D_MODEL),
                   ln1_b.reshape(1, D_MODEL), w_r, b_r, w_exp_gate.astype(BF16),
                   w_exp_up.astype(BF16), w_exp_down.astype(BF16),
                   ln2_g.reshape(1, D_MODEL), ln2_b.reshape(1, D_MODEL))

    mod = _modulation(jnp.concatenate([c_prompt, c_sample], axis=0), w_ada, b_ada)
    mod_p, mod_s = mod[:bp], mod[bp:]

    tile_p = 512
    cos_p, sin_p = _rope_tables(jnp.arange(sp))
    ya_p, q_p, k_p, v_p, conv_p, h_p = _inproj_prompt(
        x_prompt, mod_p.reshape(bp, 6, D_MODEL), w_in_bf, cos_p, sin_p, rg)
    yb_p, kwin_p, vwin_p = _attn_prompt(q_p, k_p, v_p)
    tiles_per_seq = sp // tile_p
    mod_p3 = mod_p.reshape(bp * 6, 1, D_MODEL)
    mod_spec_p = lambda j: pl.BlockSpec((None, 1, D_MODEL),
                                        lambda i: ((i // tiles_per_seq) * 6 + j, 0, 0))
    y_p = _channel_mixing(
        ya_p.reshape(bp * sp, D_A), yb_p.reshape(bp * sp, D_B), x_prompt.reshape(bp * sp, D_MODEL),
        (mod_p3, mod_spec_p(2), mod_spec_p(3), mod_spec_p(4)), mod_spec_p(5),
        mix_weights, tile_p, 256)

    cos_s, sin_s = _rope_tables(PAST_LEN + jnp.arange(ts))
    tb = lambda t: jnp.broadcast_to(jnp.tile(t, (1, N_HEAD_PAIRS))[:, None, :], (ts, bs, D_B))
    ya_s, q_s, k_s, v_s, conv_s, h_s = _inproj_sample(
        x_sample.transpose(1, 0, 2), mod_s, w_in_bf, tb(cos_s), tb(sin_s),
        state_conv.transpose(1, 0, 2), state_rglru, rg)
    bt = lambda t: t.transpose(1, 0, 2)
    yb_s, kwin_s, vwin_s = _attn_sample(
        bt(q_s), bt(k_s), bt(v_s), cache_win_k.transpose(0, 2, 3, 1),
        cache_win_v.transpose(0, 2, 3, 1))
    mod_spec_s = lambda j: pl.BlockSpec((bs, D_MODEL), lambda i: (0, j))
    y_s = _channel_mixing(
        ya_s.reshape(ts * bs, D_A), bt(yb_s).reshape(ts * bs, D_B),
        x_sample.transpose(1, 0, 2).reshape(ts * bs, D_MODEL),
        (mod_s, mod_spec_s(2), mod_spec_s(3), mod_spec_s(4)), mod_spec_s(5),
        mix_weights, bs, None)

    heads = lambda t: t.reshape(t.shape[0], t.shape[1], N_HEADS_B, HEAD_DIM)
    return (y_p.reshape(bp, sp, D_MODEL), bt(y_s.reshape(ts, bs, D_MODEL)),
            conv_p, h_p.reshape(bp, D_A), heads(kwin_p), heads(vwin_p),
            bt(conv_s), h_s, kwin_s.transpose(0, 3, 1, 2), vwin_s.transpose(0, 3, 1, 2))
```

```python
import functools
import math

import jax
import jax.numpy as jnp
from jax import lax
from jax.experimental import pallas as pl
from jax.experimental.pallas import tpu as pltpu

F32 = jnp.float32
BF16 = jnp.bfloat16

D_MODEL = 1024
D_A = 512
N_BLK_A = 8
BLK_W_A = D_A // N_BLK_A
CONV_W = 4
RG_C = 8.0
D_B = 512
HEAD_DIM = 64
N_HEADS_B = D_B // HEAD_DIM
DILATED_CFG = ((128, 1), (512, 4), (2048, 16))
WIN_MAX = 2048
N_KEYS = 128
ROPE_THETA = 10000.0
PAST_LEN = 8192
N_GROUPS = 4
N_EXP_PER_GROUP = 4
N_EXPERTS = N_GROUPS * N_EXP_PER_GROUP
D_EXPERT = 512
DN_ALPHA = 2.0 ** 0.25
LN_EPS = 1e-5
NEG_INF = -1e30

N_PAIRS = 6
N_CLASSES = N_GROUPS * N_PAIRS
LANES = 128
SUBLANES = 8
HEAD_PAIR_W = 2 * HEAD_DIM
N_HEAD_PAIRS = N_HEADS_B // 2
Q_BLK = 128
ATTN_UNROLL = 32
DMA_UNROLL = 8
INPROJ_PARTS = 1
ROUTE_W = LANES
TOKEN_ROWS = D_MODEL // LANES
TOKEN_PITCH = TOKEN_ROWS + 1
N_BUF = 3
VMEM_LIMIT = 56 * 1024 * 1024


def _cparams(sem):
    return pltpu.CompilerParams(dimension_semantics=sem, vmem_limit_bytes=VMEM_LIMIT)


def _mod_kernel(c_ref, w_ref, b_ref, o_ref):
    c = c_ref[...]
    s = (c * jax.nn.sigmoid(c)).astype(BF16)
    o_ref[...] = jnp.dot(s, w_ref[...].astype(BF16), preferred_element_type=F32) + b_ref[...]


def _modulation(c_all, w_ada, b_ada):
    n = c_all.shape[0]
    tn = 1024
    return pl.pallas_call(
        _mod_kernel,
        grid=(6 * D_MODEL // tn,),
        in_specs=[pl.BlockSpec((n, D_MODEL), lambda j: (0, 0)),
                  pl.BlockSpec((D_MODEL, tn), lambda j: (0, j)),
                  pl.BlockSpec((1, tn), lambda j: (0, j))],
        out_specs=pl.BlockSpec((n, tn), lambda j: (0, j)),
        out_shape=jax.ShapeDtypeStruct((n, 6 * D_MODEL), F32),
        compiler_params=_cparams(("arbitrary",)),
        name="adaln_mod",
    )(c_all, w_ada, b_ada.reshape(1, -1))


def _rope_apply(t, cos, sin_signed):
    lane = lax.broadcasted_iota(jnp.int32, t.shape, t.ndim - 1)
    first_half = (lane & (HEAD_DIM - 1)) < HEAD_DIM // 2
    width = t.shape[-1]
    swapped = jnp.where(first_half,
                        pltpu.roll(t, width - HEAD_DIM // 2, t.ndim - 1),
                        pltpu.roll(t, HEAD_DIM // 2, t.ndim - 1))
    return t * cos + swapped * sin_signed


def _rglru_gates(xc, wg_ref, b_a, b_x, lam):
    half = D_A // 2
    xcb = xc.astype(BF16)
    g0 = jnp.dot(xcb[:, :half], wg_ref[0], preferred_element_type=F32)
    g1 = jnp.dot(xcb[:, half:], wg_ref[1], preferred_element_type=F32)
    r = jax.nn.sigmoid(jnp.concatenate([g0[:, :half], g1[:, :half]], axis=1) + b_a)
    i = jax.nn.sigmoid(jnp.concatenate([g0[:, half:], g1[:, half:]], axis=1) + b_x)
    z = -lam
    softplus = jnp.maximum(z, 0.0) + jnp.log1p(jnp.exp(-jnp.abs(z)))
    log_a = -RG_C * r * softplus
    a = jnp.exp(log_a)
    one_minus_a2 = -jnp.tanh(log_a) * (a * a + 1.0)
    u = jnp.sqrt(one_minus_a2) * (i * xc)
    return a, u


def _store_token_tiles(ref, x, pitch=TOKEN_ROWS):
    n = x.shape[0]
    for c in range(TOKEN_ROWS):
        ref[pl.ds(c, n, stride=pitch), :] = x[:, c * LANES:(c + 1) * LANES]


def _load_token_tiles(ref, n, pitch=TOKEN_ROWS):
    return jnp.concatenate(
        [ref[pl.ds(c, n, stride=pitch), :] for c in range(TOKEN_ROWS)], axis=1)


def _rms_norm(y, g):
    return y * lax.rsqrt(jnp.mean(y * y, axis=-1, keepdims=True) + LN_EPS) * g


def _layer_norm(x, g, b):
    mu = jnp.mean(x, axis=-1, keepdims=True)
    xc = x - mu
    var = jnp.mean(xc * xc, axis=-1, keepdims=True)
    return xc * lax.rsqrt(var + LN_EPS) * g + b


def _inproj_prompt_kernel(x_ref, mod_ref, w_in_ref, cos_ref, sin_ref, conv_w_ref, conv_b_ref,
                          wg_ref, b_a_ref, b_x_ref, lam_ref, gna_ref,
                          ya_ref, q_ref, k_ref, v_ref, conv_out_ref, h_out_ref,
                          xp_buf, h_carry):
    t = pl.program_id(1)
    rows = x_ref.shape[0]
    pad = SUBLANES

    @pl.when(t == 0)
    def _():
        xp_buf[0:pad, :] = jnp.zeros((pad, D_A), F32)
        h_carry[...] = jnp.zeros_like(h_carry)

    row = lax.broadcasted_iota(jnp.int32, (SUBLANES, D_A), 0)

    def part(r0, n, h_prev):
        sl = slice(r0, r0 + n)
        u = (x_ref[sl, :] * (1.0 + mod_ref[1:2, :]) + mod_ref[0:1, :]).astype(BF16)

        def proj(j):
            return jnp.dot(u, w_in_ref[:, j * D_A:(j + 1) * D_A], preferred_element_type=F32)

        cos = jnp.concatenate([cos_ref[sl, :]] * N_HEAD_PAIRS, axis=1)
        sin = jnp.concatenate([sin_ref[sl, :]] * N_HEAD_PAIRS, axis=1)
        q_ref[sl, :] = _rope_apply(proj(2), cos, sin)
        k_ref[sl, :] = _rope_apply(proj(3), cos, sin)
        v_ref[sl, :] = proj(4)

        xa = proj(0)
        xp_buf[pad + r0:pad + r0 + n, :] = xa
        xc = conv_b_ref[...] + xa * conv_w_ref[CONV_W - 1:CONV_W, :]
        for j in range(CONV_W - 1):
            off = pad - (CONV_W - 1) + j + r0
            xc = xc + xp_buf[off:off + n, :] * conv_w_ref[j:j + 1, :]

        a, u_in = _rglru_gates(xc, wg_ref, b_a_ref[...], b_x_ref[...], lam_ref[...])

        hs = []
        for g in range(n // SUBLANES):
            ag = a[g * SUBLANES:(g + 1) * SUBLANES]
            ug = u_in[g * SUBLANES:(g + 1) * SUBLANES]
            for sh in (1, 2, 4):
                keep = row >= sh
                a_sh = pltpu.roll(ag, sh, 0)
                u_sh = pltpu.roll(ug, sh, 0)
                ug = jnp.where(keep, ag * u_sh + ug, ug)
                ag = jnp.where(keep, ag * a_sh, ag)
            hg = ag * h_prev + ug
            hs.append(hg)
            h_prev = hg[SUBLANES - 1:SUBLANES, :]

        y = jnp.concatenate(hs, axis=0) * jax.nn.gelu(proj(1))
        ya_ref[sl, :] = _rms_norm(y, gna_ref[...]).astype(BF16)
        return h_prev

    h_last = h_carry[...]
    n_part = rows // INPROJ_PARTS
    for p in range(INPROJ_PARTS):
        h_last = part(p * n_part, n_part, h_last)
    h_carry[...] = h_last
    h_out_ref[...] = h_last

    tail = xp_buf[rows + pad - (CONV_W - 1):rows + pad, :]
    conv_out_ref[...] = tail
    xp_buf[pad - (CONV_W - 1):pad, :] = tail


def _inproj_prompt(x, mod3, w_in_bf, cos_t, sin_t, rg):
    b, s, _ = x.shape
    tile = 512
    nt = s // tile
    row_spec = lambda w: pl.BlockSpec((None, tile, w), lambda i, j: (i, j, 0))
    vec = lambda r, w: pl.BlockSpec((r, w), lambda i, j: (0, 0))
    outs = pl.pallas_call(
        _inproj_prompt_kernel,
        grid=(b, nt),
        in_specs=[row_spec(D_MODEL),
                  pl.BlockSpec((None, 6, D_MODEL), lambda i, j: (i, 0, 0)),
                  vec(D_MODEL, 5 * D_A),
                  pl.BlockSpec((tile, HEAD_PAIR_W), lambda i, j: (j, 0)),
                  pl.BlockSpec((tile, HEAD_PAIR_W), lambda i, j: (j, 0)),
                  vec(CONV_W, D_A), vec(1, D_A),
                  pl.BlockSpec((2, D_A // 2, D_A), lambda i, j: (0, 0, 0)),
                  vec(1, D_A), vec(1, D_A), vec(1, D_A), vec(1, D_A)],
        out_specs=[row_spec(D_A), row_spec(D_B), row_spec(D_B), row_spec(D_B),
                   pl.BlockSpec((None, CONV_W - 1, D_A), lambda i, j: (i, 0, 0)),
                   pl.BlockSpec((None, 1, D_A), lambda i, j: (i, 0, 0))],
        out_shape=[jax.ShapeDtypeStruct((b, s, D_A), BF16),
                   jax.ShapeDtypeStruct((b, s, D_B), F32),
                   jax.ShapeDtypeStruct((b, s, D_B), F32),
                   jax.ShapeDtypeStruct((b, s, D_B), F32),
                   jax.ShapeDtypeStruct((b, CONV_W - 1, D_A), F32),
                   jax.ShapeDtypeStruct((b, 1, D_A), F32)],
        scratch_shapes=[pltpu.VMEM((tile + SUBLANES, D_A), F32),
                        pltpu.VMEM((1, D_A), F32)],
        compiler_params=_cparams(("arbitrary", "arbitrary")),
        name="inproj_prompt",
    )(x, mod3, w_in_bf, cos_t, sin_t, rg["conv_w"], rg["conv_b"], rg["w_gate"],
      rg["b_a"], rg["b_x"], rg["lam"], rg["g_norm_a"])
    return outs


def _attn_prompt_kernel(q_ref, k_ref, v_ref, o_ref, kwin_ref, vwin_ref, acc_s, m_s, l_s, bias_s):
    s = q_ref.shape[0]
    keep = kwin_ref.shape[0]
    kwin_ref[...] = k_ref[s - keep:s, :]
    vwin_ref[...] = v_ref[s - keep:s, :]

    lane = lax.broadcasted_iota(jnp.int32, (Q_BLK, HEAD_PAIR_W), 1)
    head0 = lane < HEAD_DIM
    nk = 2 * Q_BLK

    qi = lax.broadcasted_iota(jnp.int32, (Q_BLK, nk), 0)
    ki = lax.broadcasted_iota(jnp.int32, (Q_BLK, nk), 1)
    for slot in range(2):
        dist = slot * Q_BLK + qi - ki
        bias_s[slot] = jnp.where((dist >= 0) & (dist <= N_KEYS), 0.0, NEG_INF)

    def rows(start, n, d):
        return pl.ds(start, n) if d == 1 else pl.ds(start, n, stride=d)

    def unit(bi, d, nb, u):
        r = u // nb
        j = u % nb
        jk = jnp.maximum(j - 1, 0)
        start_q = r + d * Q_BLK * j
        start_k = r + d * Q_BLK * jk
        bias = bias_s[j - jk]
        qb = q_ref[rows(start_q, Q_BLK, d), :] * (HEAD_DIM ** -0.5)
        kb = k_ref[rows(start_k, nk, d), :].astype(BF16)
        vb = v_ref[rows(start_k, nk, d), :].astype(BF16)
        q2 = jnp.concatenate([jnp.where(head0, qb, 0.0), jnp.where(head0, 0.0, qb)],
                             axis=0).astype(BF16)
        sc = lax.dot_general(q2, kb, (((1,), (1,)), ((), ())), preferred_element_type=F32)
        sc = sc + jnp.concatenate([bias, bias], axis=0)
        m = jnp.max(sc, axis=-1, keepdims=True)
        p = jnp.exp(sc - m)
        l = jnp.sum(p, axis=-1, keepdims=True)
        o = jnp.dot(p.astype(BF16), vb, preferred_element_type=F32)
        dst = rows(start_q, Q_BLK, d)
        acc_s[bi, dst, :] = jnp.where(head0, o[0:Q_BLK], o[Q_BLK:])
        m_s[bi, dst, :] = jnp.where(head0, m[0:Q_BLK], m[Q_BLK:])
        l_s[bi, dst, :] = jnp.where(head0, l[0:Q_BLK], l[Q_BLK:])

    for bi, (_, d) in enumerate(DILATED_CFG):
        nb = s // d // Q_BLK

        def body(u, carry, bi=bi, d=d, nb=nb):
            unit(bi, d, nb, u)
            return carry

        lax.fori_loop(0, d * nb, body, 0, unroll=ATTN_UNROLL)

    chunk = 512

    def merge(i, carry):
        r0 = pl.multiple_of(i * chunk, chunk)
        sl = pl.ds(r0, chunk)
        ms = [m_s[bi, sl, :] for bi in range(len(DILATED_CFG))]
        mx = functools.reduce(jnp.maximum, ms)
        num = jnp.zeros((chunk, HEAD_PAIR_W), F32)
        den = jnp.zeros((chunk, HEAD_PAIR_W), F32)
        for bi in range(len(DILATED_CFG)):
            w = jnp.exp(ms[bi] - mx)
            num = num + w * acc_s[bi, sl, :]
            den = den + w * l_s[bi, sl, :]
        o_ref[sl, :] = num / den
        return carry

    lax.fori_loop(0, s // chunk, merge, 0)


def _attn_prompt(q, k, v):
    b, s, _ = q.shape
    keep = min(WIN_MAX, s)
    nbr = len(DILATED_CFG)
    spec = pl.BlockSpec((None, s, HEAD_PAIR_W), lambda i, j: (i, 0, j))
    wspec = pl.BlockSpec((None, keep, HEAD_PAIR_W), lambda i, j: (i, 0, j))
    return pl.pallas_call(
        _attn_prompt_kernel,
        grid=(b, N_HEAD_PAIRS),
        in_specs=[spec, spec, spec],
        out_specs=[spec, wspec, wspec],
        out_shape=[jax.ShapeDtypeStruct((b, s, D_B), F32),
                   jax.ShapeDtypeStruct((b, keep, D_B), F32),
                   jax.ShapeDtypeStruct((b, keep, D_B), F32)],
        scratch_shapes=[pltpu.VMEM((nbr, s, HEAD_PAIR_W), F32),
                        pltpu.VMEM((nbr, s, HEAD_PAIR_W), F32),
                        pltpu.VMEM((nbr, s, HEAD_PAIR_W), F32),
                        pltpu.VMEM((2, Q_BLK, 2 * Q_BLK), F32)],
        compiler_params=_cparams(("arbitrary", "arbitrary")),
        name="attn_prompt",
    )(q, k, v)


def _inproj_sample_kernel(x_ref, mod_ref, w_in_ref, cos_ref, sin_ref, conv_state_ref, h0_ref,
                          conv_w_ref, conv_b_ref, wg_ref, b_a_ref, b_x_ref, lam_ref, gna_ref,
                          ya_ref, q_ref, k_ref, v_ref, conv_out_ref, h_out_ref):
    nt, nb, _ = x_ref.shape
    sh1 = mod_ref[:, 0:D_MODEL]
    sc1 = mod_ref[:, D_MODEL:2 * D_MODEL]
    u = (x_ref[...] * (1.0 + sc1)[None] + sh1[None]).astype(BF16).reshape(nt * nb, D_MODEL)

    def proj(j):
        return jnp.dot(u, w_in_ref[:, j * D_A:(j + 1) * D_A], preferred_element_type=F32)

    cos = cos_ref[...].reshape(nt * nb, D_B)
    sin = sin_ref[...].reshape(nt * nb, D_B)
    q_ref[...] = _rope_apply(proj(2), cos, sin).reshape(nt, nb, D_B)
    k_ref[...] = _rope_apply(proj(3), cos, sin).reshape(nt, nb, D_B)
    v_ref[...] = proj(4).reshape(nt, nb, D_B)

    xa = proj(0).reshape(nt, nb, D_A)
    xp = [conv_state_ref[j] for j in range(CONV_W - 1)] + [xa[t] for t in range(nt)]
    xc = jnp.concatenate(
        [conv_b_ref[...] + sum(xp[t + j] * conv_w_ref[j:j + 1, :] for j in range(CONV_W))
         for t in range(nt)], axis=0)
    for j in range(CONV_W - 1):
        conv_out_ref[j] = xp[nt + j]

    a, u_in = _rglru_gates(xc, wg_ref, b_a_ref[...], b_x_ref[...], lam_ref[...])
    h = h0_ref[...]
    hs = []
    for t in range(nt):
        h = a[t * nb:(t + 1) * nb] * h + u_in[t * nb:(t + 1) * nb]
        hs.append(h)
    h_out_ref[...] = h
    y = jnp.concatenate(hs, axis=0) * jax.nn.gelu(proj(1))
    ya_ref[...] = _rms_norm(y, gna_ref[...]).astype(BF16).reshape(nt, nb, D_A)


def _inproj_sample(x_tb, mod_s, w_in_bf, cos_t, sin_t, conv_state_tb, h0, rg):
    nt, nb, _ = x_tb.shape
    full = lambda shape: pl.BlockSpec(shape, lambda i: (0,) * len(shape))
    return pl.pallas_call(
        _inproj_sample_kernel,
        grid=(1,),
        in_specs=[full((nt, nb, D_MODEL)), full((nb, 6 * D_MODEL)), full((D_MODEL, 5 * D_A)),
                  full((nt, nb, D_B)), full((nt, nb, D_B)),
                  full((CONV_W - 1, nb, D_A)), full((nb, D_A)),
                  full((CONV_W, D_A)), full((1, D_A)), full((2, D_A // 2, D_A)),
                  full((1, D_A)), full((1, D_A)), full((1, D_A)), full((1, D_A))],
        out_specs=[full((nt, nb, D_A)), full((nt, nb, D_B)), full((nt, nb, D_B)),
                   full((nt, nb, D_B)), full((CONV_W - 1, nb, D_A)), full((nb, D_A))],
        out_shape=[jax.ShapeDtypeStruct((nt, nb, D_A), BF16),
                   jax.ShapeDtypeStruct((nt, nb, D_B), F32),
                   jax.ShapeDtypeStruct((nt, nb, D_B), F32),
                   jax.ShapeDtypeStruct((nt, nb, D_B), F32),
                   jax.ShapeDtypeStruct((CONV_W - 1, nb, D_A), F32),
                   jax.ShapeDtypeStruct((nb, D_A), F32)],
        compiler_params=_cparams(("arbitrary",)),
        name="inproj_sample",
    )(x_tb, mod_s, w_in_bf, cos_t, sin_t, conv_state_tb, h0, rg["conv_w"], rg["conv_b"],
      rg["w_gate"], rg["b_a"], rg["b_x"], rg["lam"], rg["g_norm_a"])


def _sample_attention_step(b, n_seq, q_ref, kn_ref, vn_ref, ck_hbm, cv_hbm, o_ref, kwin_hbm,
                           vwin_hbm, cin, cout, sem_in, sem_out, kn_pad, vn_pad):
    nt = q_ref.shape[0]
    n_buf = cin.shape[-1]
    n_rows = N_HEADS_B * nt
    srcs = (ck_hbm, cv_hbm)
    dsts = (kwin_hbm, vwin_hbm)

    def copy_in(which, seq):
        return pltpu.make_async_copy(srcs[which].at[seq], cin.at[which], sem_in.at[which])

    def copy_out(which, seq):
        return pltpu.make_async_copy(cout.at[which], dsts[which].at[seq], sem_out.at[which])

    @pl.when(b == 0)
    def _():
        kn_pad[...] = jnp.zeros_like(kn_pad)
        vn_pad[...] = jnp.zeros_like(vn_pad)
        copy_in(0, 0).start()
        copy_in(1, 0).start()

    kn_pad[0:nt, :] = kn_ref[...]
    vn_pad[0:nt, :] = vn_ref[...]

    tail_lane = lax.broadcasted_iota(jnp.int32, (D_B, LANES), 1)

    def shift_in(old_t, new_pad, out_ref):
        rolled = pltpu.roll(old_t, n_buf - nt, 1)
        new_t = pltpu.roll(new_pad.T, LANES - nt, 1)
        last = jnp.where(tail_lane < LANES - nt, rolled[:, n_buf - LANES:n_buf], new_t)
        out_ref[:, :, 0:n_buf - LANES] = rolled[:, 0:n_buf - LANES].reshape(
            N_HEADS_B, HEAD_DIM, n_buf - LANES)
        out_ref[:, :, n_buf - LANES:n_buf] = last.reshape(N_HEADS_B, HEAD_DIM, LANES)

    def stream(which, new_pad, use):
        copy_in(which, b).wait()
        result = use(cin[which].reshape(D_B, n_buf))

        @pl.when(b >= 1)
        def _():
            copy_out(which, b - 1).wait()

        shift_in(cin[which].reshape(D_B, n_buf), new_pad, cout.at[which])
        copy_out(which, b).start()

        @pl.when(b + 1 < n_seq)
        def _():
            copy_in(which, b + 1).start()

        return result

    ri = lax.broadcasted_iota(jnp.int32, (n_rows, nt), 0)
    ci = lax.broadcasted_iota(jnp.int32, (n_rows, nt), 1)
    pick = (ri % nt == ci).astype(BF16)
    qs = (q_ref[...] * (HEAD_DIM ** -0.5)).astype(BF16)
    q_rep = jnp.dot(pick, qs, preferred_element_type=F32)
    row_h = lax.broadcasted_iota(jnp.int32, (n_rows, D_B), 0) // nt
    lane_h = lax.broadcasted_iota(jnp.int32, (n_rows, D_B), 1) // HEAD_DIM
    own = row_h == lane_h
    qbd = jnp.where(own, q_rep, 0.0).astype(BF16)

    nt_dims = (((1,), (1,)), ((), ()))

    def mult(dist, limit_ok):
        c = jnp.zeros(dist.shape, F32)
        for win, d in DILATED_CFG:
            hit = (dist >= 0) & (dist <= win) & (dist % d == 0) & limit_ok
            c = c + hit.astype(F32)
        return c

    def weights(ck_t):
        sc_c = jnp.dot(qbd, ck_t.astype(BF16), preferred_element_type=F32)
        sc_n = lax.dot_general(qbd, kn_pad[...].astype(BF16), nt_dims,
                               preferred_element_type=F32)
        t_c = lax.broadcasted_iota(jnp.int32, sc_c.shape, 0) % nt
        dist_c = n_buf + t_c - lax.broadcasted_iota(jnp.int32, sc_c.shape, 1)
        mult_c = mult(dist_c, dist_c >= 0)
        t_n = lax.broadcasted_iota(jnp.int32, sc_n.shape, 0) % nt
        col_n = lax.broadcasted_iota(jnp.int32, sc_n.shape, 1)
        mult_n = mult(t_n - col_n, col_n < nt)
        sc_c = jnp.where(mult_c > 0, sc_c, NEG_INF)
        sc_n = jnp.where(mult_n > 0, sc_n, NEG_INF)
        m = jnp.maximum(jnp.max(sc_c, axis=-1, keepdims=True),
                        jnp.max(sc_n, axis=-1, keepdims=True))
        p_c = mult_c * jnp.exp(sc_c - m)
        p_n = mult_n * jnp.exp(sc_n - m)
        l = jnp.sum(p_c, axis=-1, keepdims=True) + jnp.sum(p_n, axis=-1, keepdims=True)
        return p_c.astype(BF16), p_n.astype(BF16), l

    p_c, p_n, l = stream(0, kn_pad[...], weights)

    def values(cv_t):
        return (lax.dot_general(p_c, cv_t.astype(BF16), nt_dims, preferred_element_type=F32)
                + jnp.dot(p_n, vn_pad[...].astype(BF16), preferred_element_type=F32))

    acc = stream(1, vn_pad[...], values)
    o_full = jnp.where(own, acc / l, 0.0)
    out = o_full[0:nt, :]
    for h in range(1, N_HEADS_B):
        out = out + o_full[h * nt:(h + 1) * nt, :]
    o_ref[...] = out

    @pl.when(b == n_seq - 1)
    def _():
        copy_out(0, b).wait()
        copy_out(1, b).wait()


def _split_bf16(x):
    hi = x.astype(BF16)
    lo = (x - hi.astype(F32)).astype(BF16)
    return hi, lo


def _first_argmax(vals):
    mx = functools.reduce(jnp.maximum, vals)
    idx = jnp.full(mx.shape, float(len(vals) - 1), F32)
    for j in range(len(vals) - 2, -1, -1):
        idx = jnp.where(vals[j] == mx, float(j), idx)
    return mx, idx


def _outproj_kernel(ya_ref, yb_ref, x_ref, gt1_ref, sh2_ref, sc2_ref, w_out_ref, gnb_ref,
                    ln_g_ref, ln_b_ref, w_r_ref, b_r_ref, x1_ref, u2_ref, route_ref):
    yb = _rms_norm(yb_ref[...], gnb_ref[...]).astype(BF16)
    mixed = (jnp.dot(ya_ref[...], w_out_ref[0:D_A, :], preferred_element_type=F32)
             + jnp.dot(yb, w_out_ref[D_A:D_A + D_B, :], preferred_element_type=F32))
    x1 = _layer_norm(DN_ALPHA * x_ref[...] + gt1_ref[...] * mixed, ln_g_ref[...], ln_b_ref[...])
    x1_ref[...] = x1
    u2 = x1 * (1.0 + sc2_ref[...]) + sh2_ref[...]
    _store_token_tiles(u2_ref, u2, TOKEN_PITCH)

    u_hi, u_lo = _split_bf16(u2)
    w_hi, w_lo = _split_bf16(w_r_ref[...])
    both = jnp.dot(u_hi, jnp.concatenate([w_hi, w_lo], axis=1), preferred_element_type=F32)
    logits = (both[:, 0:ROUTE_W] + jnp.dot(u_lo, w_hi, preferred_element_type=F32)
              + both[:, ROUTE_W:2 * ROUTE_W]) + b_r_ref[...]
    lt = logits.T

    g_rows = [lt[j:j + 1, :] for j in range(N_GROUPS)]
    g_max, g_idx = _first_argmax(g_rows)
    p_group = 1.0 / sum(jnp.exp(g - g_max) for g in g_rows)
    e_rows = []
    for e in range(N_EXP_PER_GROUP):
        acc = jnp.zeros_like(g_max)
        for g in range(N_GROUPS):
            r = N_GROUPS + g * N_EXP_PER_GROUP + e
            acc = jnp.where(g_idx == float(g), lt[r:r + 1, :], acc)
        e_rows.append(acc)
    v1, i1 = _first_argmax(e_rows)
    rest = [jnp.where(i1 == float(e), -jnp.inf, e_rows[e]) for e in range(N_EXP_PER_GROUP)]
    v2, i2 = _first_argmax(rest)
    ex = jnp.exp(v2 - v1)
    w1 = p_group / (1.0 + ex)
    w2 = p_group * ex / (1.0 + ex)
    lo = jnp.minimum(i1, i2)
    hi = jnp.maximum(i1, i2)
    pair = jnp.where(lo == 0.0, hi - 1.0, jnp.where(lo == 1.0, hi + 1.0, 5.0))
    cls = g_idx * float(N_PAIRS) + pair
    w_of_lo = jnp.where(i1 < i2, w1, w2)
    w_of_hi = jnp.where(i1 < i2, w2, w1)
    e_lo = g_idx * float(N_EXP_PER_GROUP) + lo
    e_hi = g_idx * float(N_EXP_PER_GROUP) + hi
    n_tok = cls.shape[1]
    route = jnp.concatenate(
        [cls, w_of_lo, w_of_hi, e_lo, e_hi, jnp.zeros((ROUTE_W - 5, n_tok), F32)], axis=0)
    route_ref[...] = route[0:SUBLANES, :]
    u2_ref[pl.ds(TOKEN_ROWS, n_tok, stride=TOKEN_PITCH), :] = route.T


def _outproj(ya, yb, x, mods, w_out_bf, gnb, ln_g, ln_b, w_r, b_r, tile):
    n = x.shape[0]
    nt = n // tile
    mod_arr, gt1_spec, sh2_spec, sc2_spec = mods
    row = lambda w: pl.BlockSpec((tile, w), lambda i: (i, 0))
    vec = lambda r, w: pl.BlockSpec((r, w), lambda i: (0, 0))
    return pl.pallas_call(
        _outproj_kernel,
        grid=(nt,),
        in_specs=[row(D_A), row(D_B), row(D_MODEL), gt1_spec, sh2_spec, sc2_spec,
                  vec(D_MODEL, D_MODEL), vec(1, D_B), vec(1, D_MODEL), vec(1, D_MODEL),
                  vec(D_MODEL, ROUTE_W), vec(1, ROUTE_W)],
        out_specs=[row(D_MODEL),
                   pl.BlockSpec((tile * TOKEN_PITCH, LANES), lambda i: (i, 0)),
                   pl.BlockSpec((None, SUBLANES, tile), lambda i: (i, 0, 0))],
        out_shape=[jax.ShapeDtypeStruct((n, D_MODEL), F32),
                   jax.ShapeDtypeStruct((n * TOKEN_PITCH, LANES), F32),
                   jax.ShapeDtypeStruct((nt, SUBLANES, tile), F32)],
        compiler_params=_cparams(("arbitrary",)),
        name="outproj_router",
    )(ya, yb, x, mod_arr, mod_arr, mod_arr, w_out_bf, gnb, ln_g, ln_b, w_r, b_r)


def _moe_kernel(e_lo_ref, e_hi_ref, n_used_ref, src_ref, dst_ref,
                x_hbm, wg_lo, wg_hi, wu_lo, wu_hi, wd_lo, wd_hi,
                q_ref, kn_ref, vn_ref, ck_hbm, cv_hbm,
                o_hbm, yb_ref, kwin_hbm, vwin_hbm,
                xg, og, gsem, ssem, cin, cout, csem_in, csem_out, kn_pad, vn_pad,
                *, n_tokens, tile, n_seq):
    i = pl.program_id(0)
    n_steps = pl.num_programs(0)
    n_used = n_used_ref[0]
    cur = i % N_BUF
    ahead = (i + 2) % N_BUF
    rows = tile * TOKEN_ROWS

    def gather_token(base, r, buf):
        pltpu.make_async_copy(x_hbm.at[pl.ds(src_ref[base + r], TOKEN_PITCH)],
                              xg.at[buf, pl.ds(r * TOKEN_PITCH, TOKEN_PITCH)],
                              gsem.at[buf]).start()

    def gather_wait(buf):
        pltpu.make_async_copy(x_hbm.at[pl.ds(0, tile * TOKEN_PITCH)], xg.at[buf],
                              gsem.at[buf]).wait()

    def scatter_wait(buf):
        pltpu.make_async_copy(og.at[buf], o_hbm.at[pl.ds(0, rows)], ssem.at[buf]).wait()

    @pl.when(i == 0)
    def _():
        def first(r, c):
            gather_token(0, r, 0)
            gather_token(jnp.minimum(1, n_steps - 1) * tile, r, 1)
            return c
        lax.fori_loop(0, tile, first, 0, unroll=DMA_UNROLL)
        og[...] = jnp.zeros_like(og)
        for buf in range(N_BUF):
            pad_rows = pltpu.make_async_copy(
                og.at[buf], o_hbm.at[pl.ds((n_tokens + buf * tile) * TOKEN_ROWS, rows)],
                ssem.at[buf])
            pad_rows.start()
            pad_rows.wait()

    @pl.when(i <= n_used)
    def _():
        gather_wait(cur)

        @pl.when(i >= 2)
        def _():
            scatter_wait(cur)

        nxt = jnp.minimum(i + 2, n_steps - 1) * tile
        for r in range(tile):
            gather_token(nxt, r, ahead)
        prev = i * tile
        for r in range(tile):
            row = pl.multiple_of(dst_ref[prev + r], TOKEN_ROWS)
            pltpu.make_async_copy(og.at[ahead, pl.ds(r * TOKEN_ROWS, TOKEN_ROWS)],
                                  o_hbm.at[pl.ds(row, TOKEN_ROWS)], ssem.at[ahead]).start()

        xb = _load_token_tiles(xg.at[cur], tile, TOKEN_PITCH).astype(BF16)
        w2 = xg[cur, pl.ds(TOKEN_ROWS, tile, stride=TOKEN_PITCH), :]
        out = jnp.zeros((tile, D_MODEL), F32)
        for col, wg, wu, wd in ((1, wg_lo, wu_lo, wd_lo), (2, wg_hi, wu_hi, wd_hi)):
            hg = jnp.dot(xb, wg[...], preferred_element_type=F32)
            hu = jnp.dot(xb, wu[...], preferred_element_type=F32)
            act = (hg * jax.nn.sigmoid(hg)) * hu * w2[:, col:col + 1]
            out = out + jnp.dot(act.astype(BF16), wd[...], preferred_element_type=F32)
        _store_token_tiles(og.at[cur], out)

        @pl.when(i == n_used)
        def _():
            for buf in range(N_BUF):
                @pl.when(buf != cur)
                def _():
                    gather_wait(buf)
                    scatter_wait(buf)

    @pl.when(i < n_seq)
    def _():
        _sample_attention_step(i, n_seq, q_ref, kn_ref, vn_ref, ck_hbm, cv_hbm, yb_ref,
                               kwin_hbm, vwin_hbm, cin, cout, csem_in, csem_out, kn_pad, vn_pad)


def _moe(u2t, cls, w_gate_bf, w_up_bf, w_down_bf, tile, q_s, k_new, v_new, cache_k_t, cache_v_t):
    n = cls.shape[0]
    n_seq, nt, _ = q_s.shape
    n_buf = cache_k_t.shape[-1]
    n_steps = n // tile + N_CLASSES
    assert n_steps >= n_seq, "one sample sequence per grid step"
    cls = cls.astype(jnp.int32)
    order = jnp.argsort(cls, stable=True).astype(jnp.int32)
    class_ids = jnp.arange(N_CLASSES, dtype=jnp.int32)
    counts = jnp.sum((cls[:, None] == class_ids[None, :]).astype(jnp.int32), axis=0)
    tiles_per = (counts + tile - 1) // tile
    tile_end = jnp.cumsum(tiles_per)
    tile_off = tile_end - tiles_per
    n_used = tile_end[-1]
    class_start = jnp.cumsum(counts) - counts
    step = jnp.arange(n_steps, dtype=jnp.int32)
    step_c = jnp.minimum(step, n_used - 1)
    cls_of = jnp.sum((step_c[:, None] >= tile_end[None, :]).astype(jnp.int32), axis=1)
    onehot = (cls_of[:, None] == class_ids[None, :]).astype(jnp.int32)
    pick = lambda table: jnp.sum(onehot * table[None, :], axis=1)
    local = step - pick(tile_off)
    nvalid = jnp.where(step < n_used, jnp.clip(pick(counts) - local * tile, 0, tile), 0)
    r = jnp.arange(tile, dtype=jnp.int32)
    pos = pick(class_start)[:, None] + local[:, None] * tile + r[None, :]
    valid = r[None, :] < nvalid[:, None]
    tok = order[jnp.clip(pos, 0, n - 1)]
    src = (jnp.where(valid, tok, 0) * TOKEN_PITCH).astype(jnp.int32).reshape(-1)
    spare = n + (step[:, None] % N_BUF) * tile + r[None, :]
    dst = jnp.where(valid, tok, spare)
    dst = jnp.concatenate([(n + (N_BUF - 1) * tile + r)[None, :], dst], axis=0)
    dst = (dst * TOKEN_ROWS).astype(jnp.int32).reshape(-1)
    grp = cls_of // N_PAIRS
    pair = cls_of % N_PAIRS
    pair_lo = (pair >= 3).astype(jnp.int32) + (pair >= 5).astype(jnp.int32)
    pair_hi = pair + 1 - 2 * (pair >= 3).astype(jnp.int32) - (pair >= 5).astype(jnp.int32)
    e_lo = (grp * N_EXP_PER_GROUP + pair_lo).astype(jnp.int32)
    e_hi = (grp * N_EXP_PER_GROUP + pair_hi).astype(jnp.int32)

    w_in_spec = lambda which: pl.BlockSpec(
        (None, D_MODEL, D_EXPERT), lambda i, elo, ehi, nu, s, d: ((elo, ehi)[which][i], 0, 0))
    w_dn_spec = lambda which: pl.BlockSpec(
        (None, D_EXPERT, D_MODEL), lambda i, elo, ehi, nu, s, d: ((elo, ehi)[which][i], 0, 0))
    any_spec = pl.BlockSpec(memory_space=pl.ANY)
    seq_spec = pl.BlockSpec((None, nt, D_B),
                            lambda i, elo, ehi, nu, s, d: (jnp.minimum(i, n_seq - 1), 0, 0))
    win_shape = jax.ShapeDtypeStruct((n_seq, N_HEADS_B, HEAD_DIM, n_buf), F32)
    grid_spec = pltpu.PrefetchScalarGridSpec(
        num_scalar_prefetch=5,
        grid=(n_steps,),
        in_specs=[any_spec,
                  w_in_spec(0), w_in_spec(1), w_in_spec(0), w_in_spec(1),
                  w_dn_spec(0), w_dn_spec(1),
                  seq_spec, seq_spec, seq_spec, any_spec, any_spec],
        out_specs=[any_spec, seq_spec, any_spec, any_spec],
        scratch_shapes=[pltpu.VMEM((N_BUF, tile * TOKEN_PITCH, LANES), F32),
                        pltpu.VMEM((N_BUF, tile * TOKEN_ROWS, LANES), F32),
                        pltpu.SemaphoreType.DMA((N_BUF,)),
                        pltpu.SemaphoreType.DMA((N_BUF,)),
                        pltpu.VMEM((2, N_HEADS_B, HEAD_DIM, n_buf), F32),
                        pltpu.VMEM((2, N_HEADS_B, HEAD_DIM, n_buf), F32),
                        pltpu.SemaphoreType.DMA((2,)),
                        pltpu.SemaphoreType.DMA((2,)),
                        pltpu.VMEM((LANES, D_B), F32),
                        pltpu.VMEM((LANES, D_B), F32)],
    )
    return pl.pallas_call(
        functools.partial(_moe_kernel, n_tokens=n, tile=tile, n_seq=n_seq),
        grid_spec=grid_spec,
        out_shape=[jax.ShapeDtypeStruct(((n + N_BUF * tile) * TOKEN_ROWS, LANES), F32),
                   jax.ShapeDtypeStruct((n_seq, nt, D_B), F32), win_shape, win_shape],
        compiler_params=_cparams(("arbitrary",)),
        name="moe_sparse",
    )(e_lo, e_hi, n_used.reshape(1).astype(jnp.int32), src, dst,
      u2t, w_gate_bf, w_gate_bf, w_up_bf, w_up_bf, w_down_bf, w_down_bf,
      q_s, k_new, v_new, cache_k_t, cache_v_t)


def _moe_dense_kernel(x_ref, wg_ref, wu_ref, wd_ref, o_ref):
    e = pl.program_id(0)
    n = o_ref.shape[0]

    @pl.when(e == 0)
    def _():
        o_ref[...] = jnp.zeros_like(o_ref)

    xb = _load_token_tiles(x_ref, n, TOKEN_PITCH).astype(BF16)
    ef = e.astype(F32)
    route = x_ref[pl.ds(TOKEN_ROWS, n, stride=TOKEN_PITCH), :]
    col = lambda c: route[:, c:c + 1]
    comb = jnp.where(col(3) == ef, col(1), 0.0) + jnp.where(col(4) == ef, col(2), 0.0)
    hg = jnp.dot(xb, wg_ref[...], preferred_element_type=F32)
    hu = jnp.dot(xb, wu_ref[...], preferred_element_type=F32)
    act = (hg * jax.nn.sigmoid(hg)) * hu * comb
    o_ref[...] += jnp.dot(act.astype(BF16), wd_ref[...], preferred_element_type=F32)


def _moe_dense(u2t, w_gate_bf, w_up_bf, w_down_bf):
    n = u2t.shape[0] // TOKEN_PITCH
    return pl.pallas_call(
        _moe_dense_kernel,
        grid=(N_EXPERTS,),
        in_specs=[pl.BlockSpec((n * TOKEN_PITCH, LANES), lambda e: (0, 0)),
                  pl.BlockSpec((None, D_MODEL, D_EXPERT), lambda e: (e, 0, 0)),
                  pl.BlockSpec((None, D_MODEL, D_EXPERT), lambda e: (e, 0, 0)),
                  pl.BlockSpec((None, D_EXPERT, D_MODEL), lambda e: (e, 0, 0))],
        out_specs=pl.BlockSpec((n, D_MODEL), lambda e: (0, 0)),
        out_shape=jax.ShapeDtypeStruct((n, D_MODEL), F32),
        compiler_params=_cparams(("arbitrary",)),
        name="moe_dense",
    )(u2t, w_gate_bf, w_up_bf, w_down_bf)


def _final_kernel(x1_ref, ffn_ref, gt2_ref, g_ref, b_ref, o_ref, *, token_tiled):
    rows = x1_ref.shape[0]
    ffn = _load_token_tiles(ffn_ref, rows) if token_tiled else ffn_ref[...]
    o_ref[...] = _layer_norm(DN_ALPHA * x1_ref[...] + gt2_ref[...] * ffn, g_ref[...], b_ref[...])


def _final_norm(x1, ffn, mod_arr, gt2_spec, ln_g, ln_b, tile):
    n = x1.shape[0]
    token_tiled = ffn.shape[-1] == LANES
    row = pl.BlockSpec((tile, D_MODEL), lambda i: (i, 0))
    ffn_spec = pl.BlockSpec((tile * TOKEN_ROWS, LANES), lambda i: (i, 0)) if token_tiled else row
    vec = pl.BlockSpec((1, D_MODEL), lambda i: (0, 0))
    return pl.pallas_call(
        functools.partial(_final_kernel, token_tiled=token_tiled),
        grid=(n // tile,),
        in_specs=[row, ffn_spec, gt2_spec, vec, vec],
        out_specs=row,
        out_shape=jax.ShapeDtypeStruct((n, D_MODEL), F32),
        compiler_params=_cparams(("arbitrary",)),
        name="final_norm",
    )(x1, ffn, mod_arr, ln_g, ln_b)


def _rope_tables(pos):
    half = HEAD_DIM // 2
    inv = ROPE_THETA ** (-jnp.arange(half, dtype=F32) * 2.0 / HEAD_DIM)
    ang = pos.astype(F32)[:, None] * inv[None, :]
    cos = jnp.cos(ang)
    sin = jnp.sin(ang)
    cos_t = jnp.tile(jnp.concatenate([cos, cos], axis=-1), (1, 2))
    sin_t = jnp.tile(jnp.concatenate([-sin, sin], axis=-1), (1, 2))
    return cos_t, sin_t


def _block_diag(w):
    n, a, b = w.shape
    eye = jnp.eye(n, dtype=w.dtype)
    return (eye[:, None, :, None] * w[:, :, None, :]).reshape(n * a, n * b)


def _prepare_weights(w_in, conv_w, conv_b, w_rg_a, b_rg_a, w_rg_x, b_rg_x, rg_lambda, g_norm_a,
                     w_router_group, b_router_group, w_router_expert, b_router_expert):
    half_blocks = N_BLK_A // 2
    w_gate = jnp.stack([
        jnp.concatenate([_block_diag(w_rg_a[h * half_blocks:(h + 1) * half_blocks]),
                         _block_diag(w_rg_x[h * half_blocks:(h + 1) * half_blocks])], axis=1)
        for h in range(2)]).astype(BF16)
    rg = dict(conv_w=conv_w, conv_b=conv_b.reshape(1, D_A), w_gate=w_gate,
              b_a=b_rg_a.reshape(1, D_A), b_x=b_rg_x.reshape(1, D_A),
              lam=rg_lambda.reshape(1, D_A), g_norm_a=g_norm_a.reshape(1, D_A))
    n_logits = N_GROUPS + N_EXPERTS
    w_r = jnp.concatenate(
        [w_router_group,
         w_router_expert.transpose(1, 0, 2).reshape(D_MODEL, N_EXPERTS),
         jnp.zeros((D_MODEL, ROUTE_W - n_logits), F32)], axis=1)
    b_r = jnp.concatenate([b_router_group, b_router_expert.reshape(-1),
                           jnp.zeros((ROUTE_W - n_logits,), F32)]).reshape(1, ROUTE_W)
    return rg, w_r, b_r


def kernel(x_prompt, x_sample, state_conv, state_rglru, cache_win_k, cache_win_v, c_prompt, c_sample, w_ada, b_ada, w_in, conv_w, conv_b, w_rg_a, b_rg_a, w_rg_x, b_rg_x, rg_lambda, g_norm_a, g_norm_b, w_out, ln1_g, ln1_b, w_router_group, b_router_group, w_router_expert, b_router_expert, w_exp_gate, w_exp_up, w_exp_down, ln2_g, ln2_b):
    bp, sp, _ = x_prompt.shape
    bs, ts, _ = x_sample.shape
    n_buf = cache_win_k.shape[1]

    rg, w_r, b_r = _prepare_weights(w_in, conv_w, conv_b, w_rg_a, b_rg_a, w_rg_x, b_rg_x,
                                    rg_lambda, g_norm_a, w_router_group, b_router_group,
                                    w_router_expert, b_router_expert)
    w_in_bf = w_in.astype(BF16)
    outproj_weights = (w_out.astype(BF16), g_norm_b.reshape(1, D_B), ln1_g.reshape(1, D_MODEL),
                       ln1_b.reshape(1, D_MODEL), w_r, b_r)
    expert_weights = (w_exp_gate.astype(BF16), w_exp_up.astype(BF16), w_exp_down.astype(BF16))
    ln2 = (ln2_g.reshape(1, D_MODEL), ln2_b.reshape(1, D_MODEL))

    mod = _modulation(jnp.concatenate([c_prompt, c_sample], axis=0), w_ada, b_ada)
    mod_p, mod_s = mod[:bp], mod[bp:]

    tile_p = 512
    cos_p, sin_p = _rope_tables(jnp.arange(sp))
    ya_p, q_p, k_p, v_p, conv_p, h_p = _inproj_prompt(
        x_prompt, mod_p.reshape(bp, 6, D_MODEL), w_in_bf, cos_p, sin_p, rg)
    yb_p, kwin_p, vwin_p = _attn_prompt(q_p, k_p, v_p)
    tiles_per_seq = sp // tile_p
    mod_p3 = mod_p.reshape(bp * 6, 1, D_MODEL)
    mod_spec_p = lambda j: pl.BlockSpec((None, 1, D_MODEL),
                                        lambda i: ((i // tiles_per_seq) * 6 + j, 0, 0))
    x1_p, u2t_p, route_p = _outproj(
        ya_p.reshape(bp * sp, D_A), yb_p.reshape(bp * sp, D_B), x_prompt.reshape(bp * sp, D_MODEL),
        (mod_p3, mod_spec_p(2), mod_spec_p(3), mod_spec_p(4)), *outproj_weights, tile_p)

    cos_s, sin_s = _rope_tables(PAST_LEN + jnp.arange(ts))
    tb = lambda t: jnp.broadcast_to(jnp.tile(t, (1, N_HEAD_PAIRS))[:, None, :], (ts, bs, D_B))
    ya_s, q_s, k_s, v_s, conv_s, h_s = _inproj_sample(
        x_sample.transpose(1, 0, 2), mod_s, w_in_bf, tb(cos_s), tb(sin_s),
        state_conv.transpose(1, 0, 2), state_rglru, rg)
    bt = lambda t: t.transpose(1, 0, 2)

    ffn_p, yb_s, kwin_s, vwin_s = _moe(
        u2t_p, route_p[:, 0, :].reshape(-1), *expert_weights, 256,
        bt(q_s), bt(k_s), bt(v_s), cache_win_k.transpose(0, 2, 3, 1),
        cache_win_v.transpose(0, 2, 3, 1))
    y_p = _final_norm(x1_p, ffn_p, mod_p3, mod_spec_p(5), *ln2, tile_p)

    mod_spec_s = lambda j: pl.BlockSpec((bs, D_MODEL), lambda i: (0, j))
    x1_s, u2t_s, _ = _outproj(
        ya_s.reshape(ts * bs, D_A), bt(yb_s).reshape(ts * bs, D_B),
        x_sample.transpose(1, 0, 2).reshape(ts * bs, D_MODEL),
        (mod_s, mod_spec_s(2), mod_spec_s(3), mod_spec_s(4)), *outproj_weights, bs)
    ffn_s = _moe_dense(u2t_s, *expert_weights)
    y_s = _final_norm(x1_s, ffn_s, mod_s, mod_spec_s(5), *ln2, bs)

    heads = lambda t: t.reshape(t.shape[0], t.shape[1], N_HEADS_B, HEAD_DIM)
    return (y_p.reshape(bp, sp, D_MODEL), bt(y_s.reshape(ts, bs, D_MODEL)),
            conv_p, h_p.reshape(bp, D_A), heads(kwin_p), heads(vwin_p),
            bt(conv_s), h_s, kwin_s.transpose(0, 3, 1, 2), vwin_s.transpose(0, 3, 1, 2))
```

```python
import functools
import math

import jax
import jax.numpy as jnp
from jax import lax
from jax.experimental import pallas as pl
from jax.experimental.pallas import tpu as pltpu

F32 = jnp.float32
BF16 = jnp.bfloat16

D_MODEL = 1024
D_A = 512
N_BLK_A = 8
BLK_W_A = D_A // N_BLK_A
CONV_W = 4
RG_C = 8.0
D_B = 512
HEAD_DIM = 64
N_HEADS_B = D_B // HEAD_DIM
DILATED_CFG = ((128, 1), (512, 4), (2048, 16))
WIN_MAX = 2048
N_KEYS = 128
ROPE_THETA = 10000.0
PAST_LEN = 8192
N_GROUPS = 4
N_EXP_PER_GROUP = 4
N_EXPERTS = N_GROUPS * N_EXP_PER_GROUP
D_EXPERT = 512
DN_ALPHA = 2.0 ** 0.25
LN_EPS = 1e-5
NEG_INF = -1e30

N_PAIRS = 6
N_CLASSES = N_GROUPS * N_PAIRS
LANES = 128
SUBLANES = 8
HEAD_PAIR_W = 2 * HEAD_DIM
N_HEAD_PAIRS = N_HEADS_B // 2
Q_BLK = 128
ATTN_UNROLL = 32
DMA_UNROLL = 8
INPROJ_PARTS = 1
ROUTE_W = LANES
TOKEN_ROWS = D_MODEL // LANES
TOKEN_PITCH = TOKEN_ROWS + 1
N_BUF = 3
VMEM_LIMIT = 56 * 1024 * 1024


def _cparams(sem):
    return pltpu.CompilerParams(dimension_semantics=sem, vmem_limit_bytes=VMEM_LIMIT)


def _mod_kernel(c_ref, w_ref, b_ref, o_ref):
    c = c_ref[...]
    s = (c * jax.nn.sigmoid(c)).astype(BF16)
    o_ref[...] = jnp.dot(s, w_ref[...].astype(BF16), preferred_element_type=F32) + b_ref[...]


def _modulation(c_all, w_ada, b_ada):
    n = c_all.shape[0]
    tn = 1024
    return pl.pallas_call(
        _mod_kernel,
        grid=(6 * D_MODEL // tn,),
        in_specs=[pl.BlockSpec((n, D_MODEL), lambda j: (0, 0)),
                  pl.BlockSpec((D_MODEL, tn), lambda j: (0, j)),
                  pl.BlockSpec((1, tn), lambda j: (0, j))],
        out_specs=pl.BlockSpec((n, tn), lambda j: (0, j)),
        out_shape=jax.ShapeDtypeStruct((n, 6 * D_MODEL), F32),
        compiler_params=_cparams(("arbitrary",)),
        name="adaln_mod",
    )(c_all, w_ada, b_ada.reshape(1, -1))


def _rope_apply(t, cos, sin_signed):
    lane = lax.broadcasted_iota(jnp.int32, t.shape, t.ndim - 1)
    first_half = (lane & (HEAD_DIM - 1)) < HEAD_DIM // 2
    width = t.shape[-1]
    swapped = jnp.where(first_half,
                        pltpu.roll(t, width - HEAD_DIM // 2, t.ndim - 1),
                        pltpu.roll(t, HEAD_DIM // 2, t.ndim - 1))
    return t * cos + swapped * sin_signed


def _rglru_gates(xc, wg_ref, b_a, b_x, lam):
    half = D_A // 2
    xcb = xc.astype(BF16)
    g0 = jnp.dot(xcb[:, :half], wg_ref[0], preferred_element_type=F32)
    g1 = jnp.dot(xcb[:, half:], wg_ref[1], preferred_element_type=F32)
    r = jax.nn.sigmoid(jnp.concatenate([g0[:, :half], g1[:, :half]], axis=1) + b_a)
    i = jax.nn.sigmoid(jnp.concatenate([g0[:, half:], g1[:, half:]], axis=1) + b_x)
    z = -lam
    softplus = jnp.maximum(z, 0.0) + jnp.log1p(jnp.exp(-jnp.abs(z)))
    log_a = -RG_C * r * softplus
    a = jnp.exp(log_a)
    one_minus_a2 = -jnp.tanh(log_a) * (a * a + 1.0)
    u = jnp.sqrt(one_minus_a2) * (i * xc)
    return a, u


def _store_token_tiles(ref, x, pitch=TOKEN_ROWS):
    n = x.shape[0]
    for c in range(TOKEN_ROWS):
        ref[pl.ds(c, n, stride=pitch), :] = x[:, c * LANES:(c + 1) * LANES]


def _load_token_tiles(ref, n, pitch=TOKEN_ROWS):
    return jnp.concatenate(
        [ref[pl.ds(c, n, stride=pitch), :] for c in range(TOKEN_ROWS)], axis=1)


def _rms_norm(y, g):
    return y * lax.rsqrt(jnp.mean(y * y, axis=-1, keepdims=True) + LN_EPS) * g


def _layer_norm(x, g, b):
    mu = jnp.mean(x, axis=-1, keepdims=True)
    xc = x - mu
    var = jnp.mean(xc * xc, axis=-1, keepdims=True)
    return xc * lax.rsqrt(var + LN_EPS) * g + b


def _inproj_prompt_kernel(x_ref, mod_ref, w_in_ref, cos_ref, sin_ref, conv_w_ref, conv_b_ref,
                          wg_ref, b_a_ref, b_x_ref, lam_ref, gna_ref,
                          ya_ref, q_ref, k_ref, v_ref, conv_out_ref, h_out_ref,
                          xp_buf, h_carry):
    t = pl.program_id(1)
    rows = x_ref.shape[0]
    pad = SUBLANES

    @pl.when(t == 0)
    def _():
        xp_buf[0:pad, :] = jnp.zeros((pad, D_A), F32)
        h_carry[...] = jnp.zeros_like(h_carry)

    row = lax.broadcasted_iota(jnp.int32, (SUBLANES, D_A), 0)

    def part(r0, n, h_prev):
        sl = slice(r0, r0 + n)
        u = (x_ref[sl, :] * (1.0 + mod_ref[1:2, :]) + mod_ref[0:1, :]).astype(BF16)

        def proj(j):
            return jnp.dot(u, w_in_ref[:, j * D_A:(j + 1) * D_A], preferred_element_type=F32)

        cos = jnp.concatenate([cos_ref[sl, :]] * N_HEAD_PAIRS, axis=1)
        sin = jnp.concatenate([sin_ref[sl, :]] * N_HEAD_PAIRS, axis=1)
        q_ref[sl, :] = _rope_apply(proj(2), cos, sin)
        k_ref[sl, :] = _rope_apply(proj(3), cos, sin)
        v_ref[sl, :] = proj(4)

        xa = proj(0)
        xp_buf[pad + r0:pad + r0 + n, :] = xa
        xc = conv_b_ref[...] + xa * conv_w_ref[CONV_W - 1:CONV_W, :]
        for j in range(CONV_W - 1):
            off = pad - (CONV_W - 1) + j + r0
            xc = xc + xp_buf[off:off + n, :] * conv_w_ref[j:j + 1, :]

        a, u_in = _rglru_gates(xc, wg_ref, b_a_ref[...], b_x_ref[...], lam_ref[...])

        hs = []
        for g in range(n // SUBLANES):
            ag = a[g * SUBLANES:(g + 1) * SUBLANES]
            ug = u_in[g * SUBLANES:(g + 1) * SUBLANES]
            for sh in (1, 2, 4):
                keep = row >= sh
                a_sh = pltpu.roll(ag, sh, 0)
                u_sh = pltpu.roll(ug, sh, 0)
                ug = jnp.where(keep, ag * u_sh + ug, ug)
                ag = jnp.where(keep, ag * a_sh, ag)
            hg = ag * h_prev + ug
            hs.append(hg)
            h_prev = hg[SUBLANES - 1:SUBLANES, :]

        y = jnp.concatenate(hs, axis=0) * jax.nn.gelu(proj(1))
        ya_ref[sl, :] = _rms_norm(y, gna_ref[...]).astype(BF16)
        return h_prev

    h_last = h_carry[...]
    n_part = rows // INPROJ_PARTS
    for p in range(INPROJ_PARTS):
        h_last = part(p * n_part, n_part, h_last)
    h_carry[...] = h_last
    h_out_ref[...] = h_last

    tail = xp_buf[rows + pad - (CONV_W - 1):rows + pad, :]
    conv_out_ref[...] = tail
    xp_buf[pad - (CONV_W - 1):pad, :] = tail


def _inproj_prompt(x, mod3, w_in_bf, cos_t, sin_t, rg):
    b, s, _ = x.shape
    tile = 512
    nt = s // tile
    row_spec = lambda w: pl.BlockSpec((None, tile, w), lambda i, j: (i, j, 0))
    vec = lambda r, w: pl.BlockSpec((r, w), lambda i, j: (0, 0))
    outs = pl.pallas_call(
        _inproj_prompt_kernel,
        grid=(b, nt),
        in_specs=[row_spec(D_MODEL),
                  pl.BlockSpec((None, 6, D_MODEL), lambda i, j: (i, 0, 0)),
                  vec(D_MODEL, 5 * D_A),
                  pl.BlockSpec((tile, HEAD_PAIR_W), lambda i, j: (j, 0)),
                  pl.BlockSpec((tile, HEAD_PAIR_W), lambda i, j: (j, 0)),
                  vec(CONV_W, D_A), vec(1, D_A),
                  pl.BlockSpec((2, D_A // 2, D_A), lambda i, j: (0, 0, 0)),
                  vec(1, D_A), vec(1, D_A), vec(1, D_A), vec(1, D_A)],
        out_specs=[row_spec(D_A), row_spec(D_B), row_spec(D_B), row_spec(D_B),
                   pl.BlockSpec((None, CONV_W - 1, D_A), lambda i, j: (i, 0, 0)),
                   pl.BlockSpec((None, 1, D_A), lambda i, j: (i, 0, 0))],
        out_shape=[jax.ShapeDtypeStruct((b, s, D_A), BF16),
                   jax.ShapeDtypeStruct((b, s, D_B), F32),
                   jax.ShapeDtypeStruct((b, s, D_B), F32),
                   jax.ShapeDtypeStruct((b, s, D_B), F32),
                   jax.ShapeDtypeStruct((b, CONV_W - 1, D_A), F32),
                   jax.ShapeDtypeStruct((b, 1, D_A), F32)],
        scratch_shapes=[pltpu.VMEM((tile + SUBLANES, D_A), F32),
                        pltpu.VMEM((1, D_A), F32)],
        compiler_params=_cparams(("arbitrary", "arbitrary")),
        name="inproj_prompt",
    )(x, mod3, w_in_bf, cos_t, sin_t, rg["conv_w"], rg["conv_b"], rg["w_gate"],
      rg["b_a"], rg["b_x"], rg["lam"], rg["g_norm_a"])
    return outs


def _attn_prompt_kernel(q_ref, k_ref, v_ref, o_ref, kwin_ref, vwin_ref, acc_s, m_s, l_s, bias_s):
    s = q_ref.shape[0]
    keep = kwin_ref.shape[0]
    kwin_ref[...] = k_ref[s - keep:s, :]
    vwin_ref[...] = v_ref[s - keep:s, :]

    lane = lax.broadcasted_iota(jnp.int32, (Q_BLK, HEAD_PAIR_W), 1)
    head0 = lane < HEAD_DIM
    nk = 2 * Q_BLK

    qi = lax.broadcasted_iota(jnp.int32, (Q_BLK, nk), 0)
    ki = lax.broadcasted_iota(jnp.int32, (Q_BLK, nk), 1)
    for slot in range(2):
        dist = slot * Q_BLK + qi - ki
        bias_s[slot] = jnp.where((dist >= 0) & (dist <= N_KEYS), 0.0, NEG_INF)

    def rows(start, n, d):
        return pl.ds(start, n) if d == 1 else pl.ds(start, n, stride=d)

    def unit(bi, d, nb, u):
        r = u // nb
        j = u % nb
        jk = jnp.maximum(j - 1, 0)
        start_q = r + d * Q_BLK * j
        start_k = r + d * Q_BLK * jk
        bias = bias_s[j - jk]
        qb = q_ref[rows(start_q, Q_BLK, d), :] * (HEAD_DIM ** -0.5)
        kb = k_ref[rows(start_k, nk, d), :].astype(BF16)
        vb = v_ref[rows(start_k, nk, d), :].astype(BF16)
        q2 = jnp.concatenate([jnp.where(head0, qb, 0.0), jnp.where(head0, 0.0, qb)],
                             axis=0).astype(BF16)
        sc = lax.dot_general(q2, kb, (((1,), (1,)), ((), ())), preferred_element_type=F32)
        sc = sc + jnp.concatenate([bias, bias], axis=0)
        m = jnp.max(sc, axis=-1, keepdims=True)
        p = jnp.exp(sc - m)
        l = jnp.sum(p, axis=-1, keepdims=True)
        o = jnp.dot(p.astype(BF16), vb, preferred_element_type=F32)
        dst = rows(start_q, Q_BLK, d)
        acc_s[bi, dst, :] = jnp.where(head0, o[0:Q_BLK], o[Q_BLK:])
        m_s[bi, dst, :] = jnp.where(head0, m[0:Q_BLK], m[Q_BLK:])
        l_s[bi, dst, :] = jnp.where(head0, l[0:Q_BLK], l[Q_BLK:])

    for bi, (_, d) in enumerate(DILATED_CFG):
        nb = s // d // Q_BLK

        def body(u, carry, bi=bi, d=d, nb=nb):
            unit(bi, d, nb, u)
            return carry

        lax.fori_loop(0, d * nb, body, 0, unroll=ATTN_UNROLL)

    chunk = 512

    def merge(i, carry):
        r0 = pl.multiple_of(i * chunk, chunk)
        sl = pl.ds(r0, chunk)
        ms = [m_s[bi, sl, :] for bi in range(len(DILATED_CFG))]
        mx = functools.reduce(jnp.maximum, ms)
        num = jnp.zeros((chunk, HEAD_PAIR_W), F32)
        den = jnp.zeros((chunk, HEAD_PAIR_W), F32)
        for bi in range(len(DILATED_CFG)):
            w = jnp.exp(ms[bi] - mx)
            num = num + w * acc_s[bi, sl, :]
            den = den + w * l_s[bi, sl, :]
        o_ref[sl, :] = num / den
        return carry

    lax.fori_loop(0, s // chunk, merge, 0)


def _attn_prompt(q, k, v):
    b, s, _ = q.shape
    keep = min(WIN_MAX, s)
    nbr = len(DILATED_CFG)
    spec = pl.BlockSpec((None, s, HEAD_PAIR_W), lambda i, j: (i, 0, j))
    wspec = pl.BlockSpec((None, keep, HEAD_PAIR_W), lambda i, j: (i, 0, j))
    return pl.pallas_call(
        _attn_prompt_kernel,
        grid=(b, N_HEAD_PAIRS),
        in_specs=[spec, spec, spec],
        out_specs=[spec, wspec, wspec],
        out_shape=[jax.ShapeDtypeStruct((b, s, D_B), F32),
                   jax.ShapeDtypeStruct((b, keep, D_B), F32),
                   jax.ShapeDtypeStruct((b, keep, D_B), F32)],
        scratch_shapes=[pltpu.VMEM((nbr, s, HEAD_PAIR_W), F32),
                        pltpu.VMEM((nbr, s, HEAD_PAIR_W), F32),
                        pltpu.VMEM((nbr, s, HEAD_PAIR_W), F32),
                        pltpu.VMEM((2, Q_BLK, 2 * Q_BLK), F32)],
        compiler_params=_cparams(("arbitrary", "arbitrary")),
        name="attn_prompt",
    )(q, k, v)


def _inproj_sample_kernel(x_ref, mod_ref, w_in_ref, cos_ref, sin_ref, conv_state_ref, h0_ref,
                          conv_w_ref, conv_b_ref, wg_ref, b_a_ref, b_x_ref, lam_ref, gna_ref,
                          ya_ref, q_ref, k_ref, v_ref, conv_out_ref, h_out_ref):
    nt, nb, _ = x_ref.shape
    sh1 = mod_ref[:, 0:D_MODEL]
    sc1 = mod_ref[:, D_MODEL:2 * D_MODEL]
    u = (x_ref[...] * (1.0 + sc1)[None] + sh1[None]).astype(BF16).reshape(nt * nb, D_MODEL)

    def proj(j):
        return jnp.dot(u, w_in_ref[:, j * D_A:(j + 1) * D_A], preferred_element_type=F32)

    cos = cos_ref[...].reshape(nt * nb, D_B)
    sin = sin_ref[...].reshape(nt * nb, D_B)
    q_ref[...] = _rope_apply(proj(2), cos, sin).reshape(nt, nb, D_B)
    k_ref[...] = _rope_apply(proj(3), cos, sin).reshape(nt, nb, D_B)
    v_ref[...] = proj(4).reshape(nt, nb, D_B)

    xa = proj(0).reshape(nt, nb, D_A)
    xp = [conv_state_ref[j] for j in range(CONV_W - 1)] + [xa[t] for t in range(nt)]
    xc = jnp.concatenate(
        [conv_b_ref[...] + sum(xp[t + j] * conv_w_ref[j:j + 1, :] for j in range(CONV_W))
         for t in range(nt)], axis=0)
    for j in range(CONV_W - 1):
        conv_out_ref[j] = xp[nt + j]

    a, u_in = _rglru_gates(xc, wg_ref, b_a_ref[...], b_x_ref[...], lam_ref[...])
    h = h0_ref[...]
    hs = []
    for t in range(nt):
        h = a[t * nb:(t + 1) * nb] * h + u_in[t * nb:(t + 1) * nb]
        hs.append(h)
    h_out_ref[...] = h
    y = jnp.concatenate(hs, axis=0) * jax.nn.gelu(proj(1))
    ya_ref[...] = _rms_norm(y, gna_ref[...]).astype(BF16).reshape(nt, nb, D_A)


def _inproj_sample(x_tb, mod_s, w_in_bf, cos_t, sin_t, conv_state_tb, h0, rg):
    nt, nb, _ = x_tb.shape
    full = lambda shape: pl.BlockSpec(shape, lambda i: (0,) * len(shape))
    return pl.pallas_call(
        _inproj_sample_kernel,
        grid=(1,),
        in_specs=[full((nt, nb, D_MODEL)), full((nb, 6 * D_MODEL)), full((D_MODEL, 5 * D_A)),
                  full((nt, nb, D_B)), full((nt, nb, D_B)),
                  full((CONV_W - 1, nb, D_A)), full((nb, D_A)),
                  full((CONV_W, D_A)), full((1, D_A)), full((2, D_A // 2, D_A)),
                  full((1, D_A)), full((1, D_A)), full((1, D_A)), full((1, D_A))],
        out_specs=[full((nt, nb, D_A)), full((nt, nb, D_B)), full((nt, nb, D_B)),
                   full((nt, nb, D_B)), full((CONV_W - 1, nb, D_A)), full((nb, D_A))],
        out_shape=[jax.ShapeDtypeStruct((nt, nb, D_A), BF16),
                   jax.ShapeDtypeStruct((nt, nb, D_B), F32),
                   jax.ShapeDtypeStruct((nt, nb, D_B), F32),
                   jax.ShapeDtypeStruct((nt, nb, D_B), F32),
                   jax.ShapeDtypeStruct((CONV_W - 1, nb, D_A), F32),
                   jax.ShapeDtypeStruct((nb, D_A), F32)],
        compiler_params=_cparams(("arbitrary",)),
        name="inproj_sample",
    )(x_tb, mod_s, w_in_bf, cos_t, sin_t, conv_state_tb, h0, rg["conv_w"], rg["conv_b"],
      rg["w_gate"], rg["b_a"], rg["b_x"], rg["lam"], rg["g_norm_a"])


def _sample_attention_step(b, n_seq, q_ref, kn_ref, vn_ref, ck_hbm, cv_hbm, o_ref, kwin_hbm,
                           vwin_hbm, cin, cout, sem_in, sem_out, kn_pad, vn_pad):
    nt = q_ref.shape[0]
    n_buf = cin.shape[-1]
    n_rows = N_HEADS_B * nt
    srcs = (ck_hbm, cv_hbm)
    dsts = (kwin_hbm, vwin_hbm)

    def copy_in(which, seq):
        return pltpu.make_async_copy(srcs[which].at[seq], cin.at[which], sem_in.at[which])

    def copy_out(which, seq):
        return pltpu.make_async_copy(cout.at[which], dsts[which].at[seq], sem_out.at[which])

    @pl.when(b == 0)
    def _():
        kn_pad[...] = jnp.zeros_like(kn_pad)
        vn_pad[...] = jnp.zeros_like(vn_pad)
        copy_in(0, 0).start()
        copy_in(1, 0).start()

    kn_pad[0:nt, :] = kn_ref[...]
    vn_pad[0:nt, :] = vn_ref[...]

    tail_lane = lax.broadcasted_iota(jnp.int32, (HEAD_DIM, LANES), 1)

    def shift_in(old_ref, new_pad, out_ref):
        new_t = pltpu.roll(new_pad.T, LANES - nt, 1)
        for h in range(N_HEADS_B):
            rolled = pltpu.roll(old_ref[h], n_buf - nt, 1)
            last = jnp.where(tail_lane < LANES - nt, rolled[:, n_buf - LANES:n_buf],
                             new_t[h * HEAD_DIM:(h + 1) * HEAD_DIM])
            out_ref[h, :, 0:n_buf - LANES] = rolled[:, 0:n_buf - LANES]
            out_ref[h, :, n_buf - LANES:n_buf] = last

    copy_in(0, b).wait()
    copy_in(1, b).wait()

    @pl.when(b >= 1)
    def _():
        copy_out(0, b - 1).wait()
        copy_out(1, b - 1).wait()

    ri = lax.broadcasted_iota(jnp.int32, (n_rows, nt), 0)
    ci = lax.broadcasted_iota(jnp.int32, (n_rows, nt), 1)
    pick = (ri % nt == ci).astype(BF16)
    qs = (q_ref[...] * (HEAD_DIM ** -0.5)).astype(BF16)
    q_rep = jnp.dot(pick, qs, preferred_element_type=F32)
    row_h = lax.broadcasted_iota(jnp.int32, (n_rows, D_B), 0) // nt
    lane_h = lax.broadcasted_iota(jnp.int32, (n_rows, D_B), 1) // HEAD_DIM
    own = row_h == lane_h
    qbd = jnp.where(own, q_rep, 0.0).astype(BF16)

    nt_dims = (((1,), (1,)), ((), ()))

    def mult(dist, limit_ok):
        c = jnp.zeros(dist.shape, F32)
        for win, d in DILATED_CFG:
            hit = (dist >= 0) & (dist <= win) & (dist % d == 0) & limit_ok
            c = c + hit.astype(F32)
        return c

    def weights(ck_t):
        sc_c = jnp.dot(qbd, ck_t.astype(BF16), preferred_element_type=F32)
        sc_n = lax.dot_general(qbd, kn_pad[...].astype(BF16), nt_dims,
                               preferred_element_type=F32)
        t_c = lax.broadcasted_iota(jnp.int32, sc_c.shape, 0) % nt
        dist_c = n_buf + t_c - lax.broadcasted_iota(jnp.int32, sc_c.shape, 1)
        mult_c = mult(dist_c, dist_c >= 0)
        t_n = lax.broadcasted_iota(jnp.int32, sc_n.shape, 0) % nt
        col_n = lax.broadcasted_iota(jnp.int32, sc_n.shape, 1)
        mult_n = mult(t_n - col_n, col_n < nt)
        sc_c = jnp.where(mult_c > 0, sc_c, NEG_INF)
        sc_n = jnp.where(mult_n > 0, sc_n, NEG_INF)
        m = jnp.maximum(jnp.max(sc_c, axis=-1, keepdims=True),
                        jnp.max(sc_n, axis=-1, keepdims=True))
        p_c = mult_c * jnp.exp(sc_c - m)
        p_n = mult_n * jnp.exp(sc_n - m)
        l = jnp.sum(p_c, axis=-1, keepdims=True) + jnp.sum(p_n, axis=-1, keepdims=True)
        return p_c.astype(BF16), p_n.astype(BF16), l

    p_c, p_n, l = weights(cin[0].reshape(D_B, n_buf))
    shift_in(cin.at[0], kn_pad[...], cout.at[0])
    acc = (lax.dot_general(p_c, cin[1].reshape(D_B, n_buf).astype(BF16), nt_dims,
                           preferred_element_type=F32)
           + jnp.dot(p_n, vn_pad[...].astype(BF16), preferred_element_type=F32))
    shift_in(cin.at[1], vn_pad[...], cout.at[1])
    o_full = jnp.where(own, acc / l, 0.0)
    out = o_full[0:nt, :]
    for h in range(1, N_HEADS_B):
        out = out + o_full[h * nt:(h + 1) * nt, :]
    o_ref[...] = out

    copy_out(0, b).start()
    copy_out(1, b).start()

    @pl.when(b + 1 < n_seq)
    def _():
        copy_in(0, b + 1).start()
        copy_in(1, b + 1).start()

    @pl.when(b == n_seq - 1)
    def _():
        copy_out(0, b).wait()
        copy_out(1, b).wait()


def _split_bf16(x):
    hi = x.astype(BF16)
    lo = (x - hi.astype(F32)).astype(BF16)
    return hi, lo


def _first_argmax(vals):
    mx = functools.reduce(jnp.maximum, vals)
    idx = jnp.full(mx.shape, float(len(vals) - 1), F32)
    for j in range(len(vals) - 2, -1, -1):
        idx = jnp.where(vals[j] == mx, float(j), idx)
    return mx, idx


def _outproj_kernel(ya_ref, yb_ref, x_ref, gt1_ref, sh2_ref, sc2_ref, w_out_ref, gnb_ref,
                    ln_g_ref, ln_b_ref, w_r_ref, b_r_ref, x1_ref, u2_ref, route_ref):
    yb = _rms_norm(yb_ref[...], gnb_ref[...]).astype(BF16)
    mixed = (jnp.dot(ya_ref[...], w_out_ref[0:D_A, :], preferred_element_type=F32)
             + jnp.dot(yb, w_out_ref[D_A:D_A + D_B, :], preferred_element_type=F32))
    x1 = _layer_norm(DN_ALPHA * x_ref[...] + gt1_ref[...] * mixed, ln_g_ref[...], ln_b_ref[...])
    x1_ref[...] = x1
    u2 = x1 * (1.0 + sc2_ref[...]) + sh2_ref[...]
    _store_token_tiles(u2_ref, u2, TOKEN_PITCH)

    u_hi, u_lo = _split_bf16(u2)
    w_hi, w_lo = _split_bf16(w_r_ref[...])
    both = jnp.dot(u_hi, jnp.concatenate([w_hi, w_lo], axis=1), preferred_element_type=F32)
    logits = (both[:, 0:ROUTE_W] + jnp.dot(u_lo, w_hi, preferred_element_type=F32)
              + both[:, ROUTE_W:2 * ROUTE_W]) + b_r_ref[...]
    lt = logits.T

    g_rows = [lt[j:j + 1, :] for j in range(N_GROUPS)]
    g_max, g_idx = _first_argmax(g_rows)
    p_group = 1.0 / sum(jnp.exp(g - g_max) for g in g_rows)
    e_rows = []
    for e in range(N_EXP_PER_GROUP):
        acc = jnp.zeros_like(g_max)
        for g in range(N_GROUPS):
            r = N_GROUPS + g * N_EXP_PER_GROUP + e
            acc = jnp.where(g_idx == float(g), lt[r:r + 1, :], acc)
        e_rows.append(acc)
    v1, i1 = _first_argmax(e_rows)
    rest = [jnp.where(i1 == float(e), -jnp.inf, e_rows[e]) for e in range(N_EXP_PER_GROUP)]
    v2, i2 = _first_argmax(rest)
    ex = jnp.exp(v2 - v1)
    w1 = p_group / (1.0 + ex)
    w2 = p_group * ex / (1.0 + ex)
    lo = jnp.minimum(i1, i2)
    hi = jnp.maximum(i1, i2)
    pair = jnp.where(lo == 0.0, hi - 1.0, jnp.where(lo == 1.0, hi + 1.0, 5.0))
    cls = g_idx * float(N_PAIRS) + pair
    w_of_lo = jnp.where(i1 < i2, w1, w2)
    w_of_hi = jnp.where(i1 < i2, w2, w1)
    e_lo = g_idx * float(N_EXP_PER_GROUP) + lo
    e_hi = g_idx * float(N_EXP_PER_GROUP) + hi
    n_tok = cls.shape[1]
    route = jnp.concatenate(
        [cls, w_of_lo, w_of_hi, e_lo, e_hi, jnp.zeros((ROUTE_W - 5, n_tok), F32)], axis=0)
    route_ref[...] = route[0:SUBLANES, :]
    u2_ref[pl.ds(TOKEN_ROWS, n_tok, stride=TOKEN_PITCH), :] = route.T


def _outproj(ya, yb, x, mods, w_out_bf, gnb, ln_g, ln_b, w_r, b_r, tile):
    n = x.shape[0]
    nt = n // tile
    mod_arr, gt1_spec, sh2_spec, sc2_spec = mods
    row = lambda w: pl.BlockSpec((tile, w), lambda i: (i, 0))
    vec = lambda r, w: pl.BlockSpec((r, w), lambda i: (0, 0))
    return pl.pallas_call(
        _outproj_kernel,
        grid=(nt,),
        in_specs=[row(D_A), row(D_B), row(D_MODEL), gt1_spec, sh2_spec, sc2_spec,
                  vec(D_MODEL, D_MODEL), vec(1, D_B), vec(1, D_MODEL), vec(1, D_MODEL),
                  vec(D_MODEL, ROUTE_W), vec(1, ROUTE_W)],
        out_specs=[row(D_MODEL),
                   pl.BlockSpec((tile * TOKEN_PITCH, LANES), lambda i: (i, 0)),
                   pl.BlockSpec((None, SUBLANES, tile), lambda i: (i, 0, 0))],
        out_shape=[jax.ShapeDtypeStruct((n, D_MODEL), F32),
                   jax.ShapeDtypeStruct((n * TOKEN_PITCH, LANES), F32),
                   jax.ShapeDtypeStruct((nt, SUBLANES, tile), F32)],
        compiler_params=_cparams(("arbitrary",)),
        name="outproj_router",
    )(ya, yb, x, mod_arr, mod_arr, mod_arr, w_out_bf, gnb, ln_g, ln_b, w_r, b_r)


def _moe_kernel(e_lo_ref, e_hi_ref, n_used_ref, src_ref, dst_ref,
                x_hbm, wg_lo, wg_hi, wu_lo, wu_hi, wd_lo, wd_hi,
                q_ref, kn_ref, vn_ref, ck_hbm, cv_hbm,
                o_hbm, yb_ref, kwin_hbm, vwin_hbm,
                xg, og, gsem, ssem, cin, cout, csem_in, csem_out, kn_pad, vn_pad,
                *, n_tokens, tile, n_seq):
    i = pl.program_id(0)
    n_steps = pl.num_programs(0)
    n_used = n_used_ref[0]
    cur = i % N_BUF
    ahead = (i + 2) % N_BUF
    rows = tile * TOKEN_ROWS

    def gather_token(base, r, buf):
        pltpu.make_async_copy(x_hbm.at[pl.ds(src_ref[base + r], TOKEN_PITCH)],
                              xg.at[buf, pl.ds(r * TOKEN_PITCH, TOKEN_PITCH)],
                              gsem.at[buf]).start()

    def gather_wait(buf):
        pltpu.make_async_copy(x_hbm.at[pl.ds(0, tile * TOKEN_PITCH)], xg.at[buf],
                              gsem.at[buf]).wait()

    def scatter_wait(buf):
        pltpu.make_async_copy(og.at[buf], o_hbm.at[pl.ds(0, rows)], ssem.at[buf]).wait()

    @pl.when(i == 0)
    def _():
        def first(r, c):
            gather_token(0, r, 0)
            gather_token(jnp.minimum(1, n_steps - 1) * tile, r, 1)
            return c
        lax.fori_loop(0, tile, first, 0, unroll=DMA_UNROLL)
        og[...] = jnp.zeros_like(og)
        for buf in range(N_BUF):
            pad_rows = pltpu.make_async_copy(
                og.at[buf], o_hbm.at[pl.ds((n_tokens + buf * tile) * TOKEN_ROWS, rows)],
                ssem.at[buf])
            pad_rows.start()
            pad_rows.wait()

    def expert_tile():
        nxt = jnp.minimum(i + 2, n_steps - 1) * tile
        for r in range(tile):
            gather_token(nxt, r, ahead)
        prev = i * tile
        for r in range(tile):
            row = pl.multiple_of(dst_ref[prev + r], TOKEN_ROWS)
            pltpu.make_async_copy(og.at[ahead, pl.ds(r * TOKEN_ROWS, TOKEN_ROWS)],
                                  o_hbm.at[pl.ds(row, TOKEN_ROWS)], ssem.at[ahead]).start()

        xb = _load_token_tiles(xg.at[cur], tile, TOKEN_PITCH).astype(BF16)
        w2 = xg[cur, pl.ds(TOKEN_ROWS, tile, stride=TOKEN_PITCH), :]
        out = jnp.zeros((tile, D_MODEL), F32)
        for col, wg, wu, wd in ((1, wg_lo, wu_lo, wd_lo), (2, wg_hi, wu_hi, wd_hi)):
            hg = jnp.dot(xb, wg[...], preferred_element_type=F32)
            hu = jnp.dot(xb, wu[...], preferred_element_type=F32)
            act = (hg * jax.nn.sigmoid(hg)) * hu * w2[:, col:col + 1]
            out = out + jnp.dot(act.astype(BF16), wd[...], preferred_element_type=F32)
        _store_token_tiles(og.at[cur], out)

    @pl.when(i <= n_used)
    def _():
        gather_wait(cur)

        @pl.when(i >= 2)
        def _():
            scatter_wait(cur)

        expert_tile()

        @pl.when(i == n_used)
        def _():
            for buf in range(N_BUF):
                @pl.when(buf != cur)
                def _():
                    gather_wait(buf)
                    scatter_wait(buf)

    @pl.when(i < n_seq)
    def _():
        _sample_attention_step(i, n_seq, q_ref, kn_ref, vn_ref, ck_hbm, cv_hbm, yb_ref,
                               kwin_hbm, vwin_hbm, cin, cout, csem_in, csem_out, kn_pad, vn_pad)


def _moe(u2t, cls, w_gate_bf, w_up_bf, w_down_bf, tile, q_s, k_new, v_new, cache_k_t, cache_v_t):
    n = cls.shape[0]
    n_seq, nt, _ = q_s.shape
    n_buf = cache_k_t.shape[-1]
    n_steps = n // tile + N_CLASSES
    assert n_steps >= n_seq, "one sample sequence per grid step"
    cls = cls.astype(jnp.int32)
    order = jnp.argsort(cls, stable=True).astype(jnp.int32)
    class_ids = jnp.arange(N_CLASSES, dtype=jnp.int32)
    counts = jnp.sum((cls[:, None] == class_ids[None, :]).astype(jnp.int32), axis=0)
    tiles_per = (counts + tile - 1) // tile
    tile_end = jnp.cumsum(tiles_per)
    tile_off = tile_end - tiles_per
    n_used = tile_end[-1]
    class_start = jnp.cumsum(counts) - counts
    step = jnp.arange(n_steps, dtype=jnp.int32)
    step_c = jnp.minimum(step, n_used - 1)
    cls_of = jnp.sum((step_c[:, None] >= tile_end[None, :]).astype(jnp.int32), axis=1)
    onehot = (cls_of[:, None] == class_ids[None, :]).astype(jnp.int32)
    pick = lambda table: jnp.sum(onehot * table[None, :], axis=1)
    local = step - pick(tile_off)
    nvalid = jnp.where(step < n_used, jnp.clip(pick(counts) - local * tile, 0, tile), 0)
    r = jnp.arange(tile, dtype=jnp.int32)
    pos = pick(class_start)[:, None] + local[:, None] * tile + r[None, :]
    valid = r[None, :] < nvalid[:, None]
    tok = order[jnp.clip(pos, 0, n - 1)]
    src = (jnp.where(valid, tok, 0) * TOKEN_PITCH).astype(jnp.int32).reshape(-1)
    spare = n + (step[:, None] % N_BUF) * tile + r[None, :]
    dst = jnp.where(valid, tok, spare)
    dst = jnp.concatenate([(n + (N_BUF - 1) * tile + r)[None, :], dst], axis=0)
    dst = (dst * TOKEN_ROWS).astype(jnp.int32).reshape(-1)
    grp = cls_of // N_PAIRS
    pair = cls_of % N_PAIRS
    pair_lo = (pair >= 3).astype(jnp.int32) + (pair >= 5).astype(jnp.int32)
    pair_hi = pair + 1 - 2 * (pair >= 3).astype(jnp.int32) - (pair >= 5).astype(jnp.int32)
    e_lo = (grp * N_EXP_PER_GROUP + pair_lo).astype(jnp.int32)
    e_hi = (grp * N_EXP_PER_GROUP + pair_hi).astype(jnp.int32)

    w_in_spec = lambda which: pl.BlockSpec(
        (None, D_MODEL, D_EXPERT), lambda i, elo, ehi, nu, s, d: ((elo, ehi)[which][i], 0, 0))
    w_dn_spec = lambda which: pl.BlockSpec(
        (None, D_EXPERT, D_MODEL), lambda i, elo, ehi, nu, s, d: ((elo, ehi)[which][i], 0, 0))
    any_spec = pl.BlockSpec(memory_space=pl.ANY)
    seq_spec = pl.BlockSpec((None, nt, D_B),
                            lambda i, elo, ehi, nu, s, d: (jnp.minimum(i, n_seq - 1), 0, 0))
    win_shape = jax.ShapeDtypeStruct((n_seq, N_HEADS_B, HEAD_DIM, n_buf), F32)
    grid_spec = pltpu.PrefetchScalarGridSpec(
        num_scalar_prefetch=5,
        grid=(n_steps,),
        in_specs=[any_spec,
                  w_in_spec(0), w_in_spec(1), w_in_spec(0), w_in_spec(1),
                  w_dn_spec(0), w_dn_spec(1),
                  seq_spec, seq_spec, seq_spec, any_spec, any_spec],
        out_specs=[any_spec, seq_spec, any_spec, any_spec],
        scratch_shapes=[pltpu.VMEM((N_BUF, tile * TOKEN_PITCH, LANES), F32),
                        pltpu.VMEM((N_BUF, tile * TOKEN_ROWS, LANES), F32),
                        pltpu.SemaphoreType.DMA((N_BUF,)),
                        pltpu.SemaphoreType.DMA((N_BUF,)),
                        pltpu.VMEM((2, N_HEADS_B, HEAD_DIM, n_buf), F32),
                        pltpu.VMEM((2, N_HEADS_B, HEAD_DIM, n_buf), F32),
                        pltpu.SemaphoreType.DMA((2,)),
                        pltpu.SemaphoreType.DMA((2,)),
                        pltpu.VMEM((LANES, D_B), F32),
                        pltpu.VMEM((LANES, D_B), F32)],
    )
    return pl.pallas_call(
        functools.partial(_moe_kernel, n_tokens=n, tile=tile, n_seq=n_seq),
        grid_spec=grid_spec,
        out_shape=[jax.ShapeDtypeStruct(((n + N_BUF * tile) * TOKEN_ROWS, LANES), F32),
                   jax.ShapeDtypeStruct((n_seq, nt, D_B), F32), win_shape, win_shape],
        compiler_params=_cparams(("arbitrary",)),
        name="moe_sparse",
    )(e_lo, e_hi, n_used.reshape(1).astype(jnp.int32), src, dst,
      u2t, w_gate_bf, w_gate_bf, w_up_bf, w_up_bf, w_down_bf, w_down_bf,
      q_s, k_new, v_new, cache_k_t, cache_v_t)


def _moe_dense_kernel(x_ref, wg_ref, wu_ref, wd_ref, o_ref):
    e = pl.program_id(0)
    n = o_ref.shape[0]

    @pl.when(e == 0)
    def _():
        o_ref[...] = jnp.zeros_like(o_ref)

    xb = _load_token_tiles(x_ref, n, TOKEN_PITCH).astype(BF16)
    ef = e.astype(F32)
    route = x_ref[pl.ds(TOKEN_ROWS, n, stride=TOKEN_PITCH), :]
    col = lambda c: route[:, c:c + 1]
    comb = jnp.where(col(3) == ef, col(1), 0.0) + jnp.where(col(4) == ef, col(2), 0.0)
    hg = jnp.dot(xb, wg_ref[...], preferred_element_type=F32)
    hu = jnp.dot(xb, wu_ref[...], preferred_element_type=F32)
    act = (hg * jax.nn.sigmoid(hg)) * hu * comb
    o_ref[...] += jnp.dot(act.astype(BF16), wd_ref[...], preferred_element_type=F32)


def _moe_dense(u2t, w_gate_bf, w_up_bf, w_down_bf):
    n = u2t.shape[0] // TOKEN_PITCH
    return pl.pallas_call(
        _moe_dense_kernel,
        grid=(N_EXPERTS,),
        in_specs=[pl.BlockSpec((n * TOKEN_PITCH, LANES), lambda e: (0, 0)),
                  pl.BlockSpec((None, D_MODEL, D_EXPERT), lambda e: (e, 0, 0)),
                  pl.BlockSpec((None, D_MODEL, D_EXPERT), lambda e: (e, 0, 0)),
                  pl.BlockSpec((None, D_EXPERT, D_MODEL), lambda e: (e, 0, 0))],
        out_specs=pl.BlockSpec((n, D_MODEL), lambda e: (0, 0)),
        out_shape=jax.ShapeDtypeStruct((n, D_MODEL), F32),
        compiler_params=_cparams(("arbitrary",)),
        name="moe_dense",
    )(u2t, w_gate_bf, w_up_bf, w_down_bf)


def _final_kernel(x1_ref, ffn_ref, gt2_ref, g_ref, b_ref, o_ref, *, token_tiled):
    rows = x1_ref.shape[0]
    ffn = _load_token_tiles(ffn_ref, rows) if token_tiled else ffn_ref[...]
    o_ref[...] = _layer_norm(DN_ALPHA * x1_ref[...] + gt2_ref[...] * ffn, g_ref[...], b_ref[...])


def _final_norm(x1, ffn, mod_arr, gt2_spec, ln_g, ln_b, tile):
    n = x1.shape[0]
    token_tiled = ffn.shape[-1] == LANES
    row = pl.BlockSpec((tile, D_MODEL), lambda i: (i, 0))
    ffn_spec = pl.BlockSpec((tile * TOKEN_ROWS, LANES), lambda i: (i, 0)) if token_tiled else row
    vec = pl.BlockSpec((1, D_MODEL), lambda i: (0, 0))
    return pl.pallas_call(
        functools.partial(_final_kernel, token_tiled=token_tiled),
        grid=(n // tile,),
        in_specs=[row, ffn_spec, gt2_spec, vec, vec],
        out_specs=row,
        out_shape=jax.ShapeDtypeStruct((n, D_MODEL), F32),
        compiler_params=_cparams(("arbitrary",)),
        name="final_norm",
    )(x1, ffn, mod_arr, ln_g, ln_b)


def _rope_tables(pos):
    half = HEAD_DIM // 2
    inv = ROPE_THETA ** (-jnp.arange(half, dtype=F32) * 2.0 / HEAD_DIM)
    ang = pos.astype(F32)[:, None] * inv[None, :]
    cos = jnp.cos(ang)
    sin = jnp.sin(ang)
    cos_t = jnp.tile(jnp.concatenate([cos, cos], axis=-1), (1, 2))
    sin_t = jnp.tile(jnp.concatenate([-sin, sin], axis=-1), (1, 2))
    return cos_t, sin_t


def _block_diag(w):
    n, a, b = w.shape
    eye = jnp.eye(n, dtype=w.dtype)
    return (eye[:, None, :, None] * w[:, :, None, :]).reshape(n * a, n * b)


def _prepare_weights(w_in, conv_w, conv_b, w_rg_a, b_rg_a, w_rg_x, b_rg_x, rg_lambda, g_norm_a,
                     w_router_group, b_router_group, w_router_expert, b_router_expert):
    half_blocks = N_BLK_A // 2
    w_gate = jnp.stack([
        jnp.concatenate([_block_diag(w_rg_a[h * half_blocks:(h + 1) * half_blocks]),
                         _block_diag(w_rg_x[h * half_blocks:(h + 1) * half_blocks])], axis=1)
        for h in range(2)]).astype(BF16)
    rg = dict(conv_w=conv_w, conv_b=conv_b.reshape(1, D_A), w_gate=w_gate,
              b_a=b_rg_a.reshape(1, D_A), b_x=b_rg_x.reshape(1, D_A),
              lam=rg_lambda.reshape(1, D_A), g_norm_a=g_norm_a.reshape(1, D_A))
    n_logits = N_GROUPS + N_EXPERTS
    w_r = jnp.concatenate(
        [w_router_group,
         w_router_expert.transpose(1, 0, 2).reshape(D_MODEL, N_EXPERTS),
         jnp.zeros((D_MODEL, ROUTE_W - n_logits), F32)], axis=1)
    b_r = jnp.concatenate([b_router_group, b_router_expert.reshape(-1),
                           jnp.zeros((ROUTE_W - n_logits,), F32)]).reshape(1, ROUTE_W)
    return rg, w_r, b_r


def kernel(x_prompt, x_sample, state_conv, state_rglru, cache_win_k, cache_win_v, c_prompt, c_sample, w_ada, b_ada, w_in, conv_w, conv_b, w_rg_a, b_rg_a, w_rg_x, b_rg_x, rg_lambda, g_norm_a, g_norm_b, w_out, ln1_g, ln1_b, w_router_group, b_router_group, w_router_expert, b_router_expert, w_exp_gate, w_exp_up, w_exp_down, ln2_g, ln2_b):
    bp, sp, _ = x_prompt.shape
    bs, ts, _ = x_sample.shape
    n_buf = cache_win_k.shape[1]

    rg, w_r, b_r = _prepare_weights(w_in, conv_w, conv_b, w_rg_a, b_rg_a, w_rg_x, b_rg_x,
                                    rg_lambda, g_norm_a, w_router_group, b_router_group,
                                    w_router_expert, b_router_expert)
    w_in_bf = w_in.astype(BF16)
    outproj_weights = (w_out.astype(BF16), g_norm_b.reshape(1, D_B), ln1_g.reshape(1, D_MODEL),
                       ln1_b.reshape(1, D_MODEL), w_r, b_r)
    expert_weights = (w_exp_gate.astype(BF16), w_exp_up.astype(BF16), w_exp_down.astype(BF16))
    ln2 = (ln2_g.reshape(1, D_MODEL), ln2_b.reshape(1, D_MODEL))

    mod = _modulation(jnp.concatenate([c_prompt, c_sample], axis=0), w_ada, b_ada)
    mod_p, mod_s = mod[:bp], mod[bp:]

    tile_p = 512
    cos_p, sin_p = _rope_tables(jnp.arange(sp))
    ya_p, q_p, k_p, v_p, conv_p, h_p = _inproj_prompt(
        x_prompt, mod_p.reshape(bp, 6, D_MODEL), w_in_bf, cos_p, sin_p, rg)
    yb_p, kwin_p, vwin_p = _attn_prompt(q_p, k_p, v_p)
    tiles_per_seq = sp // tile_p
    mod_p3 = mod_p.reshape(bp * 6, 1, D_MODEL)
    mod_spec_p = lambda j: pl.BlockSpec((None, 1, D_MODEL),
                                        lambda i: ((i // tiles_per_seq) * 6 + j, 0, 0))
    x1_p, u2t_p, route_p = _outproj(
        ya_p.reshape(bp * sp, D_A), yb_p.reshape(bp * sp, D_B), x_prompt.reshape(bp * sp, D_MODEL),
        (mod_p3, mod_spec_p(2), mod_spec_p(3), mod_spec_p(4)), *outproj_weights, tile_p)

    cos_s, sin_s = _rope_tables(PAST_LEN + jnp.arange(ts))
    tb = lambda t: jnp.broadcast_to(jnp.tile(t, (1, N_HEAD_PAIRS))[:, None, :], (ts, bs, D_B))
    ya_s, q_s, k_s, v_s, conv_s, h_s = _inproj_sample(
        x_sample.transpose(1, 0, 2), mod_s, w_in_bf, tb(cos_s), tb(sin_s),
        state_conv.transpose(1, 0, 2), state_rglru, rg)
    bt = lambda t: t.transpose(1, 0, 2)

    ffn_p, yb_s, kwin_s, vwin_s = _moe(
        u2t_p, route_p[:, 0, :].reshape(-1), *expert_weights, 256,
        bt(q_s), bt(k_s), bt(v_s), cache_win_k.transpose(0, 2, 3, 1),
        cache_win_v.transpose(0, 2, 3, 1))
    y_p = _final_norm(x1_p, ffn_p, mod_p3, mod_spec_p(5), *ln2, tile_p)

    mod_spec_s = lambda j: pl.BlockSpec((bs, D_MODEL), lambda i: (0, j))
    x1_s, u2t_s, _ = _outproj(
        ya_s.reshape(ts * bs, D_A), bt(yb_s).reshape(ts * bs, D_B),
        x_sample.transpose(1, 0, 2).reshape(ts * bs, D_MODEL),
        (mod_s, mod_spec_s(2), mod_spec_s(3), mod_spec_s(4)), *outproj_weights, bs)
    ffn_s = _moe_dense(u2t_s, *expert_weights)
    y_s = _final_norm(x1_s, ffn_s, mod_s, mod_spec_s(5), *ln2, bs)

    heads = lambda t: t.reshape(t.shape[0], t.shape[1], N_HEADS_B, HEAD_DIM)
    return (y_p.reshape(bp, sp, D_MODEL), bt(y_s.reshape(ts, bs, D_MODEL)),
            conv_p, h_p.reshape(bp, D_A), heads(kwin_p), heads(vwin_p),
            bt(conv_s), h_s, kwin_s.transpose(0, 3, 1, 2), vwin_s.transpose(0, 3, 1, 2))
```

```python
import functools
import math

import jax
import jax.numpy as jnp
from jax import lax
from jax.experimental import pallas as pl
from jax.experimental.pallas import tpu as pltpu

F32 = jnp.float32
BF16 = jnp.bfloat16

D_MODEL = 1024
D_A = 512
N_BLK_A = 8
BLK_W_A = D_A // N_BLK_A
CONV_W = 4
RG_C = 8.0
D_B = 512
HEAD_DIM = 64
N_HEADS_B = D_B // HEAD_DIM
DILATED_CFG = ((128, 1), (512, 4), (2048, 16))
WIN_MAX = 2048
N_KEYS = 128
ROPE_THETA = 10000.0
PAST_LEN = 8192
N_GROUPS = 4
N_EXP_PER_GROUP = 4
N_EXPERTS = N_GROUPS * N_EXP_PER_GROUP
D_EXPERT = 512
DN_ALPHA = 2.0 ** 0.25
LN_EPS = 1e-5
NEG_INF = -1e30

N_PAIRS = 6
N_CLASSES = N_GROUPS * N_PAIRS
LANES = 128
SUBLANES = 8
HEAD_PAIR_W = 2 * HEAD_DIM
N_HEAD_PAIRS = N_HEADS_B // 2
Q_BLK = 128
ATTN_UNROLL = 32
DMA_UNROLL = 8
INPROJ_PARTS = 1
ROUTE_W = LANES
TOKEN_ROWS = D_MODEL // LANES
TOKEN_PITCH = TOKEN_ROWS + 1
N_BUF = 3
VMEM_LIMIT = 60 * 1024 * 1024


def _cparams(sem):
    return pltpu.CompilerParams(dimension_semantics=sem, vmem_limit_bytes=VMEM_LIMIT)


def _mod_kernel(c_ref, w_ref, b_ref, o_ref):
    c = c_ref[...]
    s = (c * jax.nn.sigmoid(c)).astype(BF16)
    o_ref[...] = jnp.dot(s, w_ref[...].astype(BF16), preferred_element_type=F32) + b_ref[...]


def _modulation(c_all, w_ada, b_ada):
    n = c_all.shape[0]
    tn = 1024
    return pl.pallas_call(
        _mod_kernel,
        grid=(6 * D_MODEL // tn,),
        in_specs=[pl.BlockSpec((n, D_MODEL), lambda j: (0, 0)),
                  pl.BlockSpec((D_MODEL, tn), lambda j: (0, j)),
                  pl.BlockSpec((1, tn), lambda j: (0, j))],
        out_specs=pl.BlockSpec((n, tn), lambda j: (0, j)),
        out_shape=jax.ShapeDtypeStruct((n, 6 * D_MODEL), F32),
        compiler_params=_cparams(("arbitrary",)),
        name="adaln_mod",
    )(c_all, w_ada, b_ada.reshape(1, -1))


def _rope_apply(t, cos, sin_signed):
    lane = lax.broadcasted_iota(jnp.int32, t.shape, t.ndim - 1)
    first_half = (lane & (HEAD_DIM - 1)) < HEAD_DIM // 2
    width = t.shape[-1]
    swapped = jnp.where(first_half,
                        pltpu.roll(t, width - HEAD_DIM // 2, t.ndim - 1),
                        pltpu.roll(t, HEAD_DIM // 2, t.ndim - 1))
    return t * cos + swapped * sin_signed


def _rglru_gates(xc, wg_ref, b_a, b_x, lam):
    half = D_A // 2
    xcb = xc.astype(BF16)
    g0 = jnp.dot(xcb[:, :half], wg_ref[0], preferred_element_type=F32)
    g1 = jnp.dot(xcb[:, half:], wg_ref[1], preferred_element_type=F32)
    r = jax.nn.sigmoid(jnp.concatenate([g0[:, :half], g1[:, :half]], axis=1) + b_a)
    i = jax.nn.sigmoid(jnp.concatenate([g0[:, half:], g1[:, half:]], axis=1) + b_x)
    z = -lam
    softplus = jnp.maximum(z, 0.0) + jnp.log1p(jnp.exp(-jnp.abs(z)))
    log_a = -RG_C * r * softplus
    a = jnp.exp(log_a)
    one_minus_a2 = -jnp.tanh(log_a) * (a * a + 1.0)
    u = jnp.sqrt(one_minus_a2) * (i * xc)
    return a, u


def _store_token_tiles(ref, x, pitch=TOKEN_ROWS):
    n = x.shape[0]
    for c in range(TOKEN_ROWS):
        ref[pl.ds(c, n, stride=pitch), :] = x[:, c * LANES:(c + 1) * LANES]


def _load_token_tiles(ref, n, pitch=TOKEN_ROWS):
    return jnp.concatenate(
        [ref[pl.ds(c, n, stride=pitch), :] for c in range(TOKEN_ROWS)], axis=1)


def _rms_norm(y, g):
    return y * lax.rsqrt(jnp.mean(y * y, axis=-1, keepdims=True) + LN_EPS) * g


def _layer_norm(x, g, b):
    mu = jnp.mean(x, axis=-1, keepdims=True)
    xc = x - mu
    var = jnp.mean(xc * xc, axis=-1, keepdims=True)
    return xc * lax.rsqrt(var + LN_EPS) * g + b


def _inproj_prompt_kernel(x_ref, mod_ref, w_in_ref, cos_ref, sin_ref, conv_w_ref, conv_b_ref,
                          wg_ref, b_a_ref, b_x_ref, lam_ref, gna_ref,
                          ya_ref, q_ref, k_ref, v_ref, conv_out_ref, h_out_ref,
                          xp_buf, h_carry):
    t = pl.program_id(1)
    rows = x_ref.shape[0]
    pad = SUBLANES

    @pl.when(t == 0)
    def _():
        xp_buf[0:pad, :] = jnp.zeros((pad, D_A), F32)
        h_carry[...] = jnp.zeros_like(h_carry)

    row = lax.broadcasted_iota(jnp.int32, (SUBLANES, D_A), 0)

    def part(r0, n, h_prev):
        sl = slice(r0, r0 + n)
        u = (x_ref[sl, :] * (1.0 + mod_ref[1:2, :]) + mod_ref[0:1, :]).astype(BF16)

        def proj(j):
            return jnp.dot(u, w_in_ref[:, j * D_A:(j + 1) * D_A], preferred_element_type=F32)

        cos = jnp.concatenate([cos_ref[sl, :]] * N_HEAD_PAIRS, axis=1)
        sin = jnp.concatenate([sin_ref[sl, :]] * N_HEAD_PAIRS, axis=1)
        q_ref[sl, :] = _rope_apply(proj(2), cos, sin)
        k_ref[sl, :] = _rope_apply(proj(3), cos, sin)
        v_ref[sl, :] = proj(4)

        xa = proj(0)
        xp_buf[pad + r0:pad + r0 + n, :] = xa
        xc = conv_b_ref[...] + xa * conv_w_ref[CONV_W - 1:CONV_W, :]
        for j in range(CONV_W - 1):
            off = pad - (CONV_W - 1) + j + r0
            xc = xc + xp_buf[off:off + n, :] * conv_w_ref[j:j + 1, :]

        a, u_in = _rglru_gates(xc, wg_ref, b_a_ref[...], b_x_ref[...], lam_ref[...])

        hs = []
        for g in range(n // SUBLANES):
            ag = a[g * SUBLANES:(g + 1) * SUBLANES]
            ug = u_in[g * SUBLANES:(g + 1) * SUBLANES]
            for sh in (1, 2, 4):
                keep = row >= sh
                a_sh = pltpu.roll(ag, sh, 0)
                u_sh = pltpu.roll(ug, sh, 0)
                ug = jnp.where(keep, ag * u_sh + ug, ug)
                ag = jnp.where(keep, ag * a_sh, ag)
            hg = ag * h_prev + ug
            hs.append(hg)
            h_prev = hg[SUBLANES - 1:SUBLANES, :]

        y = jnp.concatenate(hs, axis=0) * jax.nn.gelu(proj(1))
        ya_ref[sl, :] = _rms_norm(y, gna_ref[...]).astype(BF16)
        return h_prev

    h_last = h_carry[...]
    n_part = rows // INPROJ_PARTS
    for p in range(INPROJ_PARTS):
        h_last = part(p * n_part, n_part, h_last)
    h_carry[...] = h_last
    h_out_ref[...] = h_last

    tail = xp_buf[rows + pad - (CONV_W - 1):rows + pad, :]
    conv_out_ref[...] = tail
    xp_buf[pad - (CONV_W - 1):pad, :] = tail


def _inproj_prompt(x, mod3, w_in_bf, cos_t, sin_t, rg):
    b, s, _ = x.shape
    tile = 512
    nt = s // tile
    row_spec = lambda w: pl.BlockSpec((None, tile, w), lambda i, j: (i, j, 0))
    vec = lambda r, w: pl.BlockSpec((r, w), lambda i, j: (0, 0))
    outs = pl.pallas_call(
        _inproj_prompt_kernel,
        grid=(b, nt),
        in_specs=[row_spec(D_MODEL),
                  pl.BlockSpec((None, 6, D_MODEL), lambda i, j: (i, 0, 0)),
                  vec(D_MODEL, 5 * D_A),
                  pl.BlockSpec((tile, HEAD_PAIR_W), lambda i, j: (j, 0)),
                  pl.BlockSpec((tile, HEAD_PAIR_W), lambda i, j: (j, 0)),
                  vec(CONV_W, D_A), vec(1, D_A),
                  pl.BlockSpec((2, D_A // 2, D_A), lambda i, j: (0, 0, 0)),
                  vec(1, D_A), vec(1, D_A), vec(1, D_A), vec(1, D_A)],
        out_specs=[row_spec(D_A), row_spec(D_B), row_spec(D_B), row_spec(D_B),
                   pl.BlockSpec((None, CONV_W - 1, D_A), lambda i, j: (i, 0, 0)),
                   pl.BlockSpec((None, 1, D_A), lambda i, j: (i, 0, 0))],
        out_shape=[jax.ShapeDtypeStruct((b, s, D_A), BF16),
                   jax.ShapeDtypeStruct((b, s, D_B), F32),
                   jax.ShapeDtypeStruct((b, s, D_B), F32),
                   jax.ShapeDtypeStruct((b, s, D_B), F32),
                   jax.ShapeDtypeStruct((b, CONV_W - 1, D_A), F32),
                   jax.ShapeDtypeStruct((b, 1, D_A), F32)],
        scratch_shapes=[pltpu.VMEM((tile + SUBLANES, D_A), F32),
                        pltpu.VMEM((1, D_A), F32)],
        compiler_params=_cparams(("arbitrary", "arbitrary")),
        name="inproj_prompt",
    )(x, mod3, w_in_bf, cos_t, sin_t, rg["conv_w"], rg["conv_b"], rg["w_gate"],
      rg["b_a"], rg["b_x"], rg["lam"], rg["g_norm_a"])
    return outs


def _attn_prompt_kernel(q_ref, k_ref, v_ref, o_ref, kwin_ref, vwin_ref, acc_s, m_s, l_s, bias_s):
    s = q_ref.shape[0]
    keep = kwin_ref.shape[0]
    kwin_ref[...] = k_ref[s - keep:s, :]
    vwin_ref[...] = v_ref[s - keep:s, :]

    lane = lax.broadcasted_iota(jnp.int32, (Q_BLK, HEAD_PAIR_W), 1)
    head0 = lane < HEAD_DIM
    nk = 2 * Q_BLK

    qi = lax.broadcasted_iota(jnp.int32, (Q_BLK, nk), 0)
    ki = lax.broadcasted_iota(jnp.int32, (Q_BLK, nk), 1)
    for slot in range(2):
        dist = slot * Q_BLK + qi - ki
        bias_s[slot] = jnp.where((dist >= 0) & (dist <= N_KEYS), 0.0, NEG_INF)

    def rows(start, n, d):
        return pl.ds(start, n) if d == 1 else pl.ds(start, n, stride=d)

    def unit(bi, d, nb, u):
        r = u // nb
        j = u % nb
        jk = jnp.maximum(j - 1, 0)
        start_q = r + d * Q_BLK * j
        start_k = r + d * Q_BLK * jk
        bias = bias_s[j - jk]
        qb = q_ref[rows(start_q, Q_BLK, d), :] * (HEAD_DIM ** -0.5)
        kb = k_ref[rows(start_k, nk, d), :].astype(BF16)
        vb = v_ref[rows(start_k, nk, d), :].astype(BF16)
        q2 = jnp.concatenate([jnp.where(head0, qb, 0.0), jnp.where(head0, 0.0, qb)],
                             axis=0).astype(BF16)
        sc = lax.dot_general(q2, kb, (((1,), (1,)), ((), ())), preferred_element_type=F32)
        sc = sc + jnp.concatenate([bias, bias], axis=0)
        m = jnp.max(sc, axis=-1, keepdims=True)
        p = jnp.exp(sc - m)
        l = jnp.sum(p, axis=-1, keepdims=True)
        o = jnp.dot(p.astype(BF16), vb, preferred_element_type=F32)
        dst = rows(start_q, Q_BLK, d)
        acc_s[bi, dst, :] = jnp.where(head0, o[0:Q_BLK], o[Q_BLK:])
        m_s[bi, dst, :] = jnp.where(head0, m[0:Q_BLK], m[Q_BLK:])
        l_s[bi, dst, :] = jnp.where(head0, l[0:Q_BLK], l[Q_BLK:])

    for bi, (_, d) in enumerate(DILATED_CFG):
        nb = s // d // Q_BLK

        def body(u, carry, bi=bi, d=d, nb=nb):
            unit(bi, d, nb, u)
            return carry

        lax.fori_loop(0, d * nb, body, 0, unroll=ATTN_UNROLL)

    chunk = 512

    def merge(i, carry):
        r0 = pl.multiple_of(i * chunk, chunk)
        sl = pl.ds(r0, chunk)
        ms = [m_s[bi, sl, :] for bi in range(len(DILATED_CFG))]
        mx = functools.reduce(jnp.maximum, ms)
        num = jnp.zeros((chunk, HEAD_PAIR_W), F32)
        den = jnp.zeros((chunk, HEAD_PAIR_W), F32)
        for bi in range(len(DILATED_CFG)):
            w = jnp.exp(ms[bi] - mx)
            num = num + w * acc_s[bi, sl, :]
            den = den + w * l_s[bi, sl, :]
        o_ref[sl, :] = num / den
        return carry

    lax.fori_loop(0, s // chunk, merge, 0)


def _attn_prompt(q, k, v):
    b, s, _ = q.shape
    keep = min(WIN_MAX, s)
    nbr = len(DILATED_CFG)
    spec = pl.BlockSpec((None, s, HEAD_PAIR_W), lambda i, j: (i, 0, j))
    wspec = pl.BlockSpec((None, keep, HEAD_PAIR_W), lambda i, j: (i, 0, j))
    return pl.pallas_call(
        _attn_prompt_kernel,
        grid=(b, N_HEAD_PAIRS),
        in_specs=[spec, spec, spec],
        out_specs=[spec, wspec, wspec],
        out_shape=[jax.ShapeDtypeStruct((b, s, D_B), F32),
                   jax.ShapeDtypeStruct((b, keep, D_B), F32),
                   jax.ShapeDtypeStruct((b, keep, D_B), F32)],
        scratch_shapes=[pltpu.VMEM((nbr, s, HEAD_PAIR_W), F32),
                        pltpu.VMEM((nbr, s, HEAD_PAIR_W), F32),
                        pltpu.VMEM((nbr, s, HEAD_PAIR_W), F32),
                        pltpu.VMEM((2, Q_BLK, 2 * Q_BLK), F32)],
        compiler_params=_cparams(("arbitrary", "arbitrary")),
        name="attn_prompt",
    )(q, k, v)


def _inproj_sample_kernel(x_ref, mod_ref, w_in_ref, cos_ref, sin_ref, conv_state_ref, h0_ref,
                          conv_w_ref, conv_b_ref, wg_ref, b_a_ref, b_x_ref, lam_ref, gna_ref,
                          ya_ref, q_ref, k_ref, v_ref, conv_out_ref, h_out_ref):
    nt, nb, _ = x_ref.shape
    sh1 = mod_ref[:, 0:D_MODEL]
    sc1 = mod_ref[:, D_MODEL:2 * D_MODEL]
    u = (x_ref[...] * (1.0 + sc1)[None] + sh1[None]).astype(BF16).reshape(nt * nb, D_MODEL)

    def proj(j):
        return jnp.dot(u, w_in_ref[:, j * D_A:(j + 1) * D_A], preferred_element_type=F32)

    cos = cos_ref[...].reshape(nt * nb, D_B)
    sin = sin_ref[...].reshape(nt * nb, D_B)
    q_ref[...] = _rope_apply(proj(2), cos, sin).reshape(nt, nb, D_B)
    k_ref[...] = _rope_apply(proj(3), cos, sin).reshape(nt, nb, D_B)
    v_ref[...] = proj(4).reshape(nt, nb, D_B)

    xa = proj(0).reshape(nt, nb, D_A)
    xp = [conv_state_ref[j] for j in range(CONV_W - 1)] + [xa[t] for t in range(nt)]
    xc = jnp.concatenate(
        [conv_b_ref[...] + sum(xp[t + j] * conv_w_ref[j:j + 1, :] for j in range(CONV_W))
         for t in range(nt)], axis=0)
    for j in range(CONV_W - 1):
        conv_out_ref[j] = xp[nt + j]

    a, u_in = _rglru_gates(xc, wg_ref, b_a_ref[...], b_x_ref[...], lam_ref[...])
    h = h0_ref[...]
    hs = []
    for t in range(nt):
        h = a[t * nb:(t + 1) * nb] * h + u_in[t * nb:(t + 1) * nb]
        hs.append(h)
    h_out_ref[...] = h
    y = jnp.concatenate(hs, axis=0) * jax.nn.gelu(proj(1))
    ya_ref[...] = _rms_norm(y, gna_ref[...]).astype(BF16).reshape(nt, nb, D_A)


def _inproj_sample(x_tb, mod_s, w_in_bf, cos_t, sin_t, conv_state_tb, h0, rg):
    nt, nb, _ = x_tb.shape
    full = lambda shape: pl.BlockSpec(shape, lambda i: (0,) * len(shape))
    return pl.pallas_call(
        _inproj_sample_kernel,
        grid=(1,),
        in_specs=[full((nt, nb, D_MODEL)), full((nb, 6 * D_MODEL)), full((D_MODEL, 5 * D_A)),
                  full((nt, nb, D_B)), full((nt, nb, D_B)),
                  full((CONV_W - 1, nb, D_A)), full((nb, D_A)),
                  full((CONV_W, D_A)), full((1, D_A)), full((2, D_A // 2, D_A)),
                  full((1, D_A)), full((1, D_A)), full((1, D_A)), full((1, D_A))],
        out_specs=[full((nt, nb, D_A)), full((nt, nb, D_B)), full((nt, nb, D_B)),
                   full((nt, nb, D_B)), full((CONV_W - 1, nb, D_A)), full((nb, D_A))],
        out_shape=[jax.ShapeDtypeStruct((nt, nb, D_A), BF16),
                   jax.ShapeDtypeStruct((nt, nb, D_B), F32),
                   jax.ShapeDtypeStruct((nt, nb, D_B), F32),
                   jax.ShapeDtypeStruct((nt, nb, D_B), F32),
                   jax.ShapeDtypeStruct((CONV_W - 1, nb, D_A), F32),
                   jax.ShapeDtypeStruct((nb, D_A), F32)],
        compiler_params=_cparams(("arbitrary",)),
        name="inproj_sample",
    )(x_tb, mod_s, w_in_bf, cos_t, sin_t, conv_state_tb, h0, rg["conv_w"], rg["conv_b"],
      rg["w_gate"], rg["b_a"], rg["b_x"], rg["lam"], rg["g_norm_a"])


def _sample_attention_step(b, n_seq, q_ref, kn_ref, vn_ref, ck_hbm, cv_hbm, o_ref, kwin_hbm,
                           vwin_hbm, cin, cout, sem_in, sem_out, kn_pad, vn_pad):
    nt = q_ref.shape[0]
    n_buf = cin.shape[-1]
    n_rows = N_HEADS_B * nt
    srcs = (ck_hbm, cv_hbm)
    dsts = (kwin_hbm, vwin_hbm)
    slot = b % 2

    def copy_in(which, seq):
        s = seq % 2
        return pltpu.make_async_copy(srcs[which].at[seq], cin.at[s, which], sem_in.at[s, which])

    def copy_out(which, seq):
        s = seq % 2
        return pltpu.make_async_copy(cout.at[s, which], dsts[which].at[seq],
                                     sem_out.at[s, which])

    @pl.when(b == 0)
    def _():
        kn_pad[...] = jnp.zeros_like(kn_pad)
        vn_pad[...] = jnp.zeros_like(vn_pad)
        copy_in(0, 0).start()
        copy_in(1, 0).start()

    @pl.when(b + 1 < n_seq)
    def _():
        copy_in(0, b + 1).start()
        copy_in(1, b + 1).start()

    kn_pad[0:nt, :] = kn_ref[...]
    vn_pad[0:nt, :] = vn_ref[...]

    tail_lane = lax.broadcasted_iota(jnp.int32, (HEAD_DIM, LANES), 1)

    def shift_in(old_ref, new_pad, out_ref):
        new_t = pltpu.roll(new_pad.T, LANES - nt, 1)
        for h in range(N_HEADS_B):
            rolled = pltpu.roll(old_ref[h], n_buf - nt, 1)
            last = jnp.where(tail_lane < LANES - nt, rolled[:, n_buf - LANES:n_buf],
                             new_t[h * HEAD_DIM:(h + 1) * HEAD_DIM])
            out_ref[h, :, 0:n_buf - LANES] = rolled[:, 0:n_buf - LANES]
            out_ref[h, :, n_buf - LANES:n_buf] = last

    copy_in(0, b).wait()
    copy_in(1, b).wait()

    @pl.when(b >= 2)
    def _():
        copy_out(0, b - 2).wait()
        copy_out(1, b - 2).wait()

    ri = lax.broadcasted_iota(jnp.int32, (n_rows, nt), 0)
    ci = lax.broadcasted_iota(jnp.int32, (n_rows, nt), 1)
    pick = (ri % nt == ci).astype(BF16)
    qs = (q_ref[...] * (HEAD_DIM ** -0.5)).astype(BF16)
    q_rep = jnp.dot(pick, qs, preferred_element_type=F32)
    row_h = lax.broadcasted_iota(jnp.int32, (n_rows, D_B), 0) // nt
    lane_h = lax.broadcasted_iota(jnp.int32, (n_rows, D_B), 1) // HEAD_DIM
    own = row_h == lane_h
    qbd = jnp.where(own, q_rep, 0.0).astype(BF16)

    nt_dims = (((1,), (1,)), ((), ()))

    def mult(dist, limit_ok):
        c = jnp.zeros(dist.shape, F32)
        for win, d in DILATED_CFG:
            hit = (dist >= 0) & (dist <= win) & (dist % d == 0) & limit_ok
            c = c + hit.astype(F32)
        return c

    def weights(ck_t):
        sc_c = jnp.dot(qbd, ck_t.astype(BF16), preferred_element_type=F32)
        sc_n = lax.dot_general(qbd, kn_pad[...].astype(BF16), nt_dims,
                               preferred_element_type=F32)
        t_c = lax.broadcasted_iota(jnp.int32, sc_c.shape, 0) % nt
        dist_c = n_buf + t_c - lax.broadcasted_iota(jnp.int32, sc_c.shape, 1)
        mult_c = mult(dist_c, dist_c >= 0)
        t_n = lax.broadcasted_iota(jnp.int32, sc_n.shape, 0) % nt
        col_n = lax.broadcasted_iota(jnp.int32, sc_n.shape, 1)
        mult_n = mult(t_n - col_n, col_n < nt)
        sc_c = jnp.where(mult_c > 0, sc_c, NEG_INF)
        sc_n = jnp.where(mult_n > 0, sc_n, NEG_INF)
        m = jnp.maximum(jnp.max(sc_c, axis=-1, keepdims=True),
                        jnp.max(sc_n, axis=-1, keepdims=True))
        p_c = mult_c * jnp.exp(sc_c - m)
        p_n = mult_n * jnp.exp(sc_n - m)
        l = jnp.sum(p_c, axis=-1, keepdims=True) + jnp.sum(p_n, axis=-1, keepdims=True)
        return p_c.astype(BF16), p_n.astype(BF16), l

    p_c, p_n, l = weights(cin[slot, 0].reshape(D_B, n_buf))
    shift_in(cin.at[slot, 0], kn_pad[...], cout.at[slot, 0])
    acc = (lax.dot_general(p_c, cin[slot, 1].reshape(D_B, n_buf).astype(BF16), nt_dims,
                           preferred_element_type=F32)
           + jnp.dot(p_n, vn_pad[...].astype(BF16), preferred_element_type=F32))
    shift_in(cin.at[slot, 1], vn_pad[...], cout.at[slot, 1])
    o_full = jnp.where(own, acc / l, 0.0)
    out = o_full[0:nt, :]
    for h in range(1, N_HEADS_B):
        out = out + o_full[h * nt:(h + 1) * nt, :]
    o_ref[...] = out

    copy_out(0, b).start()
    copy_out(1, b).start()

    @pl.when(b == n_seq - 1)
    def _():
        @pl.when(b >= 1)
        def _():
            copy_out(0, b - 1).wait()
            copy_out(1, b - 1).wait()
        copy_out(0, b).wait()
        copy_out(1, b).wait()


def _split_bf16(x):
    hi = x.astype(BF16)
    lo = (x - hi.astype(F32)).astype(BF16)
    return hi, lo


def _first_argmax(vals):
    mx = functools.reduce(jnp.maximum, vals)
    idx = jnp.full(mx.shape, float(len(vals) - 1), F32)
    for j in range(len(vals) - 2, -1, -1):
        idx = jnp.where(vals[j] == mx, float(j), idx)
    return mx, idx


def _outproj_kernel(ya_ref, yb_ref, x_ref, gt1_ref, sh2_ref, sc2_ref, w_out_ref, gnb_ref,
                    ln_g_ref, ln_b_ref, w_r_ref, b_r_ref, x1_ref, u2_ref, route_ref):
    yb = _rms_norm(yb_ref[...], gnb_ref[...]).astype(BF16)
    mixed = (jnp.dot(ya_ref[...], w_out_ref[0:D_A, :], preferred_element_type=F32)
             + jnp.dot(yb, w_out_ref[D_A:D_A + D_B, :], preferred_element_type=F32))
    x1 = _layer_norm(DN_ALPHA * x_ref[...] + gt1_ref[...] * mixed, ln_g_ref[...], ln_b_ref[...])
    x1_ref[...] = x1
    u2 = x1 * (1.0 + sc2_ref[...]) + sh2_ref[...]
    _store_token_tiles(u2_ref, u2, TOKEN_PITCH)

    u_hi, u_lo = _split_bf16(u2)
    w_hi, w_lo = _split_bf16(w_r_ref[...])
    both = jnp.dot(u_hi, jnp.concatenate([w_hi, w_lo], axis=1), preferred_element_type=F32)
    logits = (both[:, 0:ROUTE_W] + jnp.dot(u_lo, w_hi, preferred_element_type=F32)
              + both[:, ROUTE_W:2 * ROUTE_W]) + b_r_ref[...]
    lt = logits.T

    g_rows = [lt[j:j + 1, :] for j in range(N_GROUPS)]
    g_max, g_idx = _first_argmax(g_rows)
    p_group = 1.0 / sum(jnp.exp(g - g_max) for g in g_rows)
    e_rows = []
    for e in range(N_EXP_PER_GROUP):
        acc = jnp.zeros_like(g_max)
        for g in range(N_GROUPS):
            r = N_GROUPS + g * N_EXP_PER_GROUP + e
            acc = jnp.where(g_idx == float(g), lt[r:r + 1, :], acc)
        e_rows.append(acc)
    v1, i1 = _first_argmax(e_rows)
    rest = [jnp.where(i1 == float(e), -jnp.inf, e_rows[e]) for e in range(N_EXP_PER_GROUP)]
    v2, i2 = _first_argmax(rest)
    ex = jnp.exp(v2 - v1)
    w1 = p_group / (1.0 + ex)
    w2 = p_group * ex / (1.0 + ex)
    lo = jnp.minimum(i1, i2)
    hi = jnp.maximum(i1, i2)
    pair = jnp.where(lo == 0.0, hi - 1.0, jnp.where(lo == 1.0, hi + 1.0, 5.0))
    cls = g_idx * float(N_PAIRS) + pair
    w_of_lo = jnp.where(i1 < i2, w1, w2)
    w_of_hi = jnp.where(i1 < i2, w2, w1)
    e_lo = g_idx * float(N_EXP_PER_GROUP) + lo
    e_hi = g_idx * float(N_EXP_PER_GROUP) + hi
    n_tok = cls.shape[1]
    route = jnp.concatenate(
        [cls, w_of_lo, w_of_hi, e_lo, e_hi, jnp.zeros((ROUTE_W - 5, n_tok), F32)], axis=0)
    route_ref[...] = route[0:SUBLANES, :]
    u2_ref[pl.ds(TOKEN_ROWS, n_tok, stride=TOKEN_PITCH), :] = route.T


def _outproj(ya, yb, x, mods, w_out_bf, gnb, ln_g, ln_b, w_r, b_r, tile):
    n = x.shape[0]
    nt = n // tile
    mod_arr, gt1_spec, sh2_spec, sc2_spec = mods
    row = lambda w: pl.BlockSpec((tile, w), lambda i: (i, 0))
    vec = lambda r, w: pl.BlockSpec((r, w), lambda i: (0, 0))
    return pl.pallas_call(
        _outproj_kernel,
        grid=(nt,),
        in_specs=[row(D_A), row(D_B), row(D_MODEL), gt1_spec, sh2_spec, sc2_spec,
                  vec(D_MODEL, D_MODEL), vec(1, D_B), vec(1, D_MODEL), vec(1, D_MODEL),
                  vec(D_MODEL, ROUTE_W), vec(1, ROUTE_W)],
        out_specs=[row(D_MODEL),
                   pl.BlockSpec((tile * TOKEN_PITCH, LANES), lambda i: (i, 0)),
                   pl.BlockSpec((None, SUBLANES, tile), lambda i: (i, 0, 0))],
        out_shape=[jax.ShapeDtypeStruct((n, D_MODEL), F32),
                   jax.ShapeDtypeStruct((n * TOKEN_PITCH, LANES), F32),
                   jax.ShapeDtypeStruct((nt, SUBLANES, tile), F32)],
        compiler_params=_cparams(("arbitrary",)),
        name="outproj_router",
    )(ya, yb, x, mod_arr, mod_arr, mod_arr, w_out_bf, gnb, ln_g, ln_b, w_r, b_r)


def _moe_kernel(e_lo_ref, e_hi_ref, n_used_ref, src_ref, dst_ref,
                x_hbm, wg_lo, wg_hi, wu_lo, wu_hi, wd_lo, wd_hi,
                q_ref, kn_ref, vn_ref, ck_hbm, cv_hbm,
                o_hbm, yb_ref, kwin_hbm, vwin_hbm,
                xg, og, gsem, ssem, cin, cout, csem_in, csem_out, kn_pad, vn_pad,
                *, n_tokens, tile, n_seq):
    i = pl.program_id(0)
    n_steps = pl.num_programs(0)
    n_used = n_used_ref[0]
    cur = i % N_BUF
    ahead = (i + 2) % N_BUF
    rows = tile * TOKEN_ROWS

    def gather_token(base, r, buf):
        pltpu.make_async_copy(x_hbm.at[pl.ds(src_ref[base + r], TOKEN_PITCH)],
                              xg.at[buf, pl.ds(r * TOKEN_PITCH, TOKEN_PITCH)],
                              gsem.at[buf]).start()

    def gather_wait(buf):
        pltpu.make_async_copy(x_hbm.at[pl.ds(0, tile * TOKEN_PITCH)], xg.at[buf],
                              gsem.at[buf]).wait()

    def scatter_wait(buf):
        pltpu.make_async_copy(og.at[buf], o_hbm.at[pl.ds(0, rows)], ssem.at[buf]).wait()

    @pl.when(i == 0)
    def _():
        def first(r, c):
            gather_token(0, r, 0)
            gather_token(jnp.minimum(1, n_steps - 1) * tile, r, 1)
            return c
        lax.fori_loop(0, tile, first, 0, unroll=DMA_UNROLL)
        og[...] = jnp.zeros_like(og)
        for buf in range(N_BUF):
            pad_rows = pltpu.make_async_copy(
                og.at[buf], o_hbm.at[pl.ds((n_tokens + buf * tile) * TOKEN_ROWS, rows)],
                ssem.at[buf])
            pad_rows.start()
            pad_rows.wait()

    def expert_tile():
        nxt = jnp.minimum(i + 2, n_steps - 1) * tile
        for r in range(tile):
            gather_token(nxt, r, ahead)
        prev = i * tile
        for r in range(tile):
            row = pl.multiple_of(dst_ref[prev + r], TOKEN_ROWS)
            pltpu.make_async_copy(og.at[ahead, pl.ds(r * TOKEN_ROWS, TOKEN_ROWS)],
                                  o_hbm.at[pl.ds(row, TOKEN_ROWS)], ssem.at[ahead]).start()

        xb = _load_token_tiles(xg.at[cur], tile, TOKEN_PITCH).astype(BF16)
        w2 = xg[cur, pl.ds(TOKEN_ROWS, tile, stride=TOKEN_PITCH), :]
        out = jnp.zeros((tile, D_MODEL), F32)
        for col, wg, wu, wd in ((1, wg_lo, wu_lo, wd_lo), (2, wg_hi, wu_hi, wd_hi)):
            hg = jnp.dot(xb, wg[...], preferred_element_type=F32)
            hu = jnp.dot(xb, wu[...], preferred_element_type=F32)
            act = (hg * jax.nn.sigmoid(hg)) * hu * w2[:, col:col + 1]
            out = out + jnp.dot(act.astype(BF16), wd[...], preferred_element_type=F32)
        _store_token_tiles(og.at[cur], out)

    @pl.when(i <= n_used)
    def _():
        gather_wait(cur)

        @pl.when(i >= 2)
        def _():
            scatter_wait(cur)

        expert_tile()

        @pl.when(i == n_used)
        def _():
            for buf in range(N_BUF):
                @pl.when(buf != cur)
                def _():
                    gather_wait(buf)
                    scatter_wait(buf)

    @pl.when(i < n_seq)
    def _():
        _sample_attention_step(i, n_seq, q_ref, kn_ref, vn_ref, ck_hbm, cv_hbm, yb_ref,
                               kwin_hbm, vwin_hbm, cin, cout, csem_in, csem_out, kn_pad, vn_pad)


def _moe(u2t, cls, w_gate_bf, w_up_bf, w_down_bf, tile, q_s, k_new, v_new, cache_k_t, cache_v_t):
    n = cls.shape[0]
    n_seq, nt, _ = q_s.shape
    n_buf = cache_k_t.shape[-1]
    n_steps = n // tile + N_CLASSES
    assert n_steps >= n_seq, "one sample sequence per grid step"
    cls = cls.astype(jnp.int32)
    order = jnp.argsort(cls, stable=True).astype(jnp.int32)
    class_ids = jnp.arange(N_CLASSES, dtype=jnp.int32)
    counts = jnp.sum((cls[:, None] == class_ids[None, :]).astype(jnp.int32), axis=0)
    tiles_per = (counts + tile - 1) // tile
    tile_end = jnp.cumsum(tiles_per)
    tile_off = tile_end - tiles_per
    n_used = tile_end[-1]
    class_start = jnp.cumsum(counts) - counts
    step = jnp.arange(n_steps, dtype=jnp.int32)
    step_c = jnp.minimum(step, n_used - 1)
    cls_of = jnp.sum((step_c[:, None] >= tile_end[None, :]).astype(jnp.int32), axis=1)
    onehot = (cls_of[:, None] == class_ids[None, :]).astype(jnp.int32)
    pick = lambda table: jnp.sum(onehot * table[None, :], axis=1)
    local = step - pick(tile_off)
    nvalid = jnp.where(step < n_used, jnp.clip(pick(counts) - local * tile, 0, tile), 0)
    r = jnp.arange(tile, dtype=jnp.int32)
    pos = pick(class_start)[:, None] + local[:, None] * tile + r[None, :]
    valid = r[None, :] < nvalid[:, None]
    tok = order[jnp.clip(pos, 0, n - 1)]
    src = (jnp.where(valid, tok, 0) * TOKEN_PITCH).astype(jnp.int32).reshape(-1)
    spare = n + (step[:, None] % N_BUF) * tile + r[None, :]
    dst = jnp.where(valid, tok, spare)
    dst = jnp.concatenate([(n + (N_BUF - 1) * tile + r)[None, :], dst], axis=0)
    dst = (dst * TOKEN_ROWS).astype(jnp.int32).reshape(-1)
    grp = cls_of // N_PAIRS
    pair = cls_of % N_PAIRS
    pair_lo = (pair >= 3).astype(jnp.int32) + (pair >= 5).astype(jnp.int32)
    pair_hi = pair + 1 - 2 * (pair >= 3).astype(jnp.int32) - (pair >= 5).astype(jnp.int32)
    e_lo = (grp * N_EXP_PER_GROUP + pair_lo).astype(jnp.int32)
    e_hi = (grp * N_EXP_PER_GROUP + pair_hi).astype(jnp.int32)

    w_in_spec = lambda which: pl.BlockSpec(
        (None, D_MODEL, D_EXPERT), lambda i, elo, ehi, nu, s, d: ((elo, ehi)[which][i], 0, 0))
    w_dn_spec = lambda which: pl.BlockSpec(
        (None, D_EXPERT, D_MODEL), lambda i, elo, ehi, nu, s, d: ((elo, ehi)[which][i], 0, 0))
    any_spec = pl.BlockSpec(memory_space=pl.ANY)
    seq_spec = pl.BlockSpec((None, nt, D_B),
                            lambda i, elo, ehi, nu, s, d: (jnp.minimum(i, n_seq - 1), 0, 0))
    win_shape = jax.ShapeDtypeStruct((n_seq, N_HEADS_B, HEAD_DIM, n_buf), F32)
    grid_spec = pltpu.PrefetchScalarGridSpec(
        num_scalar_prefetch=5,
        grid=(n_steps,),
        in_specs=[any_spec,
                  w_in_spec(0), w_in_spec(1), w_in_spec(0), w_in_spec(1),
                  w_dn_spec(0), w_dn_spec(1),
                  seq_spec, seq_spec, seq_spec, any_spec, any_spec],
        out_specs=[any_spec, seq_spec, any_spec, any_spec],
        scratch_shapes=[pltpu.VMEM((N_BUF, tile * TOKEN_PITCH, LANES), F32),
                        pltpu.VMEM((N_BUF, tile * TOKEN_ROWS, LANES), F32),
                        pltpu.SemaphoreType.DMA((N_BUF,)),
                        pltpu.SemaphoreType.DMA((N_BUF,)),
                        pltpu.VMEM((2, 2, N_HEADS_B, HEAD_DIM, n_buf), F32),
                        pltpu.VMEM((2, 2, N_HEADS_B, HEAD_DIM, n_buf), F32),
                        pltpu.SemaphoreType.DMA((2, 2)),
                        pltpu.SemaphoreType.DMA((2, 2)),
                        pltpu.VMEM((LANES, D_B), F32),
                        pltpu.VMEM((LANES, D_B), F32)],
    )
    return pl.pallas_call(
        functools.partial(_moe_kernel, n_tokens=n, tile=tile, n_seq=n_seq),
        grid_spec=grid_spec,
        out_shape=[jax.ShapeDtypeStruct(((n + N_BUF * tile) * TOKEN_ROWS, LANES), F32),
                   jax.ShapeDtypeStruct((n_seq, nt, D_B), F32), win_shape, win_shape],
        compiler_params=_cparams(("arbitrary",)),
        name="moe_sparse",
    )(e_lo, e_hi, n_used.reshape(1).astype(jnp.int32), src, dst,
      u2t, w_gate_bf, w_gate_bf, w_up_bf, w_up_bf, w_down_bf, w_down_bf,
      q_s, k_new, v_new, cache_k_t, cache_v_t)


def _moe_dense_kernel(x_ref, wg_ref, wu_ref, wd_ref, o_ref):
    e = pl.program_id(0)
    n = o_ref.shape[0]

    @pl.when(e == 0)
    def _():
        o_ref[...] = jnp.zeros_like(o_ref)

    xb = _load_token_tiles(x_ref, n, TOKEN_PITCH).astype(BF16)
    ef = e.astype(F32)
    route = x_ref[pl.ds(TOKEN_ROWS, n, stride=TOKEN_PITCH), :]
    col = lambda c: route[:, c:c + 1]
    comb = jnp.where(col(3) == ef, col(1), 0.0) + jnp.where(col(4) == ef, col(2), 0.0)
    hg = jnp.dot(xb, wg_ref[...], preferred_element_type=F32)
    hu = jnp.dot(xb, wu_ref[...], preferred_element_type=F32)
    act = (hg * jax.nn.sigmoid(hg)) * hu * comb
    o_ref[...] += jnp.dot(act.astype(BF16), wd_ref[...], preferred_element_type=F32)


def _moe_dense(u2t, w_gate_bf, w_up_bf, w_down_bf):
    n = u2t.shape[0] // TOKEN_PITCH
    return pl.pallas_call(
        _moe_dense_kernel,
        grid=(N_EXPERTS,),
        in_specs=[pl.BlockSpec((n * TOKEN_PITCH, LANES), lambda e: (0, 0)),
                  pl.BlockSpec((None, D_MODEL, D_EXPERT), lambda e: (e, 0, 0)),
                  pl.BlockSpec((None, D_MODEL, D_EXPERT), lambda e: (e, 0, 0)),
                  pl.BlockSpec((None, D_EXPERT, D_MODEL), lambda e: (e, 0, 0))],
        out_specs=pl.BlockSpec((n, D_MODEL), lambda e: (0, 0)),
        out_shape=jax.ShapeDtypeStruct((n, D_MODEL), F32),
        compiler_params=_cparams(("arbitrary",)),
        name="moe_dense",
    )(u2t, w_gate_bf, w_up_bf, w_down_bf)


def _final_kernel(x1_ref, ffn_ref, gt2_ref, g_ref, b_ref, o_ref, *, token_tiled):
    rows = x1_ref.shape[0]
    ffn = _load_token_tiles(ffn_ref, rows) if token_tiled else ffn_ref[...]
    o_ref[...] = _layer_norm(DN_ALPHA * x1_ref[...] + gt2_ref[...] * ffn, g_ref[...], b_ref[...])


def _final_norm(x1, ffn, mod_arr, gt2_spec, ln_g, ln_b, tile):
    n = x1.shape[0]
    token_tiled = ffn.shape[-1] == LANES
    row = pl.BlockSpec((tile, D_MODEL), lambda i: (i, 0))
    ffn_spec = pl.BlockSpec((tile * TOKEN_ROWS, LANES), lambda i: (i, 0)) if token_tiled else row
    vec = pl.BlockSpec((1, D_MODEL), lambda i: (0, 0))
    return pl.pallas_call(
        functools.partial(_final_kernel, token_tiled=token_tiled),
        grid=(n // tile,),
        in_specs=[row, ffn_spec, gt2_spec, vec, vec],
        out_specs=row,
        out_shape=jax.ShapeDtypeStruct((n, D_MODEL), F32),
        compiler_params=_cparams(("arbitrary",)),
        name="final_norm",
    )(x1, ffn, mod_arr, ln_g, ln_b)


def _rope_tables(pos):
    half = HEAD_DIM // 2
    inv = ROPE_THETA ** (-jnp.arange(half, dtype=F32) * 2.0 / HEAD_DIM)
    ang = pos.astype(F32)[:, None] * inv[None, :]
    cos = jnp.cos(ang)
    sin = jnp.sin(ang)
    cos_t = jnp.tile(jnp.concatenate([cos, cos], axis=-1), (1, 2))
    sin_t = jnp.tile(jnp.concatenate([-sin, sin], axis=-1), (1, 2))
    return cos_t, sin_t


def _block_diag(w):
    n, a, b = w.shape
    eye = jnp.eye(n, dtype=w.dtype)
    return (eye[:, None, :, None] * w[:, :, None, :]).reshape(n * a, n * b)


def _prepare_weights(w_in, conv_w, conv_b, w_rg_a, b_rg_a, w_rg_x, b_rg_x, rg_lambda, g_norm_a,
                     w_router_group, b_router_group, w_router_expert, b_router_expert):
    half_blocks = N_BLK_A // 2
    w_gate = jnp.stack([
        jnp.concatenate([_block_diag(w_rg_a[h * half_blocks:(h + 1) * half_blocks]),
                         _block_diag(w_rg_x[h * half_blocks:(h + 1) * half_blocks])], axis=1)
        for h in range(2)]).astype(BF16)
    rg = dict(conv_w=conv_w, conv_b=conv_b.reshape(1, D_A), w_gate=w_gate,
              b_a=b_rg_a.reshape(1, D_A), b_x=b_rg_x.reshape(1, D_A),
              lam=rg_lambda.reshape(1, D_A), g_norm_a=g_norm_a.reshape(1, D_A))
    n_logits = N_GROUPS + N_EXPERTS
    w_r = jnp.concatenate(
        [w_router_group,
         w_router_expert.transpose(1, 0, 2).reshape(D_MODEL, N_EXPERTS),
         jnp.zeros((D_MODEL, ROUTE_W - n_logits), F32)], axis=1)
    b_r = jnp.concatenate([b_router_group, b_router_expert.reshape(-1),
                           jnp.zeros((ROUTE_W - n_logits,), F32)]).reshape(1, ROUTE_W)
    return rg, w_r, b_r


def kernel(x_prompt, x_sample, state_conv, state_rglru, cache_win_k, cache_win_v, c_prompt, c_sample, w_ada, b_ada, w_in, conv_w, conv_b, w_rg_a, b_rg_a, w_rg_x, b_rg_x, rg_lambda, g_norm_a, g_norm_b, w_out, ln1_g, ln1_b, w_router_group, b_router_group, w_router_expert, b_router_expert, w_exp_gate, w_exp_up, w_exp_down, ln2_g, ln2_b):
    bp, sp, _ = x_prompt.shape
    bs, ts, _ = x_sample.shape
    n_buf = cache_win_k.shape[1]

    rg, w_r, b_r = _prepare_weights(w_in, conv_w, conv_b, w_rg_a, b_rg_a, w_rg_x, b_rg_x,
                                    rg_lambda, g_norm_a, w_router_group, b_router_group,
                                    w_router_expert, b_router_expert)
    w_in_bf = w_in.astype(BF16)
    outproj_weights = (w_out.astype(BF16), g_norm_b.reshape(1, D_B), ln1_g.reshape(1, D_MODEL),
                       ln1_b.reshape(1, D_MODEL), w_r, b_r)
    expert_weights = (w_exp_gate.astype(BF16), w_exp_up.astype(BF16), w_exp_down.astype(BF16))
    ln2 = (ln2_g.reshape(1, D_MODEL), ln2_b.reshape(1, D_MODEL))

    mod = _modulation(jnp.concatenate([c_prompt, c_sample], axis=0), w_ada, b_ada)
    mod_p, mod_s = mod[:bp], mod[bp:]

    tile_p = 512
    cos_p, sin_p = _rope_tables(jnp.arange(sp))
    ya_p, q_p, k_p, v_p, conv_p, h_p = _inproj_prompt(
        x_prompt, mod_p.reshape(bp, 6, D_MODEL), w_in_bf, cos_p, sin_p, rg)
    yb_p, kwin_p, vwin_p = _attn_prompt(q_p, k_p, v_p)
    tiles_per_seq = sp // tile_p
    mod_p3 = mod_p.reshape(bp * 6, 1, D_MODEL)
    mod_spec_p = lambda j: pl.BlockSpec((None, 1, D_MODEL),
                                        lambda i: ((i // tiles_per_seq) * 6 + j, 0, 0))
    x1_p, u2t_p, route_p = _outproj(
        ya_p.reshape(bp * sp, D_A), yb_p.reshape(bp * sp, D_B), x_prompt.reshape(bp * sp, D_MODEL),
        (mod_p3, mod_spec_p(2), mod_spec_p(3), mod_spec_p(4)), *outproj_weights, tile_p)

    cos_s, sin_s = _rope_tables(PAST_LEN + jnp.arange(ts))
    tb = lambda t: jnp.broadcast_to(jnp.tile(t, (1, N_HEAD_PAIRS))[:, None, :], (ts, bs, D_B))
    ya_s, q_s, k_s, v_s, conv_s, h_s = _inproj_sample(
        x_sample.transpose(1, 0, 2), mod_s, w_in_bf, tb(cos_s), tb(sin_s),
        state_conv.transpose(1, 0, 2), state_rglru, rg)
    bt = lambda t: t.transpose(1, 0, 2)

    ffn_p, yb_s, kwin_s, vwin_s = _moe(
        u2t_p, route_p[:, 0, :].reshape(-1), *expert_weights, 256,
        bt(q_s), bt(k_s), bt(v_s), cache_win_k.transpose(0, 2, 3, 1),
        cache_win_v.transpose(0, 2, 3, 1))
    y_p = _final_norm(x1_p, ffn_p, mod_p3, mod_spec_p(5), *ln2, tile_p)

    mod_spec_s = lambda j: pl.BlockSpec((bs, D_MODEL), lambda i: (0, j))
    x1_s, u2t_s, _ = _outproj(
        ya_s.reshape(ts * bs, D_A), bt(yb_s).reshape(ts * bs, D_B),
        x_sample.transpose(1, 0, 2).reshape(ts * bs, D_MODEL),
        (mod_s, mod_spec_s(2), mod_spec_s(3), mod_spec_s(4)), *outproj_weights, bs)
    ffn_s = _moe_dense(u2t_s, *expert_weights)
    y_s = _final_norm(x1_s, ffn_s, mod_s, mod_spec_s(5), *ln2, bs)

    heads = lambda t: t.reshape(t.shape[0], t.shape[1], N_HEADS_B, HEAD_DIM)
    return (y_p.reshape(bp, sp, D_MODEL), bt(y_s.reshape(ts, bs, D_MODEL)),
            conv_p, h_p.reshape(bp, D_A), heads(kwin_p), heads(vwin_p),
            bt(conv_s), h_s, kwin_s.transpose(0, 3, 1, 2), vwin_s.transpose(0, 3, 1, 2))
```

```python
import functools
import math

import jax
import jax.numpy as jnp
from jax import lax
from jax.experimental import pallas as pl
from jax.experimental.pallas import tpu as pltpu

F32 = jnp.float32
BF16 = jnp.bfloat16

D_MODEL = 1024
D_A = 512
N_BLK_A = 8
BLK_W_A = D_A // N_BLK_A
CONV_W = 4
RG_C = 8.0
D_B = 512
HEAD_DIM = 64
N_HEADS_B = D_B // HEAD_DIM
DILATED_CFG = ((128, 1), (512, 4), (2048, 16))
WIN_MAX = 2048
N_KEYS = 128
ROPE_THETA = 10000.0
PAST_LEN = 8192
N_GROUPS = 4
N_EXP_PER_GROUP = 4
N_EXPERTS = N_GROUPS * N_EXP_PER_GROUP
D_EXPERT = 512
DN_ALPHA = 2.0 ** 0.25
LN_EPS = 1e-5
NEG_INF = -1e30

N_PAIRS = 6
N_CLASSES = N_GROUPS * N_PAIRS
LANES = 128
SUBLANES = 8
HEAD_PAIR_W = 2 * HEAD_DIM
N_HEAD_PAIRS = N_HEADS_B // 2
Q_BLK = 128
ATTN_UNROLL = 32
DMA_UNROLL = 8
INPROJ_PARTS = 1
ROUTE_W = LANES
TOKEN_ROWS = D_MODEL // LANES
TOKEN_PITCH = TOKEN_ROWS + 1
N_BUF = 3
BULK_DMA_PRIORITY = 1
VMEM_LIMIT = 60 * 1024 * 1024


def _cparams(sem):
    return pltpu.CompilerParams(dimension_semantics=sem, vmem_limit_bytes=VMEM_LIMIT)


def _mod_kernel(c_ref, w_ref, b_ref, o_ref):
    c = c_ref[...]
    s = (c * jax.nn.sigmoid(c)).astype(BF16)
    o_ref[...] = jnp.dot(s, w_ref[...].astype(BF16), preferred_element_type=F32) + b_ref[...]


def _modulation(c_all, w_ada, b_ada):
    n = c_all.shape[0]
    tn = 1024
    return pl.pallas_call(
        _mod_kernel,
        grid=(6 * D_MODEL // tn,),
        in_specs=[pl.BlockSpec((n, D_MODEL), lambda j: (0, 0)),
                  pl.BlockSpec((D_MODEL, tn), lambda j: (0, j)),
                  pl.BlockSpec((1, tn), lambda j: (0, j))],
        out_specs=pl.BlockSpec((n, tn), lambda j: (0, j)),
        out_shape=jax.ShapeDtypeStruct((n, 6 * D_MODEL), F32),
        compiler_params=_cparams(("arbitrary",)),
        name="adaln_mod",
    )(c_all, w_ada, b_ada.reshape(1, -1))


def _rope_apply(t, cos, sin_signed):
    lane = lax.broadcasted_iota(jnp.int32, t.shape, t.ndim - 1)
    first_half = (lane & (HEAD_DIM - 1)) < HEAD_DIM // 2
    width = t.shape[-1]
    swapped = jnp.where(first_half,
                        pltpu.roll(t, width - HEAD_DIM // 2, t.ndim - 1),
                        pltpu.roll(t, HEAD_DIM // 2, t.ndim - 1))
    return t * cos + swapped * sin_signed


def _rglru_gates(xc, wg_ref, b_a, b_x, lam):
    half = D_A // 2
    xcb = xc.astype(BF16)
    g0 = jnp.dot(xcb[:, :half], wg_ref[0], preferred_element_type=F32)
    g1 = jnp.dot(xcb[:, half:], wg_ref[1], preferred_element_type=F32)
    r = jax.nn.sigmoid(jnp.concatenate([g0[:, :half], g1[:, :half]], axis=1) + b_a)
    i = jax.nn.sigmoid(jnp.concatenate([g0[:, half:], g1[:, half:]], axis=1) + b_x)
    z = -lam
    softplus = jnp.maximum(z, 0.0) + jnp.log1p(jnp.exp(-jnp.abs(z)))
    log_a = -RG_C * r * softplus
    a = jnp.exp(log_a)
    one_minus_a2 = -jnp.tanh(log_a) * (a * a + 1.0)
    u = jnp.sqrt(one_minus_a2) * (i * xc)
    return a, u


def _store_token_tiles(ref, x, pitch=TOKEN_ROWS):
    n = x.shape[0]
    for c in range(TOKEN_ROWS):
        ref[pl.ds(c, n, stride=pitch), :] = x[:, c * LANES:(c + 1) * LANES]


def _load_token_tiles(ref, n, pitch=TOKEN_ROWS):
    return jnp.concatenate(
        [ref[pl.ds(c, n, stride=pitch), :] for c in range(TOKEN_ROWS)], axis=1)


def _rms_norm(y, g):
    return y * lax.rsqrt(jnp.mean(y * y, axis=-1, keepdims=True) + LN_EPS) * g


def _layer_norm(x, g, b):
    mu = jnp.mean(x, axis=-1, keepdims=True)
    xc = x - mu
    var = jnp.mean(xc * xc, axis=-1, keepdims=True)
    return xc * lax.rsqrt(var + LN_EPS) * g + b


def _inproj_prompt_kernel(x_ref, mod_ref, w_in_ref, cos_ref, sin_ref, conv_w_ref, conv_b_ref,
                          wg_ref, b_a_ref, b_x_ref, lam_ref, gna_ref,
                          ya_ref, q_ref, k_ref, v_ref, conv_out_ref, h_out_ref,
                          xp_buf, h_carry):
    t = pl.program_id(1)
    rows = x_ref.shape[0]
    pad = SUBLANES

    @pl.when(t == 0)
    def _():
        xp_buf[0:pad, :] = jnp.zeros((pad, D_A), F32)
        h_carry[...] = jnp.zeros_like(h_carry)

    row = lax.broadcasted_iota(jnp.int32, (SUBLANES, D_A), 0)

    def part(r0, n, h_prev):
        sl = slice(r0, r0 + n)
        u = (x_ref[sl, :] * (1.0 + mod_ref[1:2, :]) + mod_ref[0:1, :]).astype(BF16)

        def proj(j):
            return jnp.dot(u, w_in_ref[:, j * D_A:(j + 1) * D_A], preferred_element_type=F32)

        cos = jnp.concatenate([cos_ref[sl, :]] * N_HEAD_PAIRS, axis=1)
        sin = jnp.concatenate([sin_ref[sl, :]] * N_HEAD_PAIRS, axis=1)
        q_ref[sl, :] = _rope_apply(proj(2), cos, sin)
        k_ref[sl, :] = _rope_apply(proj(3), cos, sin)
        v_ref[sl, :] = proj(4)

        xa = proj(0)
        xp_buf[pad + r0:pad + r0 + n, :] = xa
        xc = conv_b_ref[...] + xa * conv_w_ref[CONV_W - 1:CONV_W, :]
        for j in range(CONV_W - 1):
            off = pad - (CONV_W - 1) + j + r0
            xc = xc + xp_buf[off:off + n, :] * conv_w_ref[j:j + 1, :]

        a, u_in = _rglru_gates(xc, wg_ref, b_a_ref[...], b_x_ref[...], lam_ref[...])

        hs = []
        for g in range(n // SUBLANES):
            ag = a[g * SUBLANES:(g + 1) * SUBLANES]
            ug = u_in[g * SUBLANES:(g + 1) * SUBLANES]
            for sh in (1, 2, 4):
                keep = row >= sh
                a_sh = pltpu.roll(ag, sh, 0)
                u_sh = pltpu.roll(ug, sh, 0)
                ug = jnp.where(keep, ag * u_sh + ug, ug)
                ag = jnp.where(keep, ag * a_sh, ag)
            hg = ag * h_prev + ug
            hs.append(hg)
            h_prev = hg[SUBLANES - 1:SUBLANES, :]

        y = jnp.concatenate(hs, axis=0) * jax.nn.gelu(proj(1))
        ya_ref[sl, :] = _rms_norm(y, gna_ref[...]).astype(BF16)
        return h_prev

    h_last = h_carry[...]
    n_part = rows // INPROJ_PARTS
    for p in range(INPROJ_PARTS):
        h_last = part(p * n_part, n_part, h_last)
    h_carry[...] = h_last
    h_out_ref[...] = h_last

    tail = xp_buf[rows + pad - (CONV_W - 1):rows + pad, :]
    conv_out_ref[...] = tail
    xp_buf[pad - (CONV_W - 1):pad, :] = tail


def _inproj_prompt(x, mod3, w_in_bf, cos_t, sin_t, rg):
    b, s, _ = x.shape
    tile = 512
    nt = s // tile
    row_spec = lambda w: pl.BlockSpec((None, tile, w), lambda i, j: (i, j, 0))
    vec = lambda r, w: pl.BlockSpec((r, w), lambda i, j: (0, 0))
    outs = pl.pallas_call(
        _inproj_prompt_kernel,
        grid=(b, nt),
        in_specs=[row_spec(D_MODEL),
                  pl.BlockSpec((None, 6, D_MODEL), lambda i, j: (i, 0, 0)),
                  vec(D_MODEL, 5 * D_A),
                  pl.BlockSpec((tile, HEAD_PAIR_W), lambda i, j: (j, 0)),
                  pl.BlockSpec((tile, HEAD_PAIR_W), lambda i, j: (j, 0)),
                  vec(CONV_W, D_A), vec(1, D_A),
                  pl.BlockSpec((2, D_A // 2, D_A), lambda i, j: (0, 0, 0)),
                  vec(1, D_A), vec(1, D_A), vec(1, D_A), vec(1, D_A)],
        out_specs=[row_spec(D_A), row_spec(D_B), row_spec(D_B), row_spec(D_B),
                   pl.BlockSpec((None, CONV_W - 1, D_A), lambda i, j: (i, 0, 0)),
                   pl.BlockSpec((None, 1, D_A), lambda i, j: (i, 0, 0))],
        out_shape=[jax.ShapeDtypeStruct((b, s, D_A), BF16),
                   jax.ShapeDtypeStruct((b, s, D_B), F32),
                   jax.ShapeDtypeStruct((b, s, D_B), F32),
                   jax.ShapeDtypeStruct((b, s, D_B), F32),
                   jax.ShapeDtypeStruct((b, CONV_W - 1, D_A), F32),
                   jax.ShapeDtypeStruct((b, 1, D_A), F32)],
        scratch_shapes=[pltpu.VMEM((tile + SUBLANES, D_A), F32),
                        pltpu.VMEM((1, D_A), F32)],
        compiler_params=_cparams(("arbitrary", "arbitrary")),
        name="inproj_prompt",
    )(x, mod3, w_in_bf, cos_t, sin_t, rg["conv_w"], rg["conv_b"], rg["w_gate"],
      rg["b_a"], rg["b_x"], rg["lam"], rg["g_norm_a"])
    return outs


def _attn_prompt_kernel(q_ref, k_ref, v_ref, o_ref, kwin_ref, vwin_ref, acc_s, m_s, l_s, bias_s):
    s = q_ref.shape[0]
    keep = kwin_ref.shape[0]
    kwin_ref[...] = k_ref[s - keep:s, :]
    vwin_ref[...] = v_ref[s - keep:s, :]

    lane = lax.broadcasted_iota(jnp.int32, (Q_BLK, HEAD_PAIR_W), 1)
    head0 = lane < HEAD_DIM
    nk = 2 * Q_BLK

    qi = lax.broadcasted_iota(jnp.int32, (Q_BLK, nk), 0)
    ki = lax.broadcasted_iota(jnp.int32, (Q_BLK, nk), 1)
    for slot in range(2):
        dist = slot * Q_BLK + qi - ki
        bias_s[slot] = jnp.where((dist >= 0) & (dist <= N_KEYS), 0.0, NEG_INF)

    def rows(start, n, d):
        return pl.ds(start, n) if d == 1 else pl.ds(start, n, stride=d)

    def unit(bi, d, nb, u):
        r = u // nb
        j = u % nb
        jk = jnp.maximum(j - 1, 0)
        start_q = r + d * Q_BLK * j
        start_k = r + d * Q_BLK * jk
        bias = bias_s[j - jk]
        qb = q_ref[rows(start_q, Q_BLK, d), :] * (HEAD_DIM ** -0.5)
        kb = k_ref[rows(start_k, nk, d), :].astype(BF16)
        vb = v_ref[rows(start_k, nk, d), :].astype(BF16)
        q2 = jnp.concatenate([jnp.where(head0, qb, 0.0), jnp.where(head0, 0.0, qb)],
                             axis=0).astype(BF16)
        sc = lax.dot_general(q2, kb, (((1,), (1,)), ((), ())), preferred_element_type=F32)
        sc = sc + jnp.concatenate([bias, bias], axis=0)
        m = jnp.max(sc, axis=-1, keepdims=True)
        p = jnp.exp(sc - m)
        l = jnp.sum(p, axis=-1, keepdims=True)
        o = jnp.dot(p.astype(BF16), vb, preferred_element_type=F32)
        dst = rows(start_q, Q_BLK, d)
        acc_s[bi, dst, :] = jnp.where(head0, o[0:Q_BLK], o[Q_BLK:])
        m_s[bi, dst, :] = jnp.where(head0, m[0:Q_BLK], m[Q_BLK:])
        l_s[bi, dst, :] = jnp.where(head0, l[0:Q_BLK], l[Q_BLK:])

    for bi, (_, d) in enumerate(DILATED_CFG):
        nb = s // d // Q_BLK

        def body(u, carry, bi=bi, d=d, nb=nb):
            unit(bi, d, nb, u)
            return carry

        lax.fori_loop(0, d * nb, body, 0, unroll=ATTN_UNROLL)

    chunk = 512

    def merge(i, carry):
        r0 = pl.multiple_of(i * chunk, chunk)
        sl = pl.ds(r0, chunk)
        ms = [m_s[bi, sl, :] for bi in range(len(DILATED_CFG))]
        mx = functools.reduce(jnp.maximum, ms)
        num = jnp.zeros((chunk, HEAD_PAIR_W), F32)
        den = jnp.zeros((chunk, HEAD_PAIR_W), F32)
        for bi in range(len(DILATED_CFG)):
            w = jnp.exp(ms[bi] - mx)
            num = num + w * acc_s[bi, sl, :]
            den = den + w * l_s[bi, sl, :]
        o_ref[sl, :] = num / den
        return carry

    lax.fori_loop(0, s // chunk, merge, 0)


def _attn_prompt(q, k, v):
    b, s, _ = q.shape
    keep = min(WIN_MAX, s)
    nbr = len(DILATED_CFG)
    spec = pl.BlockSpec((None, s, HEAD_PAIR_W), lambda i, j: (i, 0, j))
    wspec = pl.BlockSpec((None, keep, HEAD_PAIR_W), lambda i, j: (i, 0, j))
    return pl.pallas_call(
        _attn_prompt_kernel,
        grid=(b, N_HEAD_PAIRS),
        in_specs=[spec, spec, spec],
        out_specs=[spec, wspec, wspec],
        out_shape=[jax.ShapeDtypeStruct((b, s, D_B), F32),
                   jax.ShapeDtypeStruct((b, keep, D_B), F32),
                   jax.ShapeDtypeStruct((b, keep, D_B), F32)],
        scratch_shapes=[pltpu.VMEM((nbr, s, HEAD_PAIR_W), F32),
                        pltpu.VMEM((nbr, s, HEAD_PAIR_W), F32),
                        pltpu.VMEM((nbr, s, HEAD_PAIR_W), F32),
                        pltpu.VMEM((2, Q_BLK, 2 * Q_BLK), F32)],
        compiler_params=_cparams(("arbitrary", "arbitrary")),
        name="attn_prompt",
    )(q, k, v)


def _inproj_sample_kernel(x_ref, mod_ref, w_in_ref, cos_ref, sin_ref, conv_state_ref, h0_ref,
                          conv_w_ref, conv_b_ref, wg_ref, b_a_ref, b_x_ref, lam_ref, gna_ref,
                          ya_ref, q_ref, k_ref, v_ref, conv_out_ref, h_out_ref):
    nt, nb, _ = x_ref.shape
    sh1 = mod_ref[:, 0:D_MODEL]
    sc1 = mod_ref[:, D_MODEL:2 * D_MODEL]
    u = (x_ref[...] * (1.0 + sc1)[None] + sh1[None]).astype(BF16).reshape(nt * nb, D_MODEL)

    def proj(j):
        return jnp.dot(u, w_in_ref[:, j * D_A:(j + 1) * D_A], preferred_element_type=F32)

    cos = cos_ref[...].reshape(nt * nb, D_B)
    sin = sin_ref[...].reshape(nt * nb, D_B)
    q_ref[...] = _rope_apply(proj(2), cos, sin).reshape(nt, nb, D_B)
    k_ref[...] = _rope_apply(proj(3), cos, sin).reshape(nt, nb, D_B)
    v_ref[...] = proj(4).reshape(nt, nb, D_B)

    xa = proj(0).reshape(nt, nb, D_A)
    xp = [conv_state_ref[j] for j in range(CONV_W - 1)] + [xa[t] for t in range(nt)]
    xc = jnp.concatenate(
        [conv_b_ref[...] + sum(xp[t + j] * conv_w_ref[j:j + 1, :] for j in range(CONV_W))
         for t in range(nt)], axis=0)
    for j in range(CONV_W - 1):
        conv_out_ref[j] = xp[nt + j]

    a, u_in = _rglru_gates(xc, wg_ref, b_a_ref[...], b_x_ref[...], lam_ref[...])
    h = h0_ref[...]
    hs = []
    for t in range(nt):
        h = a[t * nb:(t + 1) * nb] * h + u_in[t * nb:(t + 1) * nb]
        hs.append(h)
    h_out_ref[...] = h
    y = jnp.concatenate(hs, axis=0) * jax.nn.gelu(proj(1))
    ya_ref[...] = _rms_norm(y, gna_ref[...]).astype(BF16).reshape(nt, nb, D_A)


def _inproj_sample(x_tb, mod_s, w_in_bf, cos_t, sin_t, conv_state_tb, h0, rg):
    nt, nb, _ = x_tb.shape
    full = lambda shape: pl.BlockSpec(shape, lambda i: (0,) * len(shape))
    return pl.pallas_call(
        _inproj_sample_kernel,
        grid=(1,),
        in_specs=[full((nt, nb, D_MODEL)), full((nb, 6 * D_MODEL)), full((D_MODEL, 5 * D_A)),
                  full((nt, nb, D_B)), full((nt, nb, D_B)),
                  full((CONV_W - 1, nb, D_A)), full((nb, D_A)),
                  full((CONV_W, D_A)), full((1, D_A)), full((2, D_A // 2, D_A)),
                  full((1, D_A)), full((1, D_A)), full((1, D_A)), full((1, D_A))],
        out_specs=[full((nt, nb, D_A)), full((nt, nb, D_B)), full((nt, nb, D_B)),
                   full((nt, nb, D_B)), full((CONV_W - 1, nb, D_A)), full((nb, D_A))],
        out_shape=[jax.ShapeDtypeStruct((nt, nb, D_A), BF16),
                   jax.ShapeDtypeStruct((nt, nb, D_B), F32),
                   jax.ShapeDtypeStruct((nt, nb, D_B), F32),
                   jax.ShapeDtypeStruct((nt, nb, D_B), F32),
                   jax.ShapeDtypeStruct((CONV_W - 1, nb, D_A), F32),
                   jax.ShapeDtypeStruct((nb, D_A), F32)],
        compiler_params=_cparams(("arbitrary",)),
        name="inproj_sample",
    )(x_tb, mod_s, w_in_bf, cos_t, sin_t, conv_state_tb, h0, rg["conv_w"], rg["conv_b"],
      rg["w_gate"], rg["b_a"], rg["b_x"], rg["lam"], rg["g_norm_a"])


def _sample_attention_step(b, n_seq, q_ref, kn_ref, vn_ref, ck_hbm, cv_hbm, o_ref, kwin_hbm,
                           vwin_hbm, cin, cout, sem_in, sem_out, kn_pad, vn_pad):
    nt = q_ref.shape[0]
    n_buf = cin.shape[-1]
    n_rows = N_HEADS_B * nt
    srcs = (ck_hbm, cv_hbm)
    dsts = (kwin_hbm, vwin_hbm)
    slot = b % 2

    def copy_in(which, seq):
        s = seq % 2
        return pltpu.make_async_copy(srcs[which].at[seq], cin.at[s, which], sem_in.at[s, which])

    def copy_out(which, seq):
        s = seq % 2
        return pltpu.make_async_copy(cout.at[s, which], dsts[which].at[seq],
                                     sem_out.at[s, which])

    @pl.when(b == 0)
    def _():
        kn_pad[...] = jnp.zeros_like(kn_pad)
        vn_pad[...] = jnp.zeros_like(vn_pad)
        copy_in(0, 0).start(priority=BULK_DMA_PRIORITY)
        copy_in(1, 0).start(priority=BULK_DMA_PRIORITY)

    @pl.when(b + 1 < n_seq)
    def _():
        copy_in(0, b + 1).start(priority=BULK_DMA_PRIORITY)
        copy_in(1, b + 1).start(priority=BULK_DMA_PRIORITY)

    kn_pad[0:nt, :] = kn_ref[...]
    vn_pad[0:nt, :] = vn_ref[...]

    tail_lane = lax.broadcasted_iota(jnp.int32, (HEAD_DIM, LANES), 1)

    def shift_in(old_ref, new_pad, out_ref):
        new_t = pltpu.roll(new_pad.T, LANES - nt, 1)
        for h in range(N_HEADS_B):
            rolled = pltpu.roll(old_ref[h], n_buf - nt, 1)
            last = jnp.where(tail_lane < LANES - nt, rolled[:, n_buf - LANES:n_buf],
                             new_t[h * HEAD_DIM:(h + 1) * HEAD_DIM])
            out_ref[h, :, 0:n_buf - LANES] = rolled[:, 0:n_buf - LANES]
            out_ref[h, :, n_buf - LANES:n_buf] = last

    copy_in(0, b).wait()
    copy_in(1, b).wait()

    @pl.when(b >= 2)
    def _():
        copy_out(0, b - 2).wait()
        copy_out(1, b - 2).wait()

    ri = lax.broadcasted_iota(jnp.int32, (n_rows, nt), 0)
    ci = lax.broadcasted_iota(jnp.int32, (n_rows, nt), 1)
    pick = (ri % nt == ci).astype(BF16)
    qs = (q_ref[...] * (HEAD_DIM ** -0.5)).astype(BF16)
    q_rep = jnp.dot(pick, qs, preferred_element_type=F32)
    row_h = lax.broadcasted_iota(jnp.int32, (n_rows, D_B), 0) // nt
    lane_h = lax.broadcasted_iota(jnp.int32, (n_rows, D_B), 1) // HEAD_DIM
    own = row_h == lane_h
    qbd = jnp.where(own, q_rep, 0.0).astype(BF16)

    nt_dims = (((1,), (1,)), ((), ()))

    def mult(dist, limit_ok):
        c = jnp.zeros(dist.shape, F32)
        for win, d in DILATED_CFG:
            hit = (dist >= 0) & (dist <= win) & (dist % d == 0) & limit_ok
            c = c + hit.astype(F32)
        return c

    def weights(ck_t):
        sc_c = jnp.dot(qbd, ck_t.astype(BF16), preferred_element_type=F32)
        sc_n = lax.dot_general(qbd, kn_pad[...].astype(BF16), nt_dims,
                               preferred_element_type=F32)
        t_c = lax.broadcasted_iota(jnp.int32, sc_c.shape, 0) % nt
        dist_c = n_buf + t_c - lax.broadcasted_iota(jnp.int32, sc_c.shape, 1)
        mult_c = mult(dist_c, dist_c >= 0)
        t_n = lax.broadcasted_iota(jnp.int32, sc_n.shape, 0) % nt
        col_n = lax.broadcasted_iota(jnp.int32, sc_n.shape, 1)
        mult_n = mult(t_n - col_n, col_n < nt)
        sc_c = jnp.where(mult_c > 0, sc_c, NEG_INF)
        sc_n = jnp.where(mult_n > 0, sc_n, NEG_INF)
        m = jnp.maximum(jnp.max(sc_c, axis=-1, keepdims=True),
                        jnp.max(sc_n, axis=-1, keepdims=True))
        p_c = mult_c * jnp.exp(sc_c - m)
        p_n = mult_n * jnp.exp(sc_n - m)
        l = jnp.sum(p_c, axis=-1, keepdims=True) + jnp.sum(p_n, axis=-1, keepdims=True)
        return p_c.astype(BF16), p_n.astype(BF16), l

    p_c, p_n, l = weights(cin[slot, 0].reshape(D_B, n_buf))
    shift_in(cin.at[slot, 0], kn_pad[...], cout.at[slot, 0])
    acc = (lax.dot_general(p_c, cin[slot, 1].reshape(D_B, n_buf).astype(BF16), nt_dims,
                           preferred_element_type=F32)
           + jnp.dot(p_n, vn_pad[...].astype(BF16), preferred_element_type=F32))
    shift_in(cin.at[slot, 1], vn_pad[...], cout.at[slot, 1])
    o_full = jnp.where(own, acc / l, 0.0)
    out = o_full[0:nt, :]
    for h in range(1, N_HEADS_B):
        out = out + o_full[h * nt:(h + 1) * nt, :]
    o_ref[...] = out

    copy_out(0, b).start(priority=BULK_DMA_PRIORITY)
    copy_out(1, b).start(priority=BULK_DMA_PRIORITY)

    @pl.when(b == n_seq - 1)
    def _():
        @pl.when(b >= 1)
        def _():
            copy_out(0, b - 1).wait()
            copy_out(1, b - 1).wait()
        copy_out(0, b).wait()
        copy_out(1, b).wait()


def _split_bf16(x):
    hi = x.astype(BF16)
    lo = (x - hi.astype(F32)).astype(BF16)
    return hi, lo


def _first_argmax(vals):
    mx = functools.reduce(jnp.maximum, vals)
    idx = jnp.full(mx.shape, float(len(vals) - 1), F32)
    for j in range(len(vals) - 2, -1, -1):
        idx = jnp.where(vals[j] == mx, float(j), idx)
    return mx, idx


def _outproj_kernel(ya_ref, yb_ref, x_ref, gt1_ref, sh2_ref, sc2_ref, w_out_ref, gnb_ref,
                    ln_g_ref, ln_b_ref, w_r_ref, b_r_ref, x1_ref, u2_ref, route_ref):
    yb = _rms_norm(yb_ref[...], gnb_ref[...]).astype(BF16)
    mixed = (jnp.dot(ya_ref[...], w_out_ref[0:D_A, :], preferred_element_type=F32)
             + jnp.dot(yb, w_out_ref[D_A:D_A + D_B, :], preferred_element_type=F32))
    x1 = _layer_norm(DN_ALPHA * x_ref[...] + gt1_ref[...] * mixed, ln_g_ref[...], ln_b_ref[...])
    x1_ref[...] = x1
    u2 = x1 * (1.0 + sc2_ref[...]) + sh2_ref[...]
    _store_token_tiles(u2_ref, u2, TOKEN_PITCH)

    u_hi, u_lo = _split_bf16(u2)
    w_hi, w_lo = _split_bf16(w_r_ref[...])
    both = jnp.dot(u_hi, jnp.concatenate([w_hi, w_lo], axis=1), preferred_element_type=F32)
    logits = (both[:, 0:ROUTE_W] + jnp.dot(u_lo, w_hi, preferred_element_type=F32)
              + both[:, ROUTE_W:2 * ROUTE_W]) + b_r_ref[...]
    lt = logits.T

    g_rows = [lt[j:j + 1, :] for j in range(N_GROUPS)]
    g_max, g_idx = _first_argmax(g_rows)
    p_group = 1.0 / sum(jnp.exp(g - g_max) for g in g_rows)
    e_rows = []
    for e in range(N_EXP_PER_GROUP):
        acc = jnp.zeros_like(g_max)
        for g in range(N_GROUPS):
            r = N_GROUPS + g * N_EXP_PER_GROUP + e
            acc = jnp.where(g_idx == float(g), lt[r:r + 1, :], acc)
        e_rows.append(acc)
    v1, i1 = _first_argmax(e_rows)
    rest = [jnp.where(i1 == float(e), -jnp.inf, e_rows[e]) for e in range(N_EXP_PER_GROUP)]
    v2, i2 = _first_argmax(rest)
    ex = jnp.exp(v2 - v1)
    w1 = p_group / (1.0 + ex)
    w2 = p_group * ex / (1.0 + ex)
    lo = jnp.minimum(i1, i2)
    hi = jnp.maximum(i1, i2)
    pair = jnp.where(lo == 0.0, hi - 1.0, jnp.where(lo == 1.0, hi + 1.0, 5.0))
    cls = g_idx * float(N_PAIRS) + pair
    w_of_lo = jnp.where(i1 < i2, w1, w2)
    w_of_hi = jnp.where(i1 < i2, w2, w1)
    e_lo = g_idx * float(N_EXP_PER_GROUP) + lo
    e_hi = g_idx * float(N_EXP_PER_GROUP) + hi
    n_tok = cls.shape[1]
    route = jnp.concatenate(
        [cls, w_of_lo, w_of_hi, e_lo, e_hi, jnp.zeros((ROUTE_W - 5, n_tok), F32)], axis=0)
    route_ref[...] = route[0:SUBLANES, :]
    u2_ref[pl.ds(TOKEN_ROWS, n_tok, stride=TOKEN_PITCH), :] = route.T


def _outproj(ya, yb, x, mods, w_out_bf, gnb, ln_g, ln_b, w_r, b_r, tile):
    n = x.shape[0]
    nt = n // tile
    mod_arr, gt1_spec, sh2_spec, sc2_spec = mods
    row = lambda w: pl.BlockSpec((tile, w), lambda i: (i, 0))
    vec = lambda r, w: pl.BlockSpec((r, w), lambda i: (0, 0))
    return pl.pallas_call(
        _outproj_kernel,
        grid=(nt,),
        in_specs=[row(D_A), row(D_B), row(D_MODEL), gt1_spec, sh2_spec, sc2_spec,
                  vec(D_MODEL, D_MODEL), vec(1, D_B), vec(1, D_MODEL), vec(1, D_MODEL),
                  vec(D_MODEL, ROUTE_W), vec(1, ROUTE_W)],
        out_specs=[row(D_MODEL),
                   pl.BlockSpec((tile * TOKEN_PITCH, LANES), lambda i: (i, 0)),
                   pl.BlockSpec((None, SUBLANES, tile), lambda i: (i, 0, 0))],
        out_shape=[jax.ShapeDtypeStruct((n, D_MODEL), F32),
                   jax.ShapeDtypeStruct((n * TOKEN_PITCH, LANES), F32),
                   jax.ShapeDtypeStruct((nt, SUBLANES, tile), F32)],
        compiler_params=_cparams(("arbitrary",)),
        name="outproj_router",
    )(ya, yb, x, mod_arr, mod_arr, mod_arr, w_out_bf, gnb, ln_g, ln_b, w_r, b_r)


def _moe_kernel(e_lo_ref, e_hi_ref, n_used_ref, src_ref, dst_ref,
                x_hbm, wg_lo, wg_hi, wu_lo, wu_hi, wd_lo, wd_hi,
                q_ref, kn_ref, vn_ref, ck_hbm, cv_hbm,
                o_hbm, yb_ref, kwin_hbm, vwin_hbm,
                xg, og, gsem, ssem, cin, cout, csem_in, csem_out, kn_pad, vn_pad,
                *, n_tokens, tile, n_seq):
    i = pl.program_id(0)
    n_steps = pl.num_programs(0)
    n_used = n_used_ref[0]
    cur = i % N_BUF
    ahead = (i + 2) % N_BUF
    rows = tile * TOKEN_ROWS

    def gather_token(base, r, buf):
        pltpu.make_async_copy(x_hbm.at[pl.ds(src_ref[base + r], TOKEN_PITCH)],
                              xg.at[buf, pl.ds(r * TOKEN_PITCH, TOKEN_PITCH)],
                              gsem.at[buf]).start()

    def gather_wait(buf):
        pltpu.make_async_copy(x_hbm.at[pl.ds(0, tile * TOKEN_PITCH)], xg.at[buf],
                              gsem.at[buf]).wait()

    def scatter_wait(buf):
        pltpu.make_async_copy(og.at[buf], o_hbm.at[pl.ds(0, rows)], ssem.at[buf]).wait()

    @pl.when(i == 0)
    def _():
        def first(r, c):
            gather_token(0, r, 0)
            gather_token(jnp.minimum(1, n_steps - 1) * tile, r, 1)
            return c
        lax.fori_loop(0, tile, first, 0, unroll=DMA_UNROLL)
        og[...] = jnp.zeros_like(og)
        for buf in range(N_BUF):
            pad_rows = pltpu.make_async_copy(
                og.at[buf], o_hbm.at[pl.ds((n_tokens + buf * tile) * TOKEN_ROWS, rows)],
                ssem.at[buf])
            pad_rows.start()
            pad_rows.wait()

    def expert_tile():
        nxt = jnp.minimum(i + 2, n_steps - 1) * tile
        for r in range(tile):
            gather_token(nxt, r, ahead)
        prev = i * tile
        for r in range(tile):
            row = pl.multiple_of(dst_ref[prev + r], TOKEN_ROWS)
            pltpu.make_async_copy(og.at[ahead, pl.ds(r * TOKEN_ROWS, TOKEN_ROWS)],
                                  o_hbm.at[pl.ds(row, TOKEN_ROWS)], ssem.at[ahead]).start()

        xb = _load_token_tiles(xg.at[cur], tile, TOKEN_PITCH).astype(BF16)
        w2 = xg[cur, pl.ds(TOKEN_ROWS, tile, stride=TOKEN_PITCH), :]
        out = jnp.zeros((tile, D_MODEL), F32)
        for col, wg, wu, wd in ((1, wg_lo, wu_lo, wd_lo), (2, wg_hi, wu_hi, wd_hi)):
            hg = jnp.dot(xb, wg[...], preferred_element_type=F32)
            hu = jnp.dot(xb, wu[...], preferred_element_type=F32)
            act = (hg * jax.nn.sigmoid(hg)) * hu * w2[:, col:col + 1]
            out = out + jnp.dot(act.astype(BF16), wd[...], preferred_element_type=F32)
        _store_token_tiles(og.at[cur], out)

    @pl.when(i <= n_used)
    def _():
        gather_wait(cur)

        @pl.when(i >= 2)
        def _():
            scatter_wait(cur)

        expert_tile()

        @pl.when(i == n_used)
        def _():
            for buf in range(N_BUF):
                @pl.when(buf != cur)
                def _():
                    gather_wait(buf)
                    scatter_wait(buf)

    @pl.when(i < n_seq)
    def _():
        _sample_attention_step(i, n_seq, q_ref, kn_ref, vn_ref, ck_hbm, cv_hbm, yb_ref,
                               kwin_hbm, vwin_hbm, cin, cout, csem_in, csem_out, kn_pad, vn_pad)


def _moe(u2t, cls, w_gate_bf, w_up_bf, w_down_bf, tile, q_s, k_new, v_new, cache_k_t, cache_v_t):
    n = cls.shape[0]
    n_seq, nt, _ = q_s.shape
    n_buf = cache_k_t.shape[-1]
    n_steps = n // tile + N_CLASSES
    assert n_steps >= n_seq, "one sample sequence per grid step"
    cls = cls.astype(jnp.int32)
    order = jnp.argsort(cls, stable=True).astype(jnp.int32)
    class_ids = jnp.arange(N_CLASSES, dtype=jnp.int32)
    counts = jnp.sum((cls[:, None] == class_ids[None, :]).astype(jnp.int32), axis=0)
    tiles_per = (counts + tile - 1) // tile
    tile_end = jnp.cumsum(tiles_per)
    tile_off = tile_end - tiles_per
    n_used = tile_end[-1]
    class_start = jnp.cumsum(counts) - counts
    step = jnp.arange(n_steps, dtype=jnp.int32)
    step_c = jnp.minimum(step, n_used - 1)
    cls_of = jnp.sum((step_c[:, None] >= tile_end[None, :]).astype(jnp.int32), axis=1)
    onehot = (cls_of[:, None] == class_ids[None, :]).astype(jnp.int32)
    pick = lambda table: jnp.sum(onehot * table[None, :], axis=1)
    local = step - pick(tile_off)
    nvalid = jnp.where(step < n_used, jnp.clip(pick(counts) - local * tile, 0, tile), 0)
    r = jnp.arange(tile, dtype=jnp.int32)
    pos = pick(class_start)[:, None] + local[:, None] * tile + r[None, :]
    valid = r[None, :] < nvalid[:, None]
    tok = order[jnp.clip(pos, 0, n - 1)]
    src = (jnp.where(valid, tok, 0) * TOKEN_PITCH).astype(jnp.int32).reshape(-1)
    spare = n + (step[:, None] % N_BUF) * tile + r[None, :]
    dst = jnp.where(valid, tok, spare)
    dst = jnp.concatenate([(n + (N_BUF - 1) * tile + r)[None, :], dst], axis=0)
    dst = (dst * TOKEN_ROWS).astype(jnp.int32).reshape(-1)
    grp = cls_of // N_PAIRS
    pair = cls_of % N_PAIRS
    pair_lo = (pair >= 3).astype(jnp.int32) + (pair >= 5).astype(jnp.int32)
    pair_hi = pair + 1 - 2 * (pair >= 3).astype(jnp.int32) - (pair >= 5).astype(jnp.int32)
    e_lo = (grp * N_EXP_PER_GROUP + pair_lo).astype(jnp.int32)
    e_hi = (grp * N_EXP_PER_GROUP + pair_hi).astype(jnp.int32)

    w_in_spec = lambda which: pl.BlockSpec(
        (None, D_MODEL, D_EXPERT), lambda i, elo, ehi, nu, s, d: ((elo, ehi)[which][i], 0, 0))
    w_dn_spec = lambda which: pl.BlockSpec(
        (None, D_EXPERT, D_MODEL), lambda i, elo, ehi, nu, s, d: ((elo, ehi)[which][i], 0, 0))
    any_spec = pl.BlockSpec(memory_space=pl.ANY)
    seq_spec = pl.BlockSpec((None, nt, D_B),
                            lambda i, elo, ehi, nu, s, d: (jnp.minimum(i, n_seq - 1), 0, 0))
    win_shape = jax.ShapeDtypeStruct((n_seq, N_HEADS_B, HEAD_DIM, n_buf), F32)
    grid_spec = pltpu.PrefetchScalarGridSpec(
        num_scalar_prefetch=5,
        grid=(n_steps,),
        in_specs=[any_spec,
                  w_in_spec(0), w_in_spec(1), w_in_spec(0), w_in_spec(1),
                  w_dn_spec(0), w_dn_spec(1),
                  seq_spec, seq_spec, seq_spec, any_spec, any_spec],
        out_specs=[any_spec, seq_spec, any_spec, any_spec],
        scratch_shapes=[pltpu.VMEM((N_BUF, tile * TOKEN_PITCH, LANES), F32),
                        pltpu.VMEM((N_BUF, tile * TOKEN_ROWS, LANES), F32),
                        pltpu.SemaphoreType.DMA((N_BUF,)),
                        pltpu.SemaphoreType.DMA((N_BUF,)),
                        pltpu.VMEM((2, 2, N_HEADS_B, HEAD_DIM, n_buf), F32),
                        pltpu.VMEM((2, 2, N_HEADS_B, HEAD_DIM, n_buf), F32),
                        pltpu.SemaphoreType.DMA((2, 2)),
                        pltpu.SemaphoreType.DMA((2, 2)),
                        pltpu.VMEM((LANES, D_B), F32),
                        pltpu.VMEM((LANES, D_B), F32)],
    )
    return pl.pallas_call(
        functools.partial(_moe_kernel, n_tokens=n, tile=tile, n_seq=n_seq),
        grid_spec=grid_spec,
        out_shape=[jax.ShapeDtypeStruct(((n + N_BUF * tile) * TOKEN_ROWS, LANES), F32),
                   jax.ShapeDtypeStruct((n_seq, nt, D_B), F32), win_shape, win_shape],
        compiler_params=_cparams(("arbitrary",)),
        name="moe_sparse",
    )(e_lo, e_hi, n_used.reshape(1).astype(jnp.int32), src, dst,
      u2t, w_gate_bf, w_gate_bf, w_up_bf, w_up_bf, w_down_bf, w_down_bf,
      q_s, k_new, v_new, cache_k_t, cache_v_t)


def _moe_dense_kernel(x_ref, wg_ref, wu_ref, wd_ref, o_ref):
    e = pl.program_id(0)
    n = o_ref.shape[0]

    @pl.when(e == 0)
    def _():
        o_ref[...] = jnp.zeros_like(o_ref)

    xb = _load_token_tiles(x_ref, n, TOKEN_PITCH).astype(BF16)
    ef = e.astype(F32)
    route = x_ref[pl.ds(TOKEN_ROWS, n, stride=TOKEN_PITCH), :]
    col = lambda c: route[:, c:c + 1]
    comb = jnp.where(col(3) == ef, col(1), 0.0) + jnp.where(col(4) == ef, col(2), 0.0)
    hg = jnp.dot(xb, wg_ref[...], preferred_element_type=F32)
    hu = jnp.dot(xb, wu_ref[...], preferred_element_type=F32)
    act = (hg * jax.nn.sigmoid(hg)) * hu * comb
    o_ref[...] += jnp.dot(act.astype(BF16), wd_ref[...], preferred_element_type=F32)


def _moe_dense(u2t, w_gate_bf, w_up_bf, w_down_bf):
    n = u2t.shape[0] // TOKEN_PITCH
    return pl.pallas_call(
        _moe_dense_kernel,
        grid=(N_EXPERTS,),
        in_specs=[pl.BlockSpec((n * TOKEN_PITCH, LANES), lambda e: (0, 0)),
                  pl.BlockSpec((None, D_MODEL, D_EXPERT), lambda e: (e, 0, 0)),
                  pl.BlockSpec((None, D_MODEL, D_EXPERT), lambda e: (e, 0, 0)),
                  pl.BlockSpec((None, D_EXPERT, D_MODEL), lambda e: (e, 0, 0))],
        out_specs=pl.BlockSpec((n, D_MODEL), lambda e: (0, 0)),
        out_shape=jax.ShapeDtypeStruct((n, D_MODEL), F32),
        compiler_params=_cparams(("arbitrary",)),
        name="moe_dense",
    )(u2t, w_gate_bf, w_up_bf, w_down_bf)


def _final_kernel(x1_ref, ffn_ref, gt2_ref, g_ref, b_ref, o_ref, *, token_tiled):
    rows = x1_ref.shape[0]
    ffn = _load_token_tiles(ffn_ref, rows) if token_tiled else ffn_ref[...]
    o_ref[...] = _layer_norm(DN_ALPHA * x1_ref[...] + gt2_ref[...] * ffn, g_ref[...], b_ref[...])


def _final_norm(x1, ffn, mod_arr, gt2_spec, ln_g, ln_b, tile):
    n = x1.shape[0]
    token_tiled = ffn.shape[-1] == LANES
    row = pl.BlockSpec((tile, D_MODEL), lambda i: (i, 0))
    ffn_spec = pl.BlockSpec((tile * TOKEN_ROWS, LANES), lambda i: (i, 0)) if token_tiled else row
    vec = pl.BlockSpec((1, D_MODEL), lambda i: (0, 0))
    return pl.pallas_call(
        functools.partial(_final_kernel, token_tiled=token_tiled),
        grid=(n // tile,),
        in_specs=[row, ffn_spec, gt2_spec, vec, vec],
        out_specs=row,
        out_shape=jax.ShapeDtypeStruct((n, D_MODEL), F32),
        compiler_params=_cparams(("arbitrary",)),
        name="final_norm",
    )(x1, ffn, mod_arr, ln_g, ln_b)


def _rope_tables(pos):
    half = HEAD_DIM // 2
    inv = ROPE_THETA ** (-jnp.arange(half, dtype=F32) * 2.0 / HEAD_DIM)
    ang = pos.astype(F32)[:, None] * inv[None, :]
    cos = jnp.cos(ang)
    sin = jnp.sin(ang)
    cos_t = jnp.tile(jnp.concatenate([cos, cos], axis=-1), (1, 2))
    sin_t = jnp.tile(jnp.concatenate([-sin, sin], axis=-1), (1, 2))
    return cos_t, sin_t


def _block_diag(w):
    n, a, b = w.shape
    eye = jnp.eye(n, dtype=w.dtype)
    return (eye[:, None, :, None] * w[:, :, None, :]).reshape(n * a, n * b)


def _prepare_weights(w_in, conv_w, conv_b, w_rg_a, b_rg_a, w_rg_x, b_rg_x, rg_lambda, g_norm_a,
                     w_router_group, b_router_group, w_router_expert, b_router_expert):
    half_blocks = N_BLK_A // 2
    w_gate = jnp.stack([
        jnp.concatenate([_block_diag(w_rg_a[h * half_blocks:(h + 1) * half_blocks]),
                         _block_diag(w_rg_x[h * half_blocks:(h + 1) * half_blocks])], axis=1)
        for h in range(2)]).astype(BF16)
    rg = dict(conv_w=conv_w, conv_b=conv_b.reshape(1, D_A), w_gate=w_gate,
              b_a=b_rg_a.reshape(1, D_A), b_x=b_rg_x.reshape(1, D_A),
              lam=rg_lambda.reshape(1, D_A), g_norm_a=g_norm_a.reshape(1, D_A))
    n_logits = N_GROUPS + N_EXPERTS
    w_r = jnp.concatenate(
        [w_router_group,
         w_router_expert.transpose(1, 0, 2).reshape(D_MODEL, N_EXPERTS),
         jnp.zeros((D_MODEL, ROUTE_W - n_logits), F32)], axis=1)
    b_r = jnp.concatenate([b_router_group, b_router_expert.reshape(-1),
                           jnp.zeros((ROUTE_W - n_logits,), F32)]).reshape(1, ROUTE_W)
    return rg, w_r, b_r


def kernel(x_prompt, x_sample, state_conv, state_rglru, cache_win_k, cache_win_v, c_prompt, c_sample, w_ada, b_ada, w_in, conv_w, conv_b, w_rg_a, b_rg_a, w_rg_x, b_rg_x, rg_lambda, g_norm_a, g_norm_b, w_out, ln1_g, ln1_b, w_router_group, b_router_group, w_router_expert, b_router_expert, w_exp_gate, w_exp_up, w_exp_down, ln2_g, ln2_b):
    bp, sp, _ = x_prompt.shape
    bs, ts, _ = x_sample.shape
    n_buf = cache_win_k.shape[1]

    rg, w_r, b_r = _prepare_weights(w_in, conv_w, conv_b, w_rg_a, b_rg_a, w_rg_x, b_rg_x,
                                    rg_lambda, g_norm_a, w_router_group, b_router_group,
                                    w_router_expert, b_router_expert)
    w_in_bf = w_in.astype(BF16)
    outproj_weights = (w_out.astype(BF16), g_norm_b.reshape(1, D_B), ln1_g.reshape(1, D_MODEL),
                       ln1_b.reshape(1, D_MODEL), w_r, b_r)
    expert_weights = (w_exp_gate.astype(BF16), w_exp_up.astype(BF16), w_exp_down.astype(BF16))
    ln2 = (ln2_g.reshape(1, D_MODEL), ln2_b.reshape(1, D_MODEL))

    mod = _modulation(jnp.concatenate([c_prompt, c_sample], axis=0), w_ada, b_ada)
    mod_p, mod_s = mod[:bp], mod[bp:]

    tile_p = 512
    cos_p, sin_p = _rope_tables(jnp.arange(sp))
    ya_p, q_p, k_p, v_p, conv_p, h_p = _inproj_prompt(
        x_prompt, mod_p.reshape(bp, 6, D_MODEL), w_in_bf, cos_p, sin_p, rg)
    yb_p, kwin_p, vwin_p = _attn_prompt(q_p, k_p, v_p)
    tiles_per_seq = sp // tile_p
    mod_p3 = mod_p.reshape(bp * 6, 1, D_MODEL)
    mod_spec_p = lambda j: pl.BlockSpec((None, 1, D_MODEL),
                                        lambda i: ((i // tiles_per_seq) * 6 + j, 0, 0))
    x1_p, u2t_p, route_p = _outproj(
        ya_p.reshape(bp * sp, D_A), yb_p.reshape(bp * sp, D_B), x_prompt.reshape(bp * sp, D_MODEL),
        (mod_p3, mod_spec_p(2), mod_spec_p(3), mod_spec_p(4)), *outproj_weights, tile_p)

    cos_s, sin_s = _rope_tables(PAST_LEN + jnp.arange(ts))
    tb = lambda t: jnp.broadcast_to(jnp.tile(t, (1, N_HEAD_PAIRS))[:, None, :], (ts, bs, D_B))
    ya_s, q_s, k_s, v_s, conv_s, h_s = _inproj_sample(
        x_sample.transpose(1, 0, 2), mod_s, w_in_bf, tb(cos_s), tb(sin_s),
        state_conv.transpose(1, 0, 2), state_rglru, rg)
    bt = lambda t: t.transpose(1, 0, 2)

    ffn_p, yb_s, kwin_s, vwin_s = _moe(
        u2t_p, route_p[:, 0, :].reshape(-1), *expert_weights, 256,
        bt(q_s), bt(k_s), bt(v_s), cache_win_k.transpose(0, 2, 3, 1),
        cache_win_v.transpose(0, 2, 3, 1))
    y_p = _final_norm(x1_p, ffn_p, mod_p3, mod_spec_p(5), *ln2, tile_p)

    mod_spec_s = lambda j: pl.BlockSpec((bs, D_MODEL), lambda i: (0, j))
    x1_s, u2t_s, _ = _outproj(
        ya_s.reshape(ts * bs, D_A), bt(yb_s).reshape(ts * bs, D_B),
        x_sample.transpose(1, 0, 2).reshape(ts * bs, D_MODEL),
        (mod_s, mod_spec_s(2), mod_spec_s(3), mod_spec_s(4)), *outproj_weights, bs)
    ffn_s = _moe_dense(u2t_s, *expert_weights)
    y_s = _final_norm(x1_s, ffn_s, mod_s, mod_spec_s(5), *ln2, bs)

    heads = lambda t: t.reshape(t.shape[0], t.shape[1], N_HEADS_B, HEAD_DIM)
    return (y_p.reshape(bp, sp, D_MODEL), bt(y_s.reshape(ts, bs, D_MODEL)),
            conv_p, h_p.reshape(bp, D_A), heads(kwin_p), heads(vwin_p),
            bt(conv_s), h_s, kwin_s.transpose(0, 3, 1, 2), vwin_s.transpose(0, 3, 1, 2))
```

```python
import functools
import math

import jax
import jax.numpy as jnp
from jax import lax
from jax.experimental import pallas as pl
from jax.experimental.pallas import tpu as pltpu

F32 = jnp.float32
BF16 = jnp.bfloat16

D_MODEL = 1024
D_A = 512
N_BLK_A = 8
BLK_W_A = D_A // N_BLK_A
CONV_W = 4
RG_C = 8.0
D_B = 512
HEAD_DIM = 64
N_HEADS_B = D_B // HEAD_DIM
DILATED_CFG = ((128, 1), (512, 4), (2048, 16))
WIN_MAX = 2048
N_KEYS = 128
ROPE_THETA = 10000.0
PAST_LEN = 8192
N_GROUPS = 4
N_EXP_PER_GROUP = 4
N_EXPERTS = N_GROUPS * N_EXP_PER_GROUP
D_EXPERT = 512
DN_ALPHA = 2.0 ** 0.25
LN_EPS = 1e-5
NEG_INF = -1e30

N_PAIRS = 6
N_CLASSES = N_GROUPS * N_PAIRS
LANES = 128
SUBLANES = 8
HEAD_PAIR_W = 2 * HEAD_DIM
N_HEAD_PAIRS = N_HEADS_B // 2
Q_BLK = 128
ATTN_UNROLL = 32
DMA_UNROLL = 8
INPROJ_PARTS = 1
ROUTE_W = LANES
TOKEN_ROWS = D_MODEL // LANES
TOKEN_PITCH = TOKEN_ROWS + 1
N_BUF = 3
VMEM_LIMIT = 60 * 1024 * 1024


def _cparams(sem):
    return pltpu.CompilerParams(dimension_semantics=sem, vmem_limit_bytes=VMEM_LIMIT)


def _mod_kernel(c_ref, w_ref, b_ref, o_ref):
    c = c_ref[...]
    s = (c * jax.nn.sigmoid(c)).astype(BF16)
    o_ref[...] = jnp.dot(s, w_ref[...].astype(BF16), preferred_element_type=F32) + b_ref[...]


def _modulation(c_all, w_ada, b_ada):
    n = c_all.shape[0]
    tn = 1024
    return pl.pallas_call(
        _mod_kernel,
        grid=(6 * D_MODEL // tn,),
        in_specs=[pl.BlockSpec((n, D_MODEL), lambda j: (0, 0)),
                  pl.BlockSpec((D_MODEL, tn), lambda j: (0, j)),
                  pl.BlockSpec((1, tn), lambda j: (0, j))],
        out_specs=pl.BlockSpec((n, tn), lambda j: (0, j)),
        out_shape=jax.ShapeDtypeStruct((n, 6 * D_MODEL), F32),
        compiler_params=_cparams(("arbitrary",)),
        name="adaln_mod",
    )(c_all, w_ada, b_ada.reshape(1, -1))


def _cast_kernel(g_ref, u_ref, d_ref, go_ref, uo_ref, do_ref):
    go_ref[...] = g_ref[...].astype(BF16)
    uo_ref[...] = u_ref[...].astype(BF16)
    do_ref[...] = d_ref[...].astype(BF16)


def _cast_expert_weights(w_gate, w_up, w_down):
    spec = lambda w: pl.BlockSpec((None,) + w.shape[1:], lambda e: (e, 0, 0))
    return pl.pallas_call(
        _cast_kernel,
        grid=(w_gate.shape[0],),
        in_specs=[spec(w_gate), spec(w_up), spec(w_down)],
        out_specs=[spec(w_gate), spec(w_up), spec(w_down)],
        out_shape=[jax.ShapeDtypeStruct(w.shape, BF16) for w in (w_gate, w_up, w_down)],
        compiler_params=_cparams(("arbitrary",)),
        name="cast_expert_weights",
    )(w_gate, w_up, w_down)


def _rope_apply(t, cos, sin_signed):
    lane = lax.broadcasted_iota(jnp.int32, t.shape, t.ndim - 1)
    first_half = (lane & (HEAD_DIM - 1)) < HEAD_DIM // 2
    width = t.shape[-1]
    swapped = jnp.where(first_half,
                        pltpu.roll(t, width - HEAD_DIM // 2, t.ndim - 1),
                        pltpu.roll(t, HEAD_DIM // 2, t.ndim - 1))
    return t * cos + swapped * sin_signed


def _rglru_gates(xc, wg_ref, b_a, b_x, lam):
    half = D_A // 2
    xcb = xc.astype(BF16)
    g0 = jnp.dot(xcb[:, :half], wg_ref[0], preferred_element_type=F32)
    g1 = jnp.dot(xcb[:, half:], wg_ref[1], preferred_element_type=F32)
    r = jax.nn.sigmoid(jnp.concatenate([g0[:, :half], g1[:, :half]], axis=1) + b_a)
    i = jax.nn.sigmoid(jnp.concatenate([g0[:, half:], g1[:, half:]], axis=1) + b_x)
    z = -lam
    softplus = jnp.maximum(z, 0.0) + jnp.log1p(jnp.exp(-jnp.abs(z)))
    log_a = -RG_C * r * softplus
    a = jnp.exp(log_a)
    one_minus_a2 = -jnp.tanh(log_a) * (a * a + 1.0)
    u = jnp.sqrt(one_minus_a2) * (i * xc)
    return a, u


def _store_token_tiles(ref, x, pitch=TOKEN_ROWS):
    n = x.shape[0]
    for c in range(TOKEN_ROWS):
        ref[pl.ds(c, n, stride=pitch), :] = x[:, c * LANES:(c + 1) * LANES]


def _load_token_tiles(ref, n, pitch=TOKEN_ROWS):
    return jnp.concatenate(
        [ref[pl.ds(c, n, stride=pitch), :] for c in range(TOKEN_ROWS)], axis=1)


def _rms_norm(y, g):
    return y * lax.rsqrt(jnp.mean(y * y, axis=-1, keepdims=True) + LN_EPS) * g


def _layer_norm(x, g, b):
    mu = jnp.mean(x, axis=-1, keepdims=True)
    xc = x - mu
    var = jnp.mean(xc * xc, axis=-1, keepdims=True)
    return xc * lax.rsqrt(var + LN_EPS) * g + b


def _inproj_prompt_kernel(x_ref, mod_ref, w_in_ref, cos_ref, sin_ref, conv_w_ref, conv_b_ref,
                          wg_ref, b_a_ref, b_x_ref, lam_ref, gna_ref,
                          ya_ref, q_ref, k_ref, v_ref, conv_out_ref, h_out_ref,
                          xp_buf, h_carry):
    t = pl.program_id(1)
    rows = x_ref.shape[0]
    pad = SUBLANES

    @pl.when(t == 0)
    def _():
        xp_buf[0:pad, :] = jnp.zeros((pad, D_A), F32)
        h_carry[...] = jnp.zeros_like(h_carry)

    row = lax.broadcasted_iota(jnp.int32, (SUBLANES, D_A), 0)

    def part(r0, n, h_prev):
        sl = slice(r0, r0 + n)
        u = (x_ref[sl, :] * (1.0 + mod_ref[1:2, :]) + mod_ref[0:1, :]).astype(BF16)

        def proj(j):
            return jnp.dot(u, w_in_ref[:, j * D_A:(j + 1) * D_A], preferred_element_type=F32)

        cos = jnp.concatenate([cos_ref[sl, :]] * N_HEAD_PAIRS, axis=1)
        sin = jnp.concatenate([sin_ref[sl, :]] * N_HEAD_PAIRS, axis=1)
        q_ref[sl, :] = _rope_apply(proj(2), cos, sin)
        k_ref[sl, :] = _rope_apply(proj(3), cos, sin)
        v_ref[sl, :] = proj(4)

        xa = proj(0)
        xp_buf[pad + r0:pad + r0 + n, :] = xa
        xc = conv_b_ref[...] + xa * conv_w_ref[CONV_W - 1:CONV_W, :]
        for j in range(CONV_W - 1):
            off = pad - (CONV_W - 1) + j + r0
            xc = xc + xp_buf[off:off + n, :] * conv_w_ref[j:j + 1, :]

        a, u_in = _rglru_gates(xc, wg_ref, b_a_ref[...], b_x_ref[...], lam_ref[...])

        hs = []
        for g in range(n // SUBLANES):
            ag = a[g * SUBLANES:(g + 1) * SUBLANES]
            ug = u_in[g * SUBLANES:(g + 1) * SUBLANES]
            for sh in (1, 2, 4):
                keep = row >= sh
                a_sh = pltpu.roll(ag, sh, 0)
                u_sh = pltpu.roll(ug, sh, 0)
                ug = jnp.where(keep, ag * u_sh + ug, ug)
                ag = jnp.where(keep, ag * a_sh, ag)
            hg = ag * h_prev + ug
            hs.append(hg)
            h_prev = hg[SUBLANES - 1:SUBLANES, :]

        y = jnp.concatenate(hs, axis=0) * jax.nn.gelu(proj(1))
        ya_ref[sl, :] = _rms_norm(y, gna_ref[...]).astype(BF16)
        return h_prev

    h_last = h_carry[...]
    n_part = rows // INPROJ_PARTS
    for p in range(INPROJ_PARTS):
        h_last = part(p * n_part, n_part, h_last)
    h_carry[...] = h_last
    h_out_ref[...] = h_last

    tail = xp_buf[rows + pad - (CONV_W - 1):rows + pad, :]
    conv_out_ref[...] = tail
    xp_buf[pad - (CONV_W - 1):pad, :] = tail


def _inproj_prompt(x, mod3, w_in_bf, cos_t, sin_t, rg):
    b, s, _ = x.shape
    tile = 512
    nt = s // tile
    row_spec = lambda w: pl.BlockSpec((None, tile, w), lambda i, j: (i, j, 0))
    vec = lambda r, w: pl.BlockSpec((r, w), lambda i, j: (0, 0))
    outs = pl.pallas_call(
        _inproj_prompt_kernel,
        grid=(b, nt),
        in_specs=[row_spec(D_MODEL),
                  pl.BlockSpec((None, 6, D_MODEL), lambda i, j: (i, 0, 0)),
                  vec(D_MODEL, 5 * D_A),
                  pl.BlockSpec((tile, HEAD_PAIR_W), lambda i, j: (j, 0)),
                  pl.BlockSpec((tile, HEAD_PAIR_W), lambda i, j: (j, 0)),
                  vec(CONV_W, D_A), vec(1, D_A),
                  pl.BlockSpec((2, D_A // 2, D_A), lambda i, j: (0, 0, 0)),
                  vec(1, D_A), vec(1, D_A), vec(1, D_A), vec(1, D_A)],
        out_specs=[row_spec(D_A), row_spec(D_B), row_spec(D_B), row_spec(D_B),
                   pl.BlockSpec((None, CONV_W - 1, D_A), lambda i, j: (i, 0, 0)),
                   pl.BlockSpec((None, 1, D_A), lambda i, j: (i, 0, 0))],
        out_shape=[jax.ShapeDtypeStruct((b, s, D_A), BF16),
                   jax.ShapeDtypeStruct((b, s, D_B), F32),
                   jax.ShapeDtypeStruct((b, s, D_B), F32),
                   jax.ShapeDtypeStruct((b, s, D_B), F32),
                   jax.ShapeDtypeStruct((b, CONV_W - 1, D_A), F32),
                   jax.ShapeDtypeStruct((b, 1, D_A), F32)],
        scratch_shapes=[pltpu.VMEM((tile + SUBLANES, D_A), F32),
                        pltpu.VMEM((1, D_A), F32)],
        compiler_params=_cparams(("arbitrary", "arbitrary")),
        name="inproj_prompt",
    )(x, mod3, w_in_bf, cos_t, sin_t, rg["conv_w"], rg["conv_b"], rg["w_gate"],
      rg["b_a"], rg["b_x"], rg["lam"], rg["g_norm_a"])
    return outs


def _attn_prompt_kernel(q_ref, k_ref, v_ref, o_ref, kwin_ref, vwin_ref, acc_s, m_s, l_s, bias_s):
    s = q_ref.shape[0]
    keep = kwin_ref.shape[0]
    kwin_ref[...] = k_ref[s - keep:s, :]
    vwin_ref[...] = v_ref[s - keep:s, :]

    lane = lax.broadcasted_iota(jnp.int32, (Q_BLK, HEAD_PAIR_W), 1)
    head0 = lane < HEAD_DIM
    nk = 2 * Q_BLK

    qi = lax.broadcasted_iota(jnp.int32, (Q_BLK, nk), 0)
    ki = lax.broadcasted_iota(jnp.int32, (Q_BLK, nk), 1)
    for slot in range(2):
        dist = slot * Q_BLK + qi - ki
        bias_s[slot] = jnp.where((dist >= 0) & (dist <= N_KEYS), 0.0, NEG_INF)

    def rows(start, n, d):
        return pl.ds(start, n) if d == 1 else pl.ds(start, n, stride=d)

    def unit(bi, d, nb, u):
        r = u // nb
        j = u % nb
        jk = jnp.maximum(j - 1, 0)
        start_q = r + d * Q_BLK * j
        start_k = r + d * Q_BLK * jk
        bias = bias_s[j - jk]
        qb = q_ref[rows(start_q, Q_BLK, d), :] * (HEAD_DIM ** -0.5)
        kb = k_ref[rows(start_k, nk, d), :].astype(BF16)
        vb = v_ref[rows(start_k, nk, d), :].astype(BF16)
        q2 = jnp.concatenate([jnp.where(head0, qb, 0.0), jnp.where(head0, 0.0, qb)],
                             axis=0).astype(BF16)
        sc = lax.dot_general(q2, kb, (((1,), (1,)), ((), ())), preferred_element_type=F32)
        sc = sc + jnp.concatenate([bias, bias], axis=0)
        m = jnp.max(sc, axis=-1, keepdims=True)
        p = jnp.exp(sc - m)
        l = jnp.sum(p, axis=-1, keepdims=True)
        o = jnp.dot(p.astype(BF16), vb, preferred_element_type=F32)
        dst = rows(start_q, Q_BLK, d)
        acc_s[bi, dst, :] = jnp.where(head0, o[0:Q_BLK], o[Q_BLK:])
        m_s[bi, dst, :] = jnp.where(head0, m[0:Q_BLK], m[Q_BLK:])
        l_s[bi, dst, :] = jnp.where(head0, l[0:Q_BLK], l[Q_BLK:])

    for bi, (_, d) in enumerate(DILATED_CFG):
        nb = s // d // Q_BLK

        def body(u, carry, bi=bi, d=d, nb=nb):
            unit(bi, d, nb, u)
            return carry

        lax.fori_loop(0, d * nb, body, 0, unroll=ATTN_UNROLL)

    chunk = 512

    def merge(i, carry):
        r0 = pl.multiple_of(i * chunk, chunk)
        sl = pl.ds(r0, chunk)
        ms = [m_s[bi, sl, :] for bi in range(len(DILATED_CFG))]
        mx = functools.reduce(jnp.maximum, ms)
        num = jnp.zeros((chunk, HEAD_PAIR_W), F32)
        den = jnp.zeros((chunk, HEAD_PAIR_W), F32)
        for bi in range(len(DILATED_CFG)):
            w = jnp.exp(ms[bi] - mx)
            num = num + w * acc_s[bi, sl, :]
            den = den + w * l_s[bi, sl, :]
        o_ref[sl, :] = num / den
        return carry

    lax.fori_loop(0, s // chunk, merge, 0)


def _attn_prompt(q, k, v):
    b, s, _ = q.shape
    keep = min(WIN_MAX, s)
    nbr = len(DILATED_CFG)
    spec = pl.BlockSpec((None, s, HEAD_PAIR_W), lambda i, j: (i, 0, j))
    wspec = pl.BlockSpec((None, keep, HEAD_PAIR_W), lambda i, j: (i, 0, j))
    return pl.pallas_call(
        _attn_prompt_kernel,
        grid=(b, N_HEAD_PAIRS),
        in_specs=[spec, spec, spec],
        out_specs=[spec, wspec, wspec],
        out_shape=[jax.ShapeDtypeStruct((b, s, D_B), F32),
                   jax.ShapeDtypeStruct((b, keep, D_B), F32),
                   jax.ShapeDtypeStruct((b, keep, D_B), F32)],
        scratch_shapes=[pltpu.VMEM((nbr, s, HEAD_PAIR_W), F32),
                        pltpu.VMEM((nbr, s, HEAD_PAIR_W), F32),
                        pltpu.VMEM((nbr, s, HEAD_PAIR_W), F32),
                        pltpu.VMEM((2, Q_BLK, 2 * Q_BLK), F32)],
        compiler_params=_cparams(("arbitrary", "arbitrary")),
        name="attn_prompt",
    )(q, k, v)


def _inproj_sample_kernel(x_ref, mod_ref, w_in_ref, cos_ref, sin_ref, conv_state_ref, h0_ref,
                          conv_w_ref, conv_b_ref, wg_ref, b_a_ref, b_x_ref, lam_ref, gna_ref,
                          ya_ref, q_ref, k_ref, v_ref, conv_out_ref, h_out_ref):
    nt, nb, _ = x_ref.shape
    sh1 = mod_ref[:, 0:D_MODEL]
    sc1 = mod_ref[:, D_MODEL:2 * D_MODEL]
    u = (x_ref[...] * (1.0 + sc1)[None] + sh1[None]).astype(BF16).reshape(nt * nb, D_MODEL)

    def proj(j):
        return jnp.dot(u, w_in_ref[:, j * D_A:(j + 1) * D_A], preferred_element_type=F32)

    cos = cos_ref[...].reshape(nt * nb, D_B)
    sin = sin_ref[...].reshape(nt * nb, D_B)
    q_ref[...] = _rope_apply(proj(2), cos, sin).reshape(nt, nb, D_B)
    k_ref[...] = _rope_apply(proj(3), cos, sin).reshape(nt, nb, D_B)
    v_ref[...] = proj(4).reshape(nt, nb, D_B)

    xa = proj(0).reshape(nt, nb, D_A)
    xp = [conv_state_ref[j] for j in range(CONV_W - 1)] + [xa[t] for t in range(nt)]
    xc = jnp.concatenate(
        [conv_b_ref[...] + sum(xp[t + j] * conv_w_ref[j:j + 1, :] for j in range(CONV_W))
         for t in range(nt)], axis=0)
    for j in range(CONV_W - 1):
        conv_out_ref[j] = xp[nt + j]

    a, u_in = _rglru_gates(xc, wg_ref, b_a_ref[...], b_x_ref[...], lam_ref[...])
    h = h0_ref[...]
    hs = []
    for t in range(nt):
        h = a[t * nb:(t + 1) * nb] * h + u_in[t * nb:(t + 1) * nb]
        hs.append(h)
    h_out_ref[...] = h
    y = jnp.concatenate(hs, axis=0) * jax.nn.gelu(proj(1))
    ya_ref[...] = _rms_norm(y, gna_ref[...]).astype(BF16).reshape(nt, nb, D_A)


def _inproj_sample(x_tb, mod_s, w_in_bf, cos_t, sin_t, conv_state_tb, h0, rg):
    nt, nb, _ = x_tb.shape
    full = lambda shape: pl.BlockSpec(shape, lambda i: (0,) * len(shape))
    return pl.pallas_call(
        _inproj_sample_kernel,
        grid=(1,),
        in_specs=[full((nt, nb, D_MODEL)), full((nb, 6 * D_MODEL)), full((D_MODEL, 5 * D_A)),
                  full((nt, nb, D_B)), full((nt, nb, D_B)),
                  full((CONV_W - 1, nb, D_A)), full((nb, D_A)),
                  full((CONV_W, D_A)), full((1, D_A)), full((2, D_A // 2, D_A)),
                  full((1, D_A)), full((1, D_A)), full((1, D_A)), full((1, D_A))],
        out_specs=[full((nt, nb, D_A)), full((nt, nb, D_B)), full((nt, nb, D_B)),
                   full((nt, nb, D_B)), full((CONV_W - 1, nb, D_A)), full((nb, D_A))],
        out_shape=[jax.ShapeDtypeStruct((nt, nb, D_A), BF16),
                   jax.ShapeDtypeStruct((nt, nb, D_B), F32),
                   jax.ShapeDtypeStruct((nt, nb, D_B), F32),
                   jax.ShapeDtypeStruct((nt, nb, D_B), F32),
                   jax.ShapeDtypeStruct((CONV_W - 1, nb, D_A), F32),
                   jax.ShapeDtypeStruct((nb, D_A), F32)],
        compiler_params=_cparams(("arbitrary",)),
        name="inproj_sample",
    )(x_tb, mod_s, w_in_bf, cos_t, sin_t, conv_state_tb, h0, rg["conv_w"], rg["conv_b"],
      rg["w_gate"], rg["b_a"], rg["b_x"], rg["lam"], rg["g_norm_a"])


def _sample_attention_step(b, n_seq, q_ref, kn_ref, vn_ref, ck_hbm, cv_hbm, o_ref, kwin_hbm,
                           vwin_hbm, cin, cout, sem_in, sem_out, kn_pad, vn_pad):
    nt = q_ref.shape[0]
    n_buf = cin.shape[-1]
    n_rows = N_HEADS_B * nt
    srcs = (ck_hbm, cv_hbm)
    dsts = (kwin_hbm, vwin_hbm)
    slot = b % 2

    def copy_in(which, seq):
        s = seq % 2
        return pltpu.make_async_copy(srcs[which].at[seq], cin.at[s, which], sem_in.at[s, which])

    def copy_out(which, seq):
        s = seq % 2
        return pltpu.make_async_copy(cout.at[s, which], dsts[which].at[seq],
                                     sem_out.at[s, which])

    @pl.when(b == 0)
    def _():
        kn_pad[...] = jnp.zeros_like(kn_pad)
        vn_pad[...] = jnp.zeros_like(vn_pad)
        copy_in(0, 0).start()
        copy_in(1, 0).start()

    @pl.when(b + 1 < n_seq)
    def _():
        copy_in(0, b + 1).start()
        copy_in(1, b + 1).start()

    kn_pad[0:nt, :] = kn_ref[...]
    vn_pad[0:nt, :] = vn_ref[...]

    tail_lane = lax.broadcasted_iota(jnp.int32, (HEAD_DIM, LANES), 1)

    def shift_in(old_ref, new_pad, out_ref):
        new_t = pltpu.roll(new_pad.T, LANES - nt, 1)
        for h in range(N_HEADS_B):
            rolled = pltpu.roll(old_ref[h], n_buf - nt, 1)
            last = jnp.where(tail_lane < LANES - nt, rolled[:, n_buf - LANES:n_buf],
                             new_t[h * HEAD_DIM:(h + 1) * HEAD_DIM])
            out_ref[h, :, 0:n_buf - LANES] = rolled[:, 0:n_buf - LANES]
            out_ref[h, :, n_buf - LANES:n_buf] = last

    copy_in(0, b).wait()
    copy_in(1, b).wait()

    @pl.when(b >= 2)
    def _():
        copy_out(0, b - 2).wait()
        copy_out(1, b - 2).wait()

    ri = lax.broadcasted_iota(jnp.int32, (n_rows, nt), 0)
    ci = lax.broadcasted_iota(jnp.int32, (n_rows, nt), 1)
    pick = (ri % nt == ci).astype(BF16)
    qs = (q_ref[...] * (HEAD_DIM ** -0.5)).astype(BF16)
    q_rep = jnp.dot(pick, qs, preferred_element_type=F32)
    row_h = lax.broadcasted_iota(jnp.int32, (n_rows, D_B), 0) // nt
    lane_h = lax.broadcasted_iota(jnp.int32, (n_rows, D_B), 1) // HEAD_DIM
    own = row_h == lane_h
    qbd = jnp.where(own, q_rep, 0.0).astype(BF16)

    nt_dims = (((1,), (1,)), ((), ()))

    def mult(dist, limit_ok):
        c = jnp.zeros(dist.shape, F32)
        for win, d in DILATED_CFG:
            hit = (dist >= 0) & (dist <= win) & (dist % d == 0) & limit_ok
            c = c + hit.astype(F32)
        return c

    def weights(ck_t):
        sc_c = jnp.dot(qbd, ck_t.astype(BF16), preferred_element_type=F32)
        sc_n = lax.dot_general(qbd, kn_pad[...].astype(BF16), nt_dims,
                               preferred_element_type=F32)
        t_c = lax.broadcasted_iota(jnp.int32, sc_c.shape, 0) % nt
        dist_c = n_buf + t_c - lax.broadcasted_iota(jnp.int32, sc_c.shape, 1)
        mult_c = mult(dist_c, dist_c >= 0)
        t_n = lax.broadcasted_iota(jnp.int32, sc_n.shape, 0) % nt
        col_n = lax.broadcasted_iota(jnp.int32, sc_n.shape, 1)
        mult_n = mult(t_n - col_n, col_n < nt)
        sc_c = jnp.where(mult_c > 0, sc_c, NEG_INF)
        sc_n = jnp.where(mult_n > 0, sc_n, NEG_INF)
        m = jnp.maximum(jnp.max(sc_c, axis=-1, keepdims=True),
                        jnp.max(sc_n, axis=-1, keepdims=True))
        p_c = mult_c * jnp.exp(sc_c - m)
        p_n = mult_n * jnp.exp(sc_n - m)
        l = jnp.sum(p_c, axis=-1, keepdims=True) + jnp.sum(p_n, axis=-1, keepdims=True)
        return p_c.astype(BF16), p_n.astype(BF16), l

    p_c, p_n, l = weights(cin[slot, 0].reshape(D_B, n_buf))
    shift_in(cin.at[slot, 0], kn_pad[...], cout.at[slot, 0])
    acc = (lax.dot_general(p_c, cin[slot, 1].reshape(D_B, n_buf).astype(BF16), nt_dims,
                           preferred_element_type=F32)
           + jnp.dot(p_n, vn_pad[...].astype(BF16), preferred_element_type=F32))
    shift_in(cin.at[slot, 1], vn_pad[...], cout.at[slot, 1])
    o_full = jnp.where(own, acc / l, 0.0)
    out = o_full[0:nt, :]
    for h in range(1, N_HEADS_B):
        out = out + o_full[h * nt:(h + 1) * nt, :]
    o_ref[...] = out

    copy_out(0, b).start()
    copy_out(1, b).start()

    @pl.when(b == n_seq - 1)
    def _():
        @pl.when(b >= 1)
        def _():
            copy_out(0, b - 1).wait()
            copy_out(1, b - 1).wait()
        copy_out(0, b).wait()
        copy_out(1, b).wait()


def _split_bf16(x):
    hi = x.astype(BF16)
    lo = (x - hi.astype(F32)).astype(BF16)
    return hi, lo


def _first_argmax(vals):
    mx = functools.reduce(jnp.maximum, vals)
    idx = jnp.full(mx.shape, float(len(vals) - 1), F32)
    for j in range(len(vals) - 2, -1, -1):
        idx = jnp.where(vals[j] == mx, float(j), idx)
    return mx, idx


def _outproj_kernel(ya_ref, yb_ref, x_ref, gt1_ref, sh2_ref, sc2_ref, w_out_ref, gnb_ref,
                    ln_g_ref, ln_b_ref, w_r_ref, b_r_ref, x1_ref, u2_ref, route_ref):
    yb = _rms_norm(yb_ref[...], gnb_ref[...]).astype(BF16)
    mixed = (jnp.dot(ya_ref[...], w_out_ref[0:D_A, :], preferred_element_type=F32)
             + jnp.dot(yb, w_out_ref[D_A:D_A + D_B, :], preferred_element_type=F32))
    x1 = _layer_norm(DN_ALPHA * x_ref[...] + gt1_ref[...] * mixed, ln_g_ref[...], ln_b_ref[...])
    x1_ref[...] = x1
    u2 = x1 * (1.0 + sc2_ref[...]) + sh2_ref[...]
    _store_token_tiles(u2_ref, u2, TOKEN_PITCH)

    u_hi, u_lo = _split_bf16(u2)
    w_hi, w_lo = _split_bf16(w_r_ref[...])
    both = jnp.dot(u_hi, jnp.concatenate([w_hi, w_lo], axis=1), preferred_element_type=F32)
    logits = (both[:, 0:ROUTE_W] + jnp.dot(u_lo, w_hi, preferred_element_type=F32)
              + both[:, ROUTE_W:2 * ROUTE_W]) + b_r_ref[...]
    lt = logits.T

    g_rows = [lt[j:j + 1, :] for j in range(N_GROUPS)]
    g_max, g_idx = _first_argmax(g_rows)
    p_group = 1.0 / sum(jnp.exp(g - g_max) for g in g_rows)
    e_rows = []
    for e in range(N_EXP_PER_GROUP):
        acc = jnp.zeros_like(g_max)
        for g in range(N_GROUPS):
            r = N_GROUPS + g * N_EXP_PER_GROUP + e
            acc = jnp.where(g_idx == float(g), lt[r:r + 1, :], acc)
        e_rows.append(acc)
    v1, i1 = _first_argmax(e_rows)
    rest = [jnp.where(i1 == float(e), -jnp.inf, e_rows[e]) for e in range(N_EXP_PER_GROUP)]
    v2, i2 = _first_argmax(rest)
    ex = jnp.exp(v2 - v1)
    w1 = p_group / (1.0 + ex)
    w2 = p_group * ex / (1.0 + ex)
    lo = jnp.minimum(i1, i2)
    hi = jnp.maximum(i1, i2)
    pair = jnp.where(lo == 0.0, hi - 1.0, jnp.where(lo == 1.0, hi + 1.0, 5.0))
    cls = g_idx * float(N_PAIRS) + pair
    w_of_lo = jnp.where(i1 < i2, w1, w2)
    w_of_hi = jnp.where(i1 < i2, w2, w1)
    e_lo = g_idx * float(N_EXP_PER_GROUP) + lo
    e_hi = g_idx * float(N_EXP_PER_GROUP) + hi
    n_tok = cls.shape[1]
    route = jnp.concatenate(
        [cls, w_of_lo, w_of_hi, e_lo, e_hi, jnp.zeros((ROUTE_W - 5, n_tok), F32)], axis=0)
    route_ref[...] = route[0:SUBLANES, :]
    u2_ref[pl.ds(TOKEN_ROWS, n_tok, stride=TOKEN_PITCH), :] = route.T


def _outproj(ya, yb, x, mods, w_out_bf, gnb, ln_g, ln_b, w_r, b_r, tile):
    n = x.shape[0]
    nt = n // tile
    mod_arr, gt1_spec, sh2_spec, sc2_spec = mods
    row = lambda w: pl.BlockSpec((tile, w), lambda i: (i, 0))
    vec = lambda r, w: pl.BlockSpec((r, w), lambda i: (0, 0))
    return pl.pallas_call(
        _outproj_kernel,
        grid=(nt,),
        in_specs=[row(D_A), row(D_B), row(D_MODEL), gt1_spec, sh2_spec, sc2_spec,
                  vec(D_MODEL, D_MODEL), vec(1, D_B), vec(1, D_MODEL), vec(1, D_MODEL),
                  vec(D_MODEL, ROUTE_W), vec(1, ROUTE_W)],
        out_specs=[row(D_MODEL),
                   pl.BlockSpec((tile * TOKEN_PITCH, LANES), lambda i: (i, 0)),
                   pl.BlockSpec((None, SUBLANES, tile), lambda i: (i, 0, 0))],
        out_shape=[jax.ShapeDtypeStruct((n, D_MODEL), F32),
                   jax.ShapeDtypeStruct((n * TOKEN_PITCH, LANES), F32),
                   jax.ShapeDtypeStruct((nt, SUBLANES, tile), F32)],
        compiler_params=_cparams(("arbitrary",)),
        name="outproj_router",
    )(ya, yb, x, mod_arr, mod_arr, mod_arr, w_out_bf, gnb, ln_g, ln_b, w_r, b_r)


def _moe_kernel(e_lo_ref, e_hi_ref, n_used_ref, src_ref, dst_ref,
                x_hbm, wg_lo, wg_hi, wu_lo, wu_hi, wd_lo, wd_hi,
                q_ref, kn_ref, vn_ref, ck_hbm, cv_hbm,
                o_hbm, yb_ref, kwin_hbm, vwin_hbm,
                xg, og, gsem, ssem, cin, cout, csem_in, csem_out, kn_pad, vn_pad,
                *, n_tokens, tile, n_seq):
    i = pl.program_id(0)
    n_steps = pl.num_programs(0)
    n_used = n_used_ref[0]
    cur = i % N_BUF
    ahead = (i + 2) % N_BUF
    rows = tile * TOKEN_ROWS

    def gather_token(base, r, buf):
        pltpu.make_async_copy(x_hbm.at[pl.ds(src_ref[base + r], TOKEN_PITCH)],
                              xg.at[buf, pl.ds(r * TOKEN_PITCH, TOKEN_PITCH)],
                              gsem.at[buf]).start()

    def gather_wait(buf):
        pltpu.make_async_copy(x_hbm.at[pl.ds(0, tile * TOKEN_PITCH)], xg.at[buf],
                              gsem.at[buf]).wait()

    def scatter_wait(buf):
        pltpu.make_async_copy(og.at[buf], o_hbm.at[pl.ds(0, rows)], ssem.at[buf]).wait()

    @pl.when(i == 0)
    def _():
        def first(r, c):
            gather_token(0, r, 0)
            gather_token(jnp.minimum(1, n_steps - 1) * tile, r, 1)
            return c
        lax.fori_loop(0, tile, first, 0, unroll=DMA_UNROLL)
        og[...] = jnp.zeros_like(og)
        for buf in range(N_BUF):
            pad_rows = pltpu.make_async_copy(
                og.at[buf], o_hbm.at[pl.ds((n_tokens + buf * tile) * TOKEN_ROWS, rows)],
                ssem.at[buf])
            pad_rows.start()
            pad_rows.wait()

    def expert_tile():
        nxt = jnp.minimum(i + 2, n_steps - 1) * tile
        for r in range(tile):
            gather_token(nxt, r, ahead)
        prev = i * tile
        for r in range(tile):
            row = pl.multiple_of(dst_ref[prev + r], TOKEN_ROWS)
            pltpu.make_async_copy(og.at[ahead, pl.ds(r * TOKEN_ROWS, TOKEN_ROWS)],
                                  o_hbm.at[pl.ds(row, TOKEN_ROWS)], ssem.at[ahead]).start()

        xb = _load_token_tiles(xg.at[cur], tile, TOKEN_PITCH).astype(BF16)
        w2 = xg[cur, pl.ds(TOKEN_ROWS, tile, stride=TOKEN_PITCH), :]
        out = jnp.zeros((tile, D_MODEL), F32)
        for col, wg, wu, wd in ((1, wg_lo, wu_lo, wd_lo), (2, wg_hi, wu_hi, wd_hi)):
            hg = jnp.dot(xb, wg[...], preferred_element_type=F32)
            hu = jnp.dot(xb, wu[...], preferred_element_type=F32)
            act = (hg * jax.nn.sigmoid(hg)) * hu * w2[:, col:col + 1]
            out = out + jnp.dot(act.astype(BF16), wd[...], preferred_element_type=F32)
        _store_token_tiles(og.at[cur], out)

    @pl.when(i <= n_used)
    def _():
        gather_wait(cur)

        @pl.when(i >= 2)
        def _():
            scatter_wait(cur)

        expert_tile()

        @pl.when(i == n_used)
        def _():
            for buf in range(N_BUF):
                @pl.when(buf != cur)
                def _():
                    gather_wait(buf)
                    scatter_wait(buf)

    @pl.when(i < n_seq)
    def _():
        _sample_attention_step(i, n_seq, q_ref, kn_ref, vn_ref, ck_hbm, cv_hbm, yb_ref,
                               kwin_hbm, vwin_hbm, cin, cout, csem_in, csem_out, kn_pad, vn_pad)


def _moe(u2t, cls, w_gate_bf, w_up_bf, w_down_bf, tile, q_s, k_new, v_new, cache_k_t, cache_v_t):
    n = cls.shape[0]
    n_seq, nt, _ = q_s.shape
    n_buf = cache_k_t.shape[-1]
    n_steps = n // tile + N_CLASSES
    assert n_steps >= n_seq, "one sample sequence per grid step"
    cls = cls.astype(jnp.int32)
    order = jnp.argsort(cls, stable=True).astype(jnp.int32)
    class_ids = jnp.arange(N_CLASSES, dtype=jnp.int32)
    counts = jnp.sum((cls[:, None] == class_ids[None, :]).astype(jnp.int32), axis=0)
    tiles_per = (counts + tile - 1) // tile
    tile_end = jnp.cumsum(tiles_per)
    tile_off = tile_end - tiles_per
    n_used = tile_end[-1]
    class_start = jnp.cumsum(counts) - counts
    step = jnp.arange(n_steps, dtype=jnp.int32)
    step_c = jnp.minimum(step, n_used - 1)
    cls_of = jnp.sum((step_c[:, None] >= tile_end[None, :]).astype(jnp.int32), axis=1)
    onehot = (cls_of[:, None] == class_ids[None, :]).astype(jnp.int32)
    pick = lambda table: jnp.sum(onehot * table[None, :], axis=1)
    local = step - pick(tile_off)
    nvalid = jnp.where(step < n_used, jnp.clip(pick(counts) - local * tile, 0, tile), 0)
    r = jnp.arange(tile, dtype=jnp.int32)
    pos = pick(class_start)[:, None] + local[:, None] * tile + r[None, :]
    valid = r[None, :] < nvalid[:, None]
    tok = order[jnp.clip(pos, 0, n - 1)]
    src = (jnp.where(valid, tok, 0) * TOKEN_PITCH).astype(jnp.int32).reshape(-1)
    spare = n + (step[:, None] % N_BUF) * tile + r[None, :]
    dst = jnp.where(valid, tok, spare)
    dst = jnp.concatenate([(n + (N_BUF - 1) * tile + r)[None, :], dst], axis=0)
    dst = (dst * TOKEN_ROWS).astype(jnp.int32).reshape(-1)
    grp = cls_of // N_PAIRS
    pair = cls_of % N_PAIRS
    pair_lo = (pair >= 3).astype(jnp.int32) + (pair >= 5).astype(jnp.int32)
    pair_hi = pair + 1 - 2 * (pair >= 3).astype(jnp.int32) - (pair >= 5).astype(jnp.int32)
    e_lo = (grp * N_EXP_PER_GROUP + pair_lo).astype(jnp.int32)
    e_hi = (grp * N_EXP_PER_GROUP + pair_hi).astype(jnp.int32)

    w_in_spec = lambda which: pl.BlockSpec(
        (None, D_MODEL, D_EXPERT), lambda i, elo, ehi, nu, s, d: ((elo, ehi)[which][i], 0, 0))
    w_dn_spec = lambda which: pl.BlockSpec(
        (None, D_EXPERT, D_MODEL), lambda i, elo, ehi, nu, s, d: ((elo, ehi)[which][i], 0, 0))
    any_spec = pl.BlockSpec(memory_space=pl.ANY)
    seq_spec = pl.BlockSpec((None, nt, D_B),
                            lambda i, elo, ehi, nu, s, d: (jnp.minimum(i, n_seq - 1), 0, 0))
    win_shape = jax.ShapeDtypeStruct((n_seq, N_HEADS_B, HEAD_DIM, n_buf), F32)
    grid_spec = pltpu.PrefetchScalarGridSpec(
        num_scalar_prefetch=5,
        grid=(n_steps,),
        in_specs=[any_spec,
                  w_in_spec(0), w_in_spec(1), w_in_spec(0), w_in_spec(1),
                  w_dn_spec(0), w_dn_spec(1),
                  seq_spec, seq_spec, seq_spec, any_spec, any_spec],
        out_specs=[any_spec, seq_spec, any_spec, any_spec],
        scratch_shapes=[pltpu.VMEM((N_BUF, tile * TOKEN_PITCH, LANES), F32),
                        pltpu.VMEM((N_BUF, tile * TOKEN_ROWS, LANES), F32),
                        pltpu.SemaphoreType.DMA((N_BUF,)),
                        pltpu.SemaphoreType.DMA((N_BUF,)),
                        pltpu.VMEM((2, 2, N_HEADS_B, HEAD_DIM, n_buf), F32),
                        pltpu.VMEM((2, 2, N_HEADS_B, HEAD_DIM, n_buf), F32),
                        pltpu.SemaphoreType.DMA((2, 2)),
                        pltpu.SemaphoreType.DMA((2, 2)),
                        pltpu.VMEM((LANES, D_B), F32),
                        pltpu.VMEM((LANES, D_B), F32)],
    )
    return pl.pallas_call(
        functools.partial(_moe_kernel, n_tokens=n, tile=tile, n_seq=n_seq),
        grid_spec=grid_spec,
        out_shape=[jax.ShapeDtypeStruct(((n + N_BUF * tile) * TOKEN_ROWS, LANES), F32),
                   jax.ShapeDtypeStruct((n_seq, nt, D_B), F32), win_shape, win_shape],
        compiler_params=_cparams(("arbitrary",)),
        name="moe_sparse",
    )(e_lo, e_hi, n_used.reshape(1).astype(jnp.int32), src, dst,
      u2t, w_gate_bf, w_gate_bf, w_up_bf, w_up_bf, w_down_bf, w_down_bf,
      q_s, k_new, v_new, cache_k_t, cache_v_t)


def _moe_dense_kernel(x_ref, wg_ref, wu_ref, wd_ref, o_ref):
    e = pl.program_id(0)
    n = o_ref.shape[0]

    @pl.when(e == 0)
    def _():
        o_ref[...] = jnp.zeros_like(o_ref)

    xb = _load_token_tiles(x_ref, n, TOKEN_PITCH).astype(BF16)
    ef = e.astype(F32)
    route = x_ref[pl.ds(TOKEN_ROWS, n, stride=TOKEN_PITCH), :]
    col = lambda c: route[:, c:c + 1]
    comb = jnp.where(col(3) == ef, col(1), 0.0) + jnp.where(col(4) == ef, col(2), 0.0)
    hg = jnp.dot(xb, wg_ref[...], preferred_element_type=F32)
    hu = jnp.dot(xb, wu_ref[...], preferred_element_type=F32)
    act = (hg * jax.nn.sigmoid(hg)) * hu * comb
    o_ref[...] += jnp.dot(act.astype(BF16), wd_ref[...], preferred_element_type=F32)


def _moe_dense(u2t, w_gate_bf, w_up_bf, w_down_bf):
    n = u2t.shape[0] // TOKEN_PITCH
    return pl.pallas_call(
        _moe_dense_kernel,
        grid=(N_EXPERTS,),
        in_specs=[pl.BlockSpec((n * TOKEN_PITCH, LANES), lambda e: (0, 0)),
                  pl.BlockSpec((None, D_MODEL, D_EXPERT), lambda e: (e, 0, 0)),
                  pl.BlockSpec((None, D_MODEL, D_EXPERT), lambda e: (e, 0, 0)),
                  pl.BlockSpec((None, D_EXPERT, D_MODEL), lambda e: (e, 0, 0))],
        out_specs=pl.BlockSpec((n, D_MODEL), lambda e: (0, 0)),
        out_shape=jax.ShapeDtypeStruct((n, D_MODEL), F32),
        compiler_params=_cparams(("arbitrary",)),
        name="moe_dense",
    )(u2t, w_gate_bf, w_up_bf, w_down_bf)


def _final_kernel(x1_ref, ffn_ref, gt2_ref, g_ref, b_ref, o_ref, *, token_tiled):
    rows = x1_ref.shape[0]
    ffn = _load_token_tiles(ffn_ref, rows) if token_tiled else ffn_ref[...]
    o_ref[...] = _layer_norm(DN_ALPHA * x1_ref[...] + gt2_ref[...] * ffn, g_ref[...], b_ref[...])


def _final_norm(x1, ffn, mod_arr, gt2_spec, ln_g, ln_b, tile):
    n = x1.shape[0]
    token_tiled = ffn.shape[-1] == LANES
    row = pl.BlockSpec((tile, D_MODEL), lambda i: (i, 0))
    ffn_spec = pl.BlockSpec((tile * TOKEN_ROWS, LANES), lambda i: (i, 0)) if token_tiled else row
    vec = pl.BlockSpec((1, D_MODEL), lambda i: (0, 0))
    return pl.pallas_call(
        functools.partial(_final_kernel, token_tiled=token_tiled),
        grid=(n // tile,),
        in_specs=[row, ffn_spec, gt2_spec, vec, vec],
        out_specs=row,
        out_shape=jax.ShapeDtypeStruct((n, D_MODEL), F32),
        compiler_params=_cparams(("arbitrary",)),
        name="final_norm",
    )(x1, ffn, mod_arr, ln_g, ln_b)


def _rope_tables(pos):
    half = HEAD_DIM // 2
    inv = ROPE_THETA ** (-jnp.arange(half, dtype=F32) * 2.0 / HEAD_DIM)
    ang = pos.astype(F32)[:, None] * inv[None, :]
    cos = jnp.cos(ang)
    sin = jnp.sin(ang)
    cos_t = jnp.tile(jnp.concatenate([cos, cos], axis=-1), (1, 2))
    sin_t = jnp.tile(jnp.concatenate([-sin, sin], axis=-1), (1, 2))
    return cos_t, sin_t


def _block_diag(w):
    n, a, b = w.shape
    eye = jnp.eye(n, dtype=w.dtype)
    return (eye[:, None, :, None] * w[:, :, None, :]).reshape(n * a, n * b)


def _prepare_weights(w_in, conv_w, conv_b, w_rg_a, b_rg_a, w_rg_x, b_rg_x, rg_lambda, g_norm_a,
                     w_router_group, b_router_group, w_router_expert, b_router_expert):
    half_blocks = N_BLK_A // 2
    w_gate = jnp.stack([
        jnp.concatenate([_block_diag(w_rg_a[h * half_blocks:(h + 1) * half_blocks]),
                         _block_diag(w_rg_x[h * half_blocks:(h + 1) * half_blocks])], axis=1)
        for h in range(2)]).astype(BF16)
    rg = dict(conv_w=conv_w, conv_b=conv_b.reshape(1, D_A), w_gate=w_gate,
              b_a=b_rg_a.reshape(1, D_A), b_x=b_rg_x.reshape(1, D_A),
              lam=rg_lambda.reshape(1, D_A), g_norm_a=g_norm_a.reshape(1, D_A))
    n_logits = N_GROUPS + N_EXPERTS
    w_r = jnp.concatenate(
        [w_router_group,
         w_router_expert.transpose(1, 0, 2).reshape(D_MODEL, N_EXPERTS),
         jnp.zeros((D_MODEL, ROUTE_W - n_logits), F32)], axis=1)
    b_r = jnp.concatenate([b_router_group, b_router_expert.reshape(-1),
                           jnp.zeros((ROUTE_W - n_logits,), F32)]).reshape(1, ROUTE_W)
    return rg, w_r, b_r


def kernel(x_prompt, x_sample, state_conv, state_rglru, cache_win_k, cache_win_v, c_prompt, c_sample, w_ada, b_ada, w_in, conv_w, conv_b, w_rg_a, b_rg_a, w_rg_x, b_rg_x, rg_lambda, g_norm_a, g_norm_b, w_out, ln1_g, ln1_b, w_router_group, b_router_group, w_router_expert, b_router_expert, w_exp_gate, w_exp_up, w_exp_down, ln2_g, ln2_b):
    bp, sp, _ = x_prompt.shape
    bs, ts, _ = x_sample.shape
    n_buf = cache_win_k.shape[1]

    rg, w_r, b_r = _prepare_weights(w_in, conv_w, conv_b, w_rg_a, b_rg_a, w_rg_x, b_rg_x,
                                    rg_lambda, g_norm_a, w_router_group, b_router_group,
                                    w_router_expert, b_router_expert)
    w_in_bf = w_in.astype(BF16)
    outproj_weights = (w_out.astype(BF16), g_norm_b.reshape(1, D_B), ln1_g.reshape(1, D_MODEL),
                       ln1_b.reshape(1, D_MODEL), w_r, b_r)
    expert_weights = _cast_expert_weights(w_exp_gate, w_exp_up, w_exp_down)
    ln2 = (ln2_g.reshape(1, D_MODEL), ln2_b.reshape(1, D_MODEL))

    mod = _modulation(jnp.concatenate([c_prompt, c_sample], axis=0), w_ada, b_ada)
    mod_p, mod_s = mod[:bp], mod[bp:]

    tile_p = 512
    cos_p, sin_p = _rope_tables(jnp.arange(sp))
    ya_p, q_p, k_p, v_p, conv_p, h_p = _inproj_prompt(
        x_prompt, mod_p.reshape(bp, 6, D_MODEL), w_in_bf, cos_p, sin_p, rg)
    yb_p, kwin_p, vwin_p = _attn_prompt(q_p, k_p, v_p)
    tiles_per_seq = sp // tile_p
    mod_p3 = mod_p.reshape(bp * 6, 1, D_MODEL)
    mod_spec_p = lambda j: pl.BlockSpec((None, 1, D_MODEL),
                                        lambda i: ((i // tiles_per_seq) * 6 + j, 0, 0))
    x1_p, u2t_p, route_p = _outproj(
        ya_p.reshape(bp * sp, D_A), yb_p.reshape(bp * sp, D_B), x_prompt.reshape(bp * sp, D_MODEL),
        (mod_p3, mod_spec_p(2), mod_spec_p(3), mod_spec_p(4)), *outproj_weights, tile_p)

    cos_s, sin_s = _rope_tables(PAST_LEN + jnp.arange(ts))
    tb = lambda t: jnp.broadcast_to(jnp.tile(t, (1, N_HEAD_PAIRS))[:, None, :], (ts, bs, D_B))
    ya_s, q_s, k_s, v_s, conv_s, h_s = _inproj_sample(
        x_sample.transpose(1, 0, 2), mod_s, w_in_bf, tb(cos_s), tb(sin_s),
        state_conv.transpose(1, 0, 2), state_rglru, rg)
    bt = lambda t: t.transpose(1, 0, 2)

    ffn_p, yb_s, kwin_s, vwin_s = _moe(
        u2t_p, route_p[:, 0, :].reshape(-1), *expert_weights, 256,
        bt(q_s), bt(k_s), bt(v_s), cache_win_k.transpose(0, 2, 3, 1),
        cache_win_v.transpose(0, 2, 3, 1))
    y_p = _final_norm(x1_p, ffn_p, mod_p3, mod_spec_p(5), *ln2, tile_p)

    mod_spec_s = lambda j: pl.BlockSpec((bs, D_MODEL), lambda i: (0, j))
    x1_s, u2t_s, _ = _outproj(
        ya_s.reshape(ts * bs, D_A), bt(yb_s).reshape(ts * bs, D_B),
        x_sample.transpose(1, 0, 2).reshape(ts * bs, D_MODEL),
        (mod_s, mod_spec_s(2), mod_spec_s(3), mod_spec_s(4)), *outproj_weights, bs)
    ffn_s = _moe_dense(u2t_s, *expert_weights)
    y_s = _final_norm(x1_s, ffn_s, mod_s, mod_spec_s(5), *ln2, bs)

    heads = lambda t: t.reshape(t.shape[0], t.shape[1], N_HEADS_B, HEAD_DIM)
    return (y_p.reshape(bp, sp, D_MODEL), bt(y_s.reshape(ts, bs, D_MODEL)),
            conv_p, h_p.reshape(bp, D_A), heads(kwin_p), heads(vwin_p),
            bt(conv_s), h_s, kwin_s.transpose(0, 3, 1, 2), vwin_s.transpose(0, 3, 1, 2))
```

```python
import functools
import math

import jax
import jax.numpy as jnp
import numpy as np
from jax import lax
from jax.experimental import pallas as pl
from jax.experimental.pallas import tpu as pltpu

F32 = jnp.float32
BF16 = jnp.bfloat16

D_MODEL = 1024
D_A = 512
N_BLK_A = 8
BLK_W_A = D_A // N_BLK_A
CONV_W = 4
RG_C = 8.0
D_B = 512
HEAD_DIM = 64
N_HEADS_B = D_B // HEAD_DIM
DILATED_CFG = ((128, 1), (512, 4), (2048, 16))
WIN_MAX = 2048
N_KEYS = 128
ROPE_THETA = 10000.0
PAST_LEN = 8192
N_GROUPS = 4
N_EXP_PER_GROUP = 4
N_EXPERTS = N_GROUPS * N_EXP_PER_GROUP
D_EXPERT = 512
DN_ALPHA = 2.0 ** 0.25
LN_EPS = 1e-5
NEG_INF = -1e30

N_PAIRS = 6
N_CLASSES = N_GROUPS * N_PAIRS
LANES = 128
SUBLANES = 8
HEAD_PAIR_W = 2 * HEAD_DIM
N_HEAD_PAIRS = N_HEADS_B // 2
Q_BLK = 128
ATTN_UNROLL = 32
DMA_UNROLL = 8
INPROJ_PARTS = 1
ROUTE_W = LANES
TOKEN_ROWS = D_MODEL // LANES
TOKEN_PITCH = TOKEN_ROWS + 1
N_BUF = 3
VMEM_LIMIT = 60 * 1024 * 1024


def _cparams(sem):
    return pltpu.CompilerParams(dimension_semantics=sem, vmem_limit_bytes=VMEM_LIMIT)


def _mod_kernel(c_ref, w_ref, b_ref, o_ref):
    c = c_ref[...]
    s = (c * jax.nn.sigmoid(c)).astype(BF16)
    o_ref[...] = jnp.dot(s, w_ref[...].astype(BF16), preferred_element_type=F32) + b_ref[...]


def _modulation(c_all, w_ada, b_ada):
    n = c_all.shape[0]
    tn = 1024
    return pl.pallas_call(
        _mod_kernel,
        grid=(6 * D_MODEL // tn,),
        in_specs=[pl.BlockSpec((n, D_MODEL), lambda j: (0, 0)),
                  pl.BlockSpec((D_MODEL, tn), lambda j: (0, j)),
                  pl.BlockSpec((1, tn), lambda j: (0, j))],
        out_specs=pl.BlockSpec((n, tn), lambda j: (0, j)),
        out_shape=jax.ShapeDtypeStruct((n, 6 * D_MODEL), F32),
        compiler_params=_cparams(("arbitrary",)),
        name="adaln_mod",
    )(c_all, w_ada, b_ada.reshape(1, -1))


def _rope_apply(t, cos, sin_signed):
    lane = lax.broadcasted_iota(jnp.int32, t.shape, t.ndim - 1)
    first_half = (lane & (HEAD_DIM - 1)) < HEAD_DIM // 2
    width = t.shape[-1]
    swapped = jnp.where(first_half,
                        pltpu.roll(t, width - HEAD_DIM // 2, t.ndim - 1),
                        pltpu.roll(t, HEAD_DIM // 2, t.ndim - 1))
    return t * cos + swapped * sin_signed


def _rglru_gates(xc, wg_ref, b_a, b_x, lam):
    half = D_A // 2
    xcb = xc.astype(BF16)
    g0 = jnp.dot(xcb[:, :half], wg_ref[0], preferred_element_type=F32)
    g1 = jnp.dot(xcb[:, half:], wg_ref[1], preferred_element_type=F32)
    r = jax.nn.sigmoid(jnp.concatenate([g0[:, :half], g1[:, :half]], axis=1) + b_a)
    i = jax.nn.sigmoid(jnp.concatenate([g0[:, half:], g1[:, half:]], axis=1) + b_x)
    z = -lam
    softplus = jnp.maximum(z, 0.0) + jnp.log1p(jnp.exp(-jnp.abs(z)))
    log_a = -RG_C * r * softplus
    a = jnp.exp(log_a)
    one_minus_a2 = -jnp.tanh(log_a) * (a * a + 1.0)
    u = jnp.sqrt(one_minus_a2) * (i * xc)
    return a, u


def _store_token_tiles(ref, x, pitch=TOKEN_ROWS):
    n = x.shape[0]
    for c in range(TOKEN_ROWS):
        ref[pl.ds(c, n, stride=pitch), :] = x[:, c * LANES:(c + 1) * LANES]


def _load_token_tiles(ref, n, pitch=TOKEN_ROWS):
    return jnp.concatenate(
        [ref[pl.ds(c, n, stride=pitch), :] for c in range(TOKEN_ROWS)], axis=1)


def _rms_norm(y, g):
    return y * lax.rsqrt(jnp.mean(y * y, axis=-1, keepdims=True) + LN_EPS) * g


def _layer_norm(x, g, b):
    mu = jnp.mean(x, axis=-1, keepdims=True)
    xc = x - mu
    var = jnp.mean(xc * xc, axis=-1, keepdims=True)
    return xc * lax.rsqrt(var + LN_EPS) * g + b


def _inproj_prompt_kernel(x_ref, mod_ref, w_in_ref, cos_ref, sin_ref, conv_w_ref, conv_b_ref,
                          wg_ref, b_a_ref, b_x_ref, lam_ref, gna_ref,
                          ya_ref, q_ref, k_ref, v_ref, conv_out_ref, h_out_ref,
                          xp_buf, h_carry):
    t = pl.program_id(1)
    rows = x_ref.shape[0]
    pad = SUBLANES

    @pl.when(t == 0)
    def _():
        xp_buf[0:pad, :] = jnp.zeros((pad, D_A), F32)
        h_carry[...] = jnp.zeros_like(h_carry)

    row = lax.broadcasted_iota(jnp.int32, (SUBLANES, D_A), 0)

    def part(r0, n, h_prev):
        sl = slice(r0, r0 + n)
        u = (x_ref[sl, :] * (1.0 + mod_ref[1:2, :]) + mod_ref[0:1, :]).astype(BF16)

        def proj(j):
            return jnp.dot(u, w_in_ref[:, j * D_A:(j + 1) * D_A], preferred_element_type=F32)

        cos = jnp.concatenate([cos_ref[sl, :]] * N_HEAD_PAIRS, axis=1)
        sin = jnp.concatenate([sin_ref[sl, :]] * N_HEAD_PAIRS, axis=1)
        q_ref[sl, :] = _rope_apply(proj(2), cos, sin)
        k_ref[sl, :] = _rope_apply(proj(3), cos, sin)
        v_ref[sl, :] = proj(4)

        xa = proj(0)
        xp_buf[pad + r0:pad + r0 + n, :] = xa
        xc = conv_b_ref[...] + xa * conv_w_ref[CONV_W - 1:CONV_W, :]
        for j in range(CONV_W - 1):
            off = pad - (CONV_W - 1) + j + r0
            xc = xc + xp_buf[off:off + n, :] * conv_w_ref[j:j + 1, :]

        a, u_in = _rglru_gates(xc, wg_ref, b_a_ref[...], b_x_ref[...], lam_ref[...])

        hs = []
        for g in range(n // SUBLANES):
            ag = a[g * SUBLANES:(g + 1) * SUBLANES]
            ug = u_in[g * SUBLANES:(g + 1) * SUBLANES]
            for sh in (1, 2, 4):
                keep = row >= sh
                a_sh = pltpu.roll(ag, sh, 0)
                u_sh = pltpu.roll(ug, sh, 0)
                ug = jnp.where(keep, ag * u_sh + ug, ug)
                ag = jnp.where(keep, ag * a_sh, ag)
            hg = ag * h_prev + ug
            hs.append(hg)
            h_prev = hg[SUBLANES - 1:SUBLANES, :]

        y = jnp.concatenate(hs, axis=0) * jax.nn.gelu(proj(1))
        ya_ref[sl, :] = _rms_norm(y, gna_ref[...]).astype(BF16)
        return h_prev

    h_last = h_carry[...]
    n_part = rows // INPROJ_PARTS
    for p in range(INPROJ_PARTS):
        h_last = part(p * n_part, n_part, h_last)
    h_carry[...] = h_last
    h_out_ref[...] = h_last

    tail = xp_buf[rows + pad - (CONV_W - 1):rows + pad, :]
    conv_out_ref[...] = tail
    xp_buf[pad - (CONV_W - 1):pad, :] = tail


def _inproj_prompt(x, mod3, w_in_bf, cos_t, sin_t, rg):
    b, s, _ = x.shape
    tile = 512
    nt = s // tile
    row_spec = lambda w: pl.BlockSpec((None, tile, w), lambda i, j: (i, j, 0))
    vec = lambda r, w: pl.BlockSpec((r, w), lambda i, j: (0, 0))
    outs = pl.pallas_call(
        _inproj_prompt_kernel,
        grid=(b, nt),
        in_specs=[row_spec(D_MODEL),
                  pl.BlockSpec((None, 6, D_MODEL), lambda i, j: (i, 0, 0)),
                  vec(D_MODEL, 5 * D_A),
                  pl.BlockSpec((tile, HEAD_PAIR_W), lambda i, j: (j, 0)),
                  pl.BlockSpec((tile, HEAD_PAIR_W), lambda i, j: (j, 0)),
                  vec(CONV_W, D_A), vec(1, D_A),
                  pl.BlockSpec((2, D_A // 2, D_A), lambda i, j: (0, 0, 0)),
                  vec(1, D_A), vec(1, D_A), vec(1, D_A), vec(1, D_A)],
        out_specs=[row_spec(D_A), row_spec(D_B), row_spec(D_B), row_spec(D_B),
                   pl.BlockSpec((None, CONV_W - 1, D_A), lambda i, j: (i, 0, 0)),
                   pl.BlockSpec((None, 1, D_A), lambda i, j: (i, 0, 0))],
        out_shape=[jax.ShapeDtypeStruct((b, s, D_A), BF16),
                   jax.ShapeDtypeStruct((b, s, D_B), F32),
                   jax.ShapeDtypeStruct((b, s, D_B), F32),
                   jax.ShapeDtypeStruct((b, s, D_B), F32),
                   jax.ShapeDtypeStruct((b, CONV_W - 1, D_A), F32),
                   jax.ShapeDtypeStruct((b, 1, D_A), F32)],
        scratch_shapes=[pltpu.VMEM((tile + SUBLANES, D_A), F32),
                        pltpu.VMEM((1, D_A), F32)],
        compiler_params=_cparams(("arbitrary", "arbitrary")),
        name="inproj_prompt",
    )(x, mod3, w_in_bf, cos_t, sin_t, rg["conv_w"], rg["conv_b"], rg["w_gate"],
      rg["b_a"], rg["b_x"], rg["lam"], rg["g_norm_a"])
    return outs


def _attn_prompt_kernel(q_ref, k_ref, v_ref, o_ref, kwin_ref, vwin_ref, acc_s, m_s, l_s, bias_s):
    s = q_ref.shape[0]
    keep = kwin_ref.shape[0]
    kwin_ref[...] = k_ref[s - keep:s, :]
    vwin_ref[...] = v_ref[s - keep:s, :]

    lane = lax.broadcasted_iota(jnp.int32, (Q_BLK, HEAD_PAIR_W), 1)
    head0 = lane < HEAD_DIM
    nk = 2 * Q_BLK

    qi = lax.broadcasted_iota(jnp.int32, (Q_BLK, nk), 0)
    ki = lax.broadcasted_iota(jnp.int32, (Q_BLK, nk), 1)
    for slot in range(2):
        dist = slot * Q_BLK + qi - ki
        bias_s[slot] = jnp.where((dist >= 0) & (dist <= N_KEYS), 0.0, NEG_INF)

    def rows(start, n, d):
        return pl.ds(start, n) if d == 1 else pl.ds(start, n, stride=d)

    def unit(bi, d, nb, u):
        r = u // nb
        j = u % nb
        jk = jnp.maximum(j - 1, 0)
        start_q = r + d * Q_BLK * j
        start_k = r + d * Q_BLK * jk
        bias = bias_s[j - jk]
        qb = q_ref[rows(start_q, Q_BLK, d), :] * (HEAD_DIM ** -0.5)
        kb = k_ref[rows(start_k, nk, d), :].astype(BF16)
        vb = v_ref[rows(start_k, nk, d), :].astype(BF16)
        q2 = jnp.concatenate([jnp.where(head0, qb, 0.0), jnp.where(head0, 0.0, qb)],
                             axis=0).astype(BF16)
        sc = lax.dot_general(q2, kb, (((1,), (1,)), ((), ())), preferred_element_type=F32)
        sc = sc + jnp.concatenate([bias, bias], axis=0)
        m = jnp.max(sc, axis=-1, keepdims=True)
        p = jnp.exp(sc - m)
        l = jnp.sum(p, axis=-1, keepdims=True)
        o = jnp.dot(p.astype(BF16), vb, preferred_element_type=F32)
        dst = rows(start_q, Q_BLK, d)
        acc_s[bi, dst, :] = jnp.where(head0, o[0:Q_BLK], o[Q_BLK:])
        m_s[bi, dst, :] = jnp.where(head0, m[0:Q_BLK], m[Q_BLK:])
        l_s[bi, dst, :] = jnp.where(head0, l[0:Q_BLK], l[Q_BLK:])

    for bi, (_, d) in enumerate(DILATED_CFG):
        nb = s // d // Q_BLK

        def body(u, carry, bi=bi, d=d, nb=nb):
            unit(bi, d, nb, u)
            return carry

        lax.fori_loop(0, d * nb, body, 0, unroll=ATTN_UNROLL)

    chunk = 512

    def merge(i, carry):
        r0 = pl.multiple_of(i * chunk, chunk)
        sl = pl.ds(r0, chunk)
        ms = [m_s[bi, sl, :] for bi in range(len(DILATED_CFG))]
        mx = functools.reduce(jnp.maximum, ms)
        num = jnp.zeros((chunk, HEAD_PAIR_W), F32)
        den = jnp.zeros((chunk, HEAD_PAIR_W), F32)
        for bi in range(len(DILATED_CFG)):
            w = jnp.exp(ms[bi] - mx)
            num = num + w * acc_s[bi, sl, :]
            den = den + w * l_s[bi, sl, :]
        o_ref[sl, :] = num / den
        return carry

    lax.fori_loop(0, s // chunk, merge, 0)


def _attn_prompt(q, k, v):
    b, s, _ = q.shape
    keep = min(WIN_MAX, s)
    nbr = len(DILATED_CFG)
    spec = pl.BlockSpec((None, s, HEAD_PAIR_W), lambda i, j: (i, 0, j))
    wspec = pl.BlockSpec((None, keep, HEAD_PAIR_W), lambda i, j: (i, 0, j))
    return pl.pallas_call(
        _attn_prompt_kernel,
        grid=(b, N_HEAD_PAIRS),
        in_specs=[spec, spec, spec],
        out_specs=[spec, wspec, wspec],
        out_shape=[jax.ShapeDtypeStruct((b, s, D_B), F32),
                   jax.ShapeDtypeStruct((b, keep, D_B), F32),
                   jax.ShapeDtypeStruct((b, keep, D_B), F32)],
        scratch_shapes=[pltpu.VMEM((nbr, s, HEAD_PAIR_W), F32),
                        pltpu.VMEM((nbr, s, HEAD_PAIR_W), F32),
                        pltpu.VMEM((nbr, s, HEAD_PAIR_W), F32),
                        pltpu.VMEM((2, Q_BLK, 2 * Q_BLK), F32)],
        compiler_params=_cparams(("arbitrary", "arbitrary")),
        name="attn_prompt",
    )(q, k, v)


def _inproj_sample_kernel(x_ref, mod_ref, w_in_ref, cos_ref, sin_ref, conv_state_ref, h0_ref,
                          conv_w_ref, conv_b_ref, wg_ref, b_a_ref, b_x_ref, lam_ref, gna_ref,
                          ya_ref, q_ref, k_ref, v_ref, conv_out_ref, h_out_ref):
    nt, nb, _ = x_ref.shape
    sh1 = mod_ref[:, 0:D_MODEL]
    sc1 = mod_ref[:, D_MODEL:2 * D_MODEL]
    u = (x_ref[...] * (1.0 + sc1)[None] + sh1[None]).astype(BF16).reshape(nt * nb, D_MODEL)

    def proj(j):
        return jnp.dot(u, w_in_ref[:, j * D_A:(j + 1) * D_A], preferred_element_type=F32)

    cos = cos_ref[...].reshape(nt * nb, D_B)
    sin = sin_ref[...].reshape(nt * nb, D_B)
    q_ref[...] = _rope_apply(proj(2), cos, sin).reshape(nt, nb, D_B)
    k_ref[...] = _rope_apply(proj(3), cos, sin).reshape(nt, nb, D_B)
    v_ref[...] = proj(4).reshape(nt, nb, D_B)

    xa = proj(0).reshape(nt, nb, D_A)
    xp = [conv_state_ref[j] for j in range(CONV_W - 1)] + [xa[t] for t in range(nt)]
    xc = jnp.concatenate(
        [conv_b_ref[...] + sum(xp[t + j] * conv_w_ref[j:j + 1, :] for j in range(CONV_W))
         for t in range(nt)], axis=0)
    for j in range(CONV_W - 1):
        conv_out_ref[j] = xp[nt + j]

    a, u_in = _rglru_gates(xc, wg_ref, b_a_ref[...], b_x_ref[...], lam_ref[...])
    h = h0_ref[...]
    hs = []
    for t in range(nt):
        h = a[t * nb:(t + 1) * nb] * h + u_in[t * nb:(t + 1) * nb]
        hs.append(h)
    h_out_ref[...] = h
    y = jnp.concatenate(hs, axis=0) * jax.nn.gelu(proj(1))
    ya_ref[...] = _rms_norm(y, gna_ref[...]).astype(BF16).reshape(nt, nb, D_A)


def _inproj_sample(x_tb, mod_s, w_in_bf, cos_t, sin_t, conv_state_tb, h0, rg):
    nt, nb, _ = x_tb.shape
    full = lambda shape: pl.BlockSpec(shape, lambda i: (0,) * len(shape))
    return pl.pallas_call(
        _inproj_sample_kernel,
        grid=(1,),
        in_specs=[full((nt, nb, D_MODEL)), full((nb, 6 * D_MODEL)), full((D_MODEL, 5 * D_A)),
                  full((nt, nb, D_B)), full((nt, nb, D_B)),
                  full((CONV_W - 1, nb, D_A)), full((nb, D_A)),
                  full((CONV_W, D_A)), full((1, D_A)), full((2, D_A // 2, D_A)),
                  full((1, D_A)), full((1, D_A)), full((1, D_A)), full((1, D_A))],
        out_specs=[full((nt, nb, D_A)), full((nt, nb, D_B)), full((nt, nb, D_B)),
                   full((nt, nb, D_B)), full((CONV_W - 1, nb, D_A)), full((nb, D_A))],
        out_shape=[jax.ShapeDtypeStruct((nt, nb, D_A), BF16),
                   jax.ShapeDtypeStruct((nt, nb, D_B), F32),
                   jax.ShapeDtypeStruct((nt, nb, D_B), F32),
                   jax.ShapeDtypeStruct((nt, nb, D_B), F32),
                   jax.ShapeDtypeStruct((CONV_W - 1, nb, D_A), F32),
                   jax.ShapeDtypeStruct((nb, D_A), F32)],
        compiler_params=_cparams(("arbitrary",)),
        name="inproj_sample",
    )(x_tb, mod_s, w_in_bf, cos_t, sin_t, conv_state_tb, h0, rg["conv_w"], rg["conv_b"],
      rg["w_gate"], rg["b_a"], rg["b_x"], rg["lam"], rg["g_norm_a"])


def _sample_attention_step(b, n_seq, q_ref, kn_ref, vn_ref, ck_hbm, cv_hbm, o_ref, kwin_hbm,
                           vwin_hbm, cin, cout, sem_in, sem_out, kn_pad, vn_pad):
    nt = q_ref.shape[0]
    n_buf = cin.shape[-1]
    n_rows = N_HEADS_B * nt
    srcs = (ck_hbm, cv_hbm)
    dsts = (kwin_hbm, vwin_hbm)
    slot = b % 2

    def copy_in(which, seq):
        s = seq % 2
        return pltpu.make_async_copy(srcs[which].at[seq], cin.at[s, which], sem_in.at[s, which])

    def copy_out(which, seq):
        s = seq % 2
        return pltpu.make_async_copy(cout.at[s, which], dsts[which].at[seq],
                                     sem_out.at[s, which])

    @pl.when(b == 0)
    def _():
        kn_pad[...] = jnp.zeros_like(kn_pad)
        vn_pad[...] = jnp.zeros_like(vn_pad)
        copy_in(0, 0).start()
        copy_in(1, 0).start()

    @pl.when(b + 1 < n_seq)
    def _():
        copy_in(0, b + 1).start()
        copy_in(1, b + 1).start()

    kn_pad[0:nt, :] = kn_ref[...]
    vn_pad[0:nt, :] = vn_ref[...]

    tail_lane = lax.broadcasted_iota(jnp.int32, (HEAD_DIM, LANES), 1)

    def shift_in(old_ref, new_pad, out_ref):
        new_t = pltpu.roll(new_pad.T, LANES - nt, 1)
        for h in range(N_HEADS_B):
            rolled = pltpu.roll(old_ref[h], n_buf - nt, 1)
            last = jnp.where(tail_lane < LANES - nt, rolled[:, n_buf - LANES:n_buf],
                             new_t[h * HEAD_DIM:(h + 1) * HEAD_DIM])
            out_ref[h, :, 0:n_buf - LANES] = rolled[:, 0:n_buf - LANES]
            out_ref[h, :, n_buf - LANES:n_buf] = last

    copy_in(0, b).wait()
    copy_in(1, b).wait()

    @pl.when(b >= 2)
    def _():
        copy_out(0, b - 2).wait()
        copy_out(1, b - 2).wait()

    ri = lax.broadcasted_iota(jnp.int32, (n_rows, nt), 0)
    ci = lax.broadcasted_iota(jnp.int32, (n_rows, nt), 1)
    pick = (ri % nt == ci).astype(BF16)
    qs = (q_ref[...] * (HEAD_DIM ** -0.5)).astype(BF16)
    q_rep = jnp.dot(pick, qs, preferred_element_type=F32)
    row_h = lax.broadcasted_iota(jnp.int32, (n_rows, D_B), 0) // nt
    lane_h = lax.broadcasted_iota(jnp.int32, (n_rows, D_B), 1) // HEAD_DIM
    own = row_h == lane_h
    qbd = jnp.where(own, q_rep, 0.0).astype(BF16)

    nt_dims = (((1,), (1,)), ((), ()))

    def mult(dist, limit_ok):
        c = jnp.zeros(dist.shape, F32)
        for win, d in DILATED_CFG:
            hit = (dist >= 0) & (dist <= win) & (dist % d == 0) & limit_ok
            c = c + hit.astype(F32)
        return c

    def weights(ck_t):
        sc_c = jnp.dot(qbd, ck_t.astype(BF16), preferred_element_type=F32)
        sc_n = lax.dot_general(qbd, kn_pad[...].astype(BF16), nt_dims,
                               preferred_element_type=F32)
        t_c = lax.broadcasted_iota(jnp.int32, sc_c.shape, 0) % nt
        dist_c = n_buf + t_c - lax.broadcasted_iota(jnp.int32, sc_c.shape, 1)
        mult_c = mult(dist_c, dist_c >= 0)
        t_n = lax.broadcasted_iota(jnp.int32, sc_n.shape, 0) % nt
        col_n = lax.broadcasted_iota(jnp.int32, sc_n.shape, 1)
        mult_n = mult(t_n - col_n, col_n < nt)
        sc_c = jnp.where(mult_c > 0, sc_c, NEG_INF)
        sc_n = jnp.where(mult_n > 0, sc_n, NEG_INF)
        m = jnp.maximum(jnp.max(sc_c, axis=-1, keepdims=True),
                        jnp.max(sc_n, axis=-1, keepdims=True))
        p_c = mult_c * jnp.exp(sc_c - m)
        p_n = mult_n * jnp.exp(sc_n - m)
        l = jnp.sum(p_c, axis=-1, keepdims=True) + jnp.sum(p_n, axis=-1, keepdims=True)
        return p_c.astype(BF16), p_n.astype(BF16), l

    p_c, p_n, l = weights(cin[slot, 0].reshape(D_B, n_buf))
    shift_in(cin.at[slot, 0], kn_pad[...], cout.at[slot, 0])
    acc = (lax.dot_general(p_c, cin[slot, 1].reshape(D_B, n_buf).astype(BF16), nt_dims,
                           preferred_element_type=F32)
           + jnp.dot(p_n, vn_pad[...].astype(BF16), preferred_element_type=F32))
    shift_in(cin.at[slot, 1], vn_pad[...], cout.at[slot, 1])
    o_full = jnp.where(own, acc / l, 0.0)
    out = o_full[0:nt, :]
    for h in range(1, N_HEADS_B):
        out = out + o_full[h * nt:(h + 1) * nt, :]
    o_ref[...] = out

    copy_out(0, b).start()
    copy_out(1, b).start()

    @pl.when(b == n_seq - 1)
    def _():
        @pl.when(b >= 1)
        def _():
            copy_out(0, b - 1).wait()
            copy_out(1, b - 1).wait()
        copy_out(0, b).wait()
        copy_out(1, b).wait()


def _split_bf16(x):
    hi = x.astype(BF16)
    lo = (x - hi.astype(F32)).astype(BF16)
    return hi, lo


def _first_argmax(vals):
    mx = functools.reduce(jnp.maximum, vals)
    idx = jnp.full(mx.shape, float(len(vals) - 1), F32)
    for j in range(len(vals) - 2, -1, -1):
        idx = jnp.where(vals[j] == mx, float(j), idx)
    return mx, idx


def _outproj_kernel(ya_ref, yb_ref, x_ref, gt1_ref, sh2_ref, sc2_ref, w_out_ref, gnb_ref,
                    ln_g_ref, ln_b_ref, w_r_ref, b_r_ref, x1_ref, u2_ref, route_ref):
    yb = _rms_norm(yb_ref[...], gnb_ref[...]).astype(BF16)
    mixed = (jnp.dot(ya_ref[...], w_out_ref[0:D_A, :], preferred_element_type=F32)
             + jnp.dot(yb, w_out_ref[D_A:D_A + D_B, :], preferred_element_type=F32))
    x1 = _layer_norm(DN_ALPHA * x_ref[...] + gt1_ref[...] * mixed, ln_g_ref[...], ln_b_ref[...])
    x1_ref[...] = x1
    u2 = x1 * (1.0 + sc2_ref[...]) + sh2_ref[...]
    _store_token_tiles(u2_ref, u2, TOKEN_PITCH)

    u_hi, u_lo = _split_bf16(u2)
    w_hi, w_lo = _split_bf16(w_r_ref[...])
    both = jnp.dot(u_hi, jnp.concatenate([w_hi, w_lo], axis=1), preferred_element_type=F32)
    logits = (both[:, 0:ROUTE_W] + jnp.dot(u_lo, w_hi, preferred_element_type=F32)
              + both[:, ROUTE_W:2 * ROUTE_W]) + b_r_ref[...]
    lt = logits.T

    g_rows = [lt[j:j + 1, :] for j in range(N_GROUPS)]
    g_max, g_idx = _first_argmax(g_rows)
    p_group = 1.0 / sum(jnp.exp(g - g_max) for g in g_rows)
    e_rows = []
    for e in range(N_EXP_PER_GROUP):
        acc = jnp.zeros_like(g_max)
        for g in range(N_GROUPS):
            r = N_GROUPS + g * N_EXP_PER_GROUP + e
            acc = jnp.where(g_idx == float(g), lt[r:r + 1, :], acc)
        e_rows.append(acc)
    v1, i1 = _first_argmax(e_rows)
    rest = [jnp.where(i1 == float(e), -jnp.inf, e_rows[e]) for e in range(N_EXP_PER_GROUP)]
    v2, i2 = _first_argmax(rest)
    ex = jnp.exp(v2 - v1)
    w1 = p_group / (1.0 + ex)
    w2 = p_group * ex / (1.0 + ex)
    lo = jnp.minimum(i1, i2)
    hi = jnp.maximum(i1, i2)
    pair = jnp.where(lo == 0.0, hi - 1.0, jnp.where(lo == 1.0, hi + 1.0, 5.0))
    cls = g_idx * float(N_PAIRS) + pair
    w_of_lo = jnp.where(i1 < i2, w1, w2)
    w_of_hi = jnp.where(i1 < i2, w2, w1)
    e_lo = g_idx * float(N_EXP_PER_GROUP) + lo
    e_hi = g_idx * float(N_EXP_PER_GROUP) + hi
    n_tok = cls.shape[1]
    route = jnp.concatenate(
        [cls, w_of_lo, w_of_hi, e_lo, e_hi, jnp.zeros((ROUTE_W - 5, n_tok), F32)], axis=0)
    route_ref[...] = route[0:SUBLANES, :]
    u2_ref[pl.ds(TOKEN_ROWS, n_tok, stride=TOKEN_PITCH), :] = route.T


def _outproj(ya, yb, x, mods, w_out_bf, gnb, ln_g, ln_b, w_r, b_r, tile):
    n = x.shape[0]
    nt = n // tile
    mod_arr, gt1_spec, sh2_spec, sc2_spec = mods
    row = lambda w: pl.BlockSpec((tile, w), lambda i: (i, 0))
    vec = lambda r, w: pl.BlockSpec((r, w), lambda i: (0, 0))
    return pl.pallas_call(
        _outproj_kernel,
        grid=(nt,),
        in_specs=[row(D_A), row(D_B), row(D_MODEL), gt1_spec, sh2_spec, sc2_spec,
                  vec(D_MODEL, D_MODEL), vec(1, D_B), vec(1, D_MODEL), vec(1, D_MODEL),
                  vec(D_MODEL, ROUTE_W), vec(1, ROUTE_W)],
        out_specs=[row(D_MODEL),
                   pl.BlockSpec((tile * TOKEN_PITCH, LANES), lambda i: (i, 0)),
                   pl.BlockSpec((None, SUBLANES, tile), lambda i: (i, 0, 0))],
        out_shape=[jax.ShapeDtypeStruct((n, D_MODEL), F32),
                   jax.ShapeDtypeStruct((n * TOKEN_PITCH, LANES), F32),
                   jax.ShapeDtypeStruct((nt, SUBLANES, tile), F32)],
        compiler_params=_cparams(("arbitrary",)),
        name="outproj_router",
    )(ya, yb, x, mod_arr, mod_arr, mod_arr, w_out_bf, gnb, ln_g, ln_b, w_r, b_r)


def _moe_kernel(e_lo_ref, e_hi_ref, n_used_ref, src_ref, dst_ref,
                x_hbm, wg_lo, wg_hi, wu_lo, wu_hi, wd_lo, wd_hi,
                q_ref, kn_ref, vn_ref, ck_hbm, cv_hbm,
                o_hbm, yb_ref, kwin_hbm, vwin_hbm,
                xg, og, gsem, ssem, cin, cout, csem_in, csem_out, kn_pad, vn_pad,
                *, n_tokens, tile, n_seq):
    i = pl.program_id(0)
    n_steps = pl.num_programs(0)
    n_used = n_used_ref[0]
    cur = i % N_BUF
    ahead = (i + 2) % N_BUF
    rows = tile * TOKEN_ROWS

    def gather_token(base, r, buf):
        pltpu.make_async_copy(x_hbm.at[pl.ds(src_ref[base + r], TOKEN_PITCH)],
                              xg.at[buf, pl.ds(r * TOKEN_PITCH, TOKEN_PITCH)],
                              gsem.at[buf]).start()

    def gather_wait(buf):
        pltpu.make_async_copy(x_hbm.at[pl.ds(0, tile * TOKEN_PITCH)], xg.at[buf],
                              gsem.at[buf]).wait()

    def scatter_wait(buf):
        pltpu.make_async_copy(og.at[buf], o_hbm.at[pl.ds(0, rows)], ssem.at[buf]).wait()

    @pl.when(i == 0)
    def _():
        def first(r, c):
            gather_token(0, r, 0)
            gather_token(jnp.minimum(1, n_steps - 1) * tile, r, 1)
            return c
        lax.fori_loop(0, tile, first, 0, unroll=DMA_UNROLL)
        og[...] = jnp.zeros_like(og)
        for buf in range(N_BUF):
            pad_rows = pltpu.make_async_copy(
                og.at[buf], o_hbm.at[pl.ds((n_tokens + buf * tile) * TOKEN_ROWS, rows)],
                ssem.at[buf])
            pad_rows.start()
            pad_rows.wait()

    def expert_tile():
        nxt = jnp.minimum(i + 2, n_steps - 1) * tile
        for r in range(tile):
            gather_token(nxt, r, ahead)
        prev = i * tile
        for r in range(tile):
            row = pl.multiple_of(dst_ref[prev + r], TOKEN_ROWS)
            pltpu.make_async_copy(og.at[ahead, pl.ds(r * TOKEN_ROWS, TOKEN_ROWS)],
                                  o_hbm.at[pl.ds(row, TOKEN_ROWS)], ssem.at[ahead]).start()

        xb = _load_token_tiles(xg.at[cur], tile, TOKEN_PITCH).astype(BF16)
        w2 = xg[cur, pl.ds(TOKEN_ROWS, tile, stride=TOKEN_PITCH), :]
        out = jnp.zeros((tile, D_MODEL), F32)
        for col, wg, wu, wd in ((1, wg_lo, wu_lo, wd_lo), (2, wg_hi, wu_hi, wd_hi)):
            hg = jnp.dot(xb, wg[...], preferred_element_type=F32)
            hu = jnp.dot(xb, wu[...], preferred_element_type=F32)
            act = (hg * jax.nn.sigmoid(hg)) * hu * w2[:, col:col + 1]
            out = out + jnp.dot(act.astype(BF16), wd[...], preferred_element_type=F32)
        _store_token_tiles(og.at[cur], out)

    @pl.when(i <= n_used)
    def _():
        gather_wait(cur)

        @pl.when(i >= 2)
        def _():
            scatter_wait(cur)

        expert_tile()

        @pl.when(i == n_used)
        def _():
            for buf in range(N_BUF):
                @pl.when(buf != cur)
                def _():
                    gather_wait(buf)
                    scatter_wait(buf)

    @pl.when(i < n_seq)
    def _():
        _sample_attention_step(i, n_seq, q_ref, kn_ref, vn_ref, ck_hbm, cv_hbm, yb_ref,
                               kwin_hbm, vwin_hbm, cin, cout, csem_in, csem_out, kn_pad, vn_pad)


def _moe(u2t, cls, w_gate_bf, w_up_bf, w_down_bf, tile, q_s, k_new, v_new, cache_k_t, cache_v_t):
    n = cls.shape[0]
    n_seq, nt, _ = q_s.shape
    n_buf = cache_k_t.shape[-1]
    n_steps = n // tile + N_CLASSES
    assert n_steps >= n_seq, "one sample sequence per grid step"
    cls = cls.astype(jnp.int32)
    order = jnp.argsort(cls, stable=True).astype(jnp.int32)
    class_ids = jnp.arange(N_CLASSES, dtype=jnp.int32)
    counts = jnp.sum((cls[:, None] == class_ids[None, :]).astype(jnp.int32), axis=0)
    tiles_per = (counts + tile - 1) // tile
    tile_end = jnp.cumsum(tiles_per)
    tile_off = tile_end - tiles_per
    n_used = tile_end[-1]
    class_start = jnp.cumsum(counts) - counts
    step = jnp.arange(n_steps, dtype=jnp.int32)
    step_c = jnp.minimum(step, n_used - 1)
    cls_of = jnp.sum((step_c[:, None] >= tile_end[None, :]).astype(jnp.int32), axis=1)
    onehot = (cls_of[:, None] == class_ids[None, :]).astype(jnp.int32)
    pick = lambda table: jnp.sum(onehot * table[None, :], axis=1)
    local = step - pick(tile_off)
    nvalid = jnp.where(step < n_used, jnp.clip(pick(counts) - local * tile, 0, tile), 0)
    r = jnp.arange(tile, dtype=jnp.int32)
    pos = pick(class_start)[:, None] + local[:, None] * tile + r[None, :]
    valid = r[None, :] < nvalid[:, None]
    tok = order[jnp.clip(pos, 0, n - 1)]
    src = (jnp.where(valid, tok, 0) * TOKEN_PITCH).astype(jnp.int32).reshape(-1)
    spare = n + (step[:, None] % N_BUF) * tile + r[None, :]
    dst = jnp.where(valid, tok, spare)
    dst = jnp.concatenate([(n + (N_BUF - 1) * tile + r)[None, :], dst], axis=0)
    dst = (dst * TOKEN_ROWS).astype(jnp.int32).reshape(-1)
    grp = cls_of // N_PAIRS
    pair = cls_of % N_PAIRS
    pair_lo = (pair >= 3).astype(jnp.int32) + (pair >= 5).astype(jnp.int32)
    pair_hi = pair + 1 - 2 * (pair >= 3).astype(jnp.int32) - (pair >= 5).astype(jnp.int32)
    e_lo = (grp * N_EXP_PER_GROUP + pair_lo).astype(jnp.int32)
    e_hi = (grp * N_EXP_PER_GROUP + pair_hi).astype(jnp.int32)

    w_in_spec = lambda which: pl.BlockSpec(
        (None, D_MODEL, D_EXPERT), lambda i, elo, ehi, nu, s, d: ((elo, ehi)[which][i], 0, 0))
    w_dn_spec = lambda which: pl.BlockSpec(
        (None, D_EXPERT, D_MODEL), lambda i, elo, ehi, nu, s, d: ((elo, ehi)[which][i], 0, 0))
    any_spec = pl.BlockSpec(memory_space=pl.ANY)
    seq_spec = pl.BlockSpec((None, nt, D_B),
                            lambda i, elo, ehi, nu, s, d: (jnp.minimum(i, n_seq - 1), 0, 0))
    win_shape = jax.ShapeDtypeStruct((n_seq, N_HEADS_B, HEAD_DIM, n_buf), F32)
    grid_spec = pltpu.PrefetchScalarGridSpec(
        num_scalar_prefetch=5,
        grid=(n_steps,),
        in_specs=[any_spec,
                  w_in_spec(0), w_in_spec(1), w_in_spec(0), w_in_spec(1),
                  w_dn_spec(0), w_dn_spec(1),
                  seq_spec, seq_spec, seq_spec, any_spec, any_spec],
        out_specs=[any_spec, seq_spec, any_spec, any_spec],
        scratch_shapes=[pltpu.VMEM((N_BUF, tile * TOKEN_PITCH, LANES), F32),
                        pltpu.VMEM((N_BUF, tile * TOKEN_ROWS, LANES), F32),
                        pltpu.SemaphoreType.DMA((N_BUF,)),
                        pltpu.SemaphoreType.DMA((N_BUF,)),
                        pltpu.VMEM((2, 2, N_HEADS_B, HEAD_DIM, n_buf), F32),
                        pltpu.VMEM((2, 2, N_HEADS_B, HEAD_DIM, n_buf), F32),
                        pltpu.SemaphoreType.DMA((2, 2)),
                        pltpu.SemaphoreType.DMA((2, 2)),
                        pltpu.VMEM((LANES, D_B), F32),
                        pltpu.VMEM((LANES, D_B), F32)],
    )
    return pl.pallas_call(
        functools.partial(_moe_kernel, n_tokens=n, tile=tile, n_seq=n_seq),
        grid_spec=grid_spec,
        out_shape=[jax.ShapeDtypeStruct(((n + N_BUF * tile) * TOKEN_ROWS, LANES), F32),
                   jax.ShapeDtypeStruct((n_seq, nt, D_B), F32), win_shape, win_shape],
        compiler_params=_cparams(("arbitrary",)),
        name="moe_sparse",
    )(e_lo, e_hi, n_used.reshape(1).astype(jnp.int32), src, dst,
      u2t, w_gate_bf, w_gate_bf, w_up_bf, w_up_bf, w_down_bf, w_down_bf,
      q_s, k_new, v_new, cache_k_t, cache_v_t)


def _moe_dense_kernel(x_ref, wg_ref, wu_ref, wd_ref, o_ref):
    e = pl.program_id(0)
    n = o_ref.shape[0]

    @pl.when(e == 0)
    def _():
        o_ref[...] = jnp.zeros_like(o_ref)

    xb = _load_token_tiles(x_ref, n, TOKEN_PITCH).astype(BF16)
    ef = e.astype(F32)
    route = x_ref[pl.ds(TOKEN_ROWS, n, stride=TOKEN_PITCH), :]
    col = lambda c: route[:, c:c + 1]
    comb = jnp.where(col(3) == ef, col(1), 0.0) + jnp.where(col(4) == ef, col(2), 0.0)
    hg = jnp.dot(xb, wg_ref[...], preferred_element_type=F32)
    hu = jnp.dot(xb, wu_ref[...], preferred_element_type=F32)
    act = (hg * jax.nn.sigmoid(hg)) * hu * comb
    o_ref[...] += jnp.dot(act.astype(BF16), wd_ref[...], preferred_element_type=F32)


def _moe_dense(u2t, w_gate_bf, w_up_bf, w_down_bf):
    n = u2t.shape[0] // TOKEN_PITCH
    return pl.pallas_call(
        _moe_dense_kernel,
        grid=(N_EXPERTS,),
        in_specs=[pl.BlockSpec((n * TOKEN_PITCH, LANES), lambda e: (0, 0)),
                  pl.BlockSpec((None, D_MODEL, D_EXPERT), lambda e: (e, 0, 0)),
                  pl.BlockSpec((None, D_MODEL, D_EXPERT), lambda e: (e, 0, 0)),
                  pl.BlockSpec((None, D_EXPERT, D_MODEL), lambda e: (e, 0, 0))],
        out_specs=pl.BlockSpec((n, D_MODEL), lambda e: (0, 0)),
        out_shape=jax.ShapeDtypeStruct((n, D_MODEL), F32),
        compiler_params=_cparams(("arbitrary",)),
        name="moe_dense",
    )(u2t, w_gate_bf, w_up_bf, w_down_bf)


def _final_kernel(x1_ref, ffn_ref, gt2_ref, g_ref, b_ref, o_ref, *, token_tiled):
    rows = x1_ref.shape[0]
    ffn = _load_token_tiles(ffn_ref, rows) if token_tiled else ffn_ref[...]
    o_ref[...] = _layer_norm(DN_ALPHA * x1_ref[...] + gt2_ref[...] * ffn, g_ref[...], b_ref[...])


def _final_norm(x1, ffn, mod_arr, gt2_spec, ln_g, ln_b, tile):
    n = x1.shape[0]
    token_tiled = ffn.shape[-1] == LANES
    row = pl.BlockSpec((tile, D_MODEL), lambda i: (i, 0))
    ffn_spec = pl.BlockSpec((tile * TOKEN_ROWS, LANES), lambda i: (i, 0)) if token_tiled else row
    vec = pl.BlockSpec((1, D_MODEL), lambda i: (0, 0))
    return pl.pallas_call(
        functools.partial(_final_kernel, token_tiled=token_tiled),
        grid=(n // tile,),
        in_specs=[row, ffn_spec, gt2_spec, vec, vec],
        out_specs=row,
        out_shape=jax.ShapeDtypeStruct((n, D_MODEL), F32),
        compiler_params=_cparams(("arbitrary",)),
        name="final_norm",
    )(x1, ffn, mod_arr, ln_g, ln_b)


def _rope_tables(pos):
    half = HEAD_DIM // 2
    inv = ROPE_THETA ** (-np.arange(half, dtype=np.float64) * 2.0 / HEAD_DIM)
    ang = np.asarray(pos, np.float64)[:, None] * inv[None, :]
    cos = np.cos(ang)
    sin = np.sin(ang)
    cos_t = np.tile(np.concatenate([cos, cos], axis=-1), (1, 2))
    sin_t = np.tile(np.concatenate([-sin, sin], axis=-1), (1, 2))
    return jnp.asarray(cos_t, F32), jnp.asarray(sin_t, F32)


def _block_diag(w):
    n, a, b = w.shape
    eye = jnp.eye(n, dtype=w.dtype)
    return (eye[:, None, :, None] * w[:, :, None, :]).reshape(n * a, n * b)


def _prepare_weights(w_in, conv_w, conv_b, w_rg_a, b_rg_a, w_rg_x, b_rg_x, rg_lambda, g_norm_a,
                     w_router_group, b_router_group, w_router_expert, b_router_expert):
    half_blocks = N_BLK_A // 2
    w_gate = jnp.stack([
        jnp.concatenate([_block_diag(w_rg_a[h * half_blocks:(h + 1) * half_blocks]),
                         _block_diag(w_rg_x[h * half_blocks:(h + 1) * half_blocks])], axis=1)
        for h in range(2)]).astype(BF16)
    rg = dict(conv_w=conv_w, conv_b=conv_b.reshape(1, D_A), w_gate=w_gate,
              b_a=b_rg_a.reshape(1, D_A), b_x=b_rg_x.reshape(1, D_A),
              lam=rg_lambda.reshape(1, D_A), g_norm_a=g_norm_a.reshape(1, D_A))
    n_logits = N_GROUPS + N_EXPERTS
    w_r = jnp.concatenate(
        [w_router_group,
         w_router_expert.transpose(1, 0, 2).reshape(D_MODEL, N_EXPERTS),
         jnp.zeros((D_MODEL, ROUTE_W - n_logits), F32)], axis=1)
    b_r = jnp.concatenate([b_router_group, b_router_expert.reshape(-1),
                           jnp.zeros((ROUTE_W - n_logits,), F32)]).reshape(1, ROUTE_W)
    return rg, w_r, b_r


def kernel(x_prompt, x_sample, state_conv, state_rglru, cache_win_k, cache_win_v, c_prompt, c_sample, w_ada, b_ada, w_in, conv_w, conv_b, w_rg_a, b_rg_a, w_rg_x, b_rg_x, rg_lambda, g_norm_a, g_norm_b, w_out, ln1_g, ln1_b, w_router_group, b_router_group, w_router_expert, b_router_expert, w_exp_gate, w_exp_up, w_exp_down, ln2_g, ln2_b):
    bp, sp, _ = x_prompt.shape
    bs, ts, _ = x_sample.shape
    n_buf = cache_win_k.shape[1]

    rg, w_r, b_r = _prepare_weights(w_in, conv_w, conv_b, w_rg_a, b_rg_a, w_rg_x, b_rg_x,
                                    rg_lambda, g_norm_a, w_router_group, b_router_group,
                                    w_router_expert, b_router_expert)
    w_in_bf = w_in.astype(BF16)
    outproj_weights = (w_out.astype(BF16), g_norm_b.reshape(1, D_B), ln1_g.reshape(1, D_MODEL),
                       ln1_b.reshape(1, D_MODEL), w_r, b_r)
    expert_weights = (w_exp_gate.astype(BF16), w_exp_up.astype(BF16), w_exp_down.astype(BF16))
    ln2 = (ln2_g.reshape(1, D_MODEL), ln2_b.reshape(1, D_MODEL))

    mod = _modulation(jnp.concatenate([c_prompt, c_sample], axis=0), w_ada, b_ada)
    mod_p, mod_s = mod[:bp], mod[bp:]

    tile_p = 512
    cos_p, sin_p = _rope_tables(np.arange(sp))
    ya_p, q_p, k_p, v_p, conv_p, h_p = _inproj_prompt(
        x_prompt, mod_p.reshape(bp, 6, D_MODEL), w_in_bf, cos_p, sin_p, rg)
    yb_p, kwin_p, vwin_p = _attn_prompt(q_p, k_p, v_p)
    tiles_per_seq = sp // tile_p
    mod_p3 = mod_p.reshape(bp * 6, 1, D_MODEL)
    mod_spec_p = lambda j: pl.BlockSpec((None, 1, D_MODEL),
                                        lambda i: ((i // tiles_per_seq) * 6 + j, 0, 0))
    x1_p, u2t_p, route_p = _outproj(
        ya_p.reshape(bp * sp, D_A), yb_p.reshape(bp * sp, D_B), x_prompt.reshape(bp * sp, D_MODEL),
        (mod_p3, mod_spec_p(2), mod_spec_p(3), mod_spec_p(4)), *outproj_weights, tile_p)

    cos_s, sin_s = _rope_tables(PAST_LEN + np.arange(ts))
    tb = lambda t: jnp.broadcast_to(jnp.tile(t, (1, N_HEAD_PAIRS))[:, None, :], (ts, bs, D_B))
    ya_s, q_s, k_s, v_s, conv_s, h_s = _inproj_sample(
        x_sample.transpose(1, 0, 2), mod_s, w_in_bf, tb(cos_s), tb(sin_s),
        state_conv.transpose(1, 0, 2), state_rglru, rg)
    bt = lambda t: t.transpose(1, 0, 2)

    ffn_p, yb_s, kwin_s, vwin_s = _moe(
        u2t_p, route_p[:, 0, :].reshape(-1), *expert_weights, 256,
        bt(q_s), bt(k_s), bt(v_s), cache_win_k.transpose(0, 2, 3, 1),
        cache_win_v.transpose(0, 2, 3, 1))
    y_p = _final_norm(x1_p, ffn_p, mod_p3, mod_spec_p(5), *ln2, tile_p)

    mod_spec_s = lambda j: pl.BlockSpec((bs, D_MODEL), lambda i: (0, j))
    x1_s, u2t_s, _ = _outproj(
        ya_s.reshape(ts * bs, D_A), bt(yb_s).reshape(ts * bs, D_B),
        x_sample.transpose(1, 0, 2).reshape(ts * bs, D_MODEL),
        (mod_s, mod_spec_s(2), mod_spec_s(3), mod_spec_s(4)), *outproj_weights, bs)
    ffn_s = _moe_dense(u2t_s, *expert_weights)
    y_s = _final_norm(x1_s, ffn_s, mod_s, mod_spec_s(5), *ln2, bs)

    heads = lambda t: t.reshape(t.shape[0], t.shape[1], N_HEADS_B, HEAD_DIM)
    return (y_p.reshape(bp, sp, D_MODEL), bt(y_s.reshape(ts, bs, D_MODEL)),
            conv_p, h_p.reshape(bp, D_A), heads(kwin_p), heads(vwin_p),
            bt(conv_s), h_s, kwin_s.transpose(0, 3, 1, 2), vwin_s.transpose(0, 3, 1, 2))
```
